```python
import math
import jax, jax.numpy as jnp
from jax import lax
import numpy as np

D_MODEL = 2048
BATCH = 2
SEQ = 4096
DEPTH = 2
DEC_BATCH = 128
DEC_SEQ = 4
PAST_LEN = 2048
PAGE_SIZE = 128

N_Q_HEADS = 8
N_KV_HEADS = 4
HEAD_DIM = 128
Q_PER_KV = N_Q_HEADS // N_KV_HEADS
D_ATT = N_Q_HEADS * HEAD_DIM
D_KV = N_KV_HEADS * HEAD_DIM
DILATED_BRANCHES = ((128, 1), (512, 4), (2048, 16))
ATT_WINDOW = 2048
ROPE_THETA = 10000.0
ATT_SCALE = HEAD_DIM ** -0.5
N_SSM_HEADS = 16
SSM_HEAD_DIM = 64
D_SSM = N_SSM_HEADS * SSM_HEAD_DIM
N_SSM_GROUPS = 4
D_STATE = 128
D_BC = N_SSM_GROUPS * D_STATE
CONV_WIDTH = 4
CONV_DIM = D_SSM + 2 * D_BC
SSD_CHUNK = 128
D_MIX = D_ATT + D_SSM
IN_PROJ = D_ATT + 2 * D_KV + D_SSM + CONV_DIM + N_SSM_HEADS
IN_SPLITS = [D_ATT, D_ATT + D_KV, D_ATT + 2 * D_KV, D_ATT + 2 * D_KV + D_SSM,
             D_ATT + 2 * D_KV + D_SSM + CONV_DIM]
D_FF = -(-8 * D_MODEL // (3 * 256)) * 256
PLE_DIM = 256
EPS = 1e-6
NEG_INF = -1e30

kernel_name = 'hymba_dilated_ssd_decode_step'


def rmsnorm(x, g):
    xf = x.astype(jnp.float32)
    y = xf * lax.rsqrt(jnp.mean(xf * xf, axis=-1, keepdims=True) + EPS)
    return (y * g.astype(jnp.float32)).astype(x.dtype)


def rotary(x, pos):
    half = HEAD_DIM // 2
    inv_freq = ROPE_THETA ** (-jnp.arange(half, dtype=jnp.float32) / half)
    ang = pos.astype(jnp.float32)[:, None] * inv_freq[None, :]
    cos = jnp.cos(ang)[None, :, None, :]
    sin = jnp.sin(ang)[None, :, None, :]
    xf = x.astype(jnp.float32)
    x1, x2 = xf[..., :half], xf[..., half:]
    return jnp.concatenate([x1 * cos - x2 * sin, x2 * cos + x1 * sin], axis=-1).astype(x.dtype)


def masked_softmax_parts(scores, mask):
    scores = jnp.where(mask, scores, NEG_INF)
    m = jnp.max(scores, axis=-1, keepdims=True)
    e = jnp.exp(scores - m)
    s = jnp.sum(e, axis=-1, keepdims=True)
    return e / s, (m + jnp.log(s))[..., 0]


def to_residues(t, dil):
    b, s = t.shape[:2]
    t = t.reshape((b, s // dil, dil) + t.shape[2:])
    t = jnp.moveaxis(t, 2, 1)
    return t.reshape((b * dil, s // dil) + t.shape[3:])


def from_residues(t, b, dil):
    n = t.shape[1]
    t = t.reshape((b, dil, n) + t.shape[2:])
    t = jnp.moveaxis(t, 1, 2)
    return t.reshape((b, n * dil) + t.shape[3:])


def pad_seq(t, front, back):
    cfg = [(0, 0)] * t.ndim
    cfg[1] = (front, back)
    return jnp.pad(t, cfg)


def dilated_branch_prompt(q, k, v, dil, steps):
    b = q.shape[0]
    qr, kr, vr = to_residues(q, dil), to_residues(k, dil), to_residues(v, dil)
    bd, n = qr.shape[:2]
    blk = steps
    nb = -(-n // blk)
    tail = nb * blk - n
    qb = pad_seq(qr, 0, tail).reshape((bd, nb, blk) + qr.shape[2:])
    kb = pad_seq(kr, blk, tail).reshape((bd, nb + 1, blk) + kr.shape[2:])
    vb = pad_seq(vr, blk, tail).reshape((bd, nb + 1, blk) + vr.shape[2:])
    k_band = jnp.concatenate([kb[:, :-1], kb[:, 1:]], axis=2)
    v_band = jnp.concatenate([vb[:, :-1], vb[:, 1:]], axis=2)
    qi = jnp.arange(blk)[:, None]
    ki = jnp.arange(2 * blk)[None, :] - blk
    dist = qi - ki
    in_window = (dist >= 0) & (dist <= steps)
    in_seq = (jnp.arange(nb)[:, None, None] * blk + ki[None]) >= 0
    mask = in_window[None] & in_seq
    scores = jnp.einsum('znqgrd,znkgd->zngrqk', qb, k_band,
                        preferred_element_type=jnp.float32) * ATT_SCALE
    probs, lse = masked_softmax_parts(scores, mask[None, :, None, None])
    o = jnp.einsum('zngrqk,znkgd->znqgrd', probs.astype(v.dtype), v_band,
                   preferred_element_type=jnp.float32)
    o = o.reshape((bd, nb * blk) + o.shape[3:])[:, :n]
    lse = jnp.moveaxis(lse, -1, 2).reshape((bd, nb * blk) + lse.shape[2:4])[:, :n]
    return from_residues(o, b, dil), from_residues(lse, b, dil)


def dilated_branch_sample(q, k_all, v_all, n_past, dil, steps):
    t_new = q.shape[1]
    idx = n_past + jnp.arange(t_new)[:, None] - dil * jnp.arange(steps + 1)[None, :]
    valid = idx >= 0
    idx = jnp.maximum(idx, 0)
    kg = k_all[:, idx]
    vg = v_all[:, idx]
    scores = jnp.einsum('btgrd,btkgd->btgrk', q, kg,
                        preferred_element_type=jnp.float32) * ATT_SCALE
    probs, lse = masked_softmax_parts(scores, valid[None, :, None, None, :])
    o = jnp.einsum('btgrk,btkgd->btgrd', probs.astype(v_all.dtype), vg,
                   preferred_element_type=jnp.float32)
    return o, lse


def merge_branches(outs, lses):
    w = jax.nn.softmax(jnp.stack(lses, axis=0), axis=0)
    return jnp.einsum('nbtgr,nbtgrd->btgrd', w, jnp.stack(outs, axis=0))


def dilated_attention(q, k, v, kv_past):
    b, l = q.shape[:2]
    qg = q.reshape(b, l, N_KV_HEADS, Q_PER_KV, HEAD_DIM)
    if kv_past is None:
        parts = [dilated_branch_prompt(qg, k, v, d, w // d) for w, d in DILATED_BRANCHES]
        keep = min(ATT_WINDOW, l)
        new_k, new_v = k[:, l - keep:], v[:, l - keep:]
    else:
        k_past, v_past = kv_past
        n_past = k_past.shape[1]
        k_all = jnp.concatenate([k_past.astype(k.dtype), k], axis=1)
        v_all = jnp.concatenate([v_past.astype(v.dtype), v], axis=1)
        parts = [dilated_branch_sample(qg, k_all, v_all, n_past, d, w // d) for w, d in DILATED_BRANCHES]
        keep = min(ATT_WINDOW, n_past + l)
        new_k, new_v = k_all[:, n_past + l - keep:], v_all[:, n_past + l - keep:]
    o = merge_branches([pr[0] for pr in parts], [pr[1] for pr in parts])
    return o.reshape(b, l, D_ATT).astype(q.dtype), new_k, new_v


def ssd_scan(x, dt, a, b_mat, c_mat, h0):
    bsz, l = x.shape[:2]
    cs = SSD_CHUNK if l % SSD_CHUNK == 0 else l
    nc = l // cs
    rep = N_SSM_HEADS // N_SSM_GROUPS

    def chunked(t):
        return t.reshape((bsz, nc, cs) + t.shape[2:])

    xc = chunked(x.astype(jnp.float32) * dt[..., None])
    bc = chunked(jnp.repeat(b_mat.astype(jnp.float32), rep, axis=2))
    cc = chunked(jnp.repeat(c_mat.astype(jnp.float32), rep, axis=2))
    acum = jnp.cumsum(chunked(dt * a), axis=2)
    causal = jnp.tril(jnp.ones((cs, cs), dtype=bool))[None, None, :, :, None]
    seg = acum[:, :, :, None, :] - acum[:, :, None, :, :]
    decay = jnp.exp(jnp.where(causal, seg, NEG_INF))
    cb = jnp.einsum('bcthn,bcshn->bctsh', cc, bc) * decay
    y_diag = jnp.einsum('bctsh,bcshp->bcthp', cb, xc)
    to_end = jnp.exp(acum[:, :, -1:, :] - acum)
    chunk_states = jnp.einsum('bcshn,bcsh,bcshp->bchpn', bc, to_end, xc)
    chunk_decay = jnp.exp(acum[:, :, -1, :])

    def carry_step(h, inp):
        st, dec = inp
        return h * dec[:, :, None, None] + st, h

    h_last, h_in = lax.scan(carry_step, h0,
                            (jnp.moveaxis(chunk_states, 1, 0), jnp.moveaxis(chunk_decay, 1, 0)))
    h_in = jnp.moveaxis(h_in, 0, 1)
    y_off = jnp.einsum('bcthn,bchpn,bcth->bcthp', cc, h_in, jnp.exp(acum))
    return (y_diag + y_off).reshape(x.shape), h_last


def mixer_block(hn, pos, kv_past, conv_buf, ssm_h0, w_in, conv_w, conv_b, dt_bias, a_log,
                d_skip, ssm_norm_g, w_out):
    b, l, _ = hn.shape
    q, k, v, z, xbc, dt_raw = jnp.split(hn @ w_in, IN_SPLITS, axis=-1)
    q = rotary(q.reshape(b, l, N_Q_HEADS, HEAD_DIM), pos)
    k = rotary(k.reshape(b, l, N_KV_HEADS, HEAD_DIM), pos)
    v = v.reshape(b, l, N_KV_HEADS, HEAD_DIM)
    att, new_k, new_v = dilated_attention(q, k, v, kv_past)
    xbc_pad = jnp.concatenate([conv_buf.astype(xbc.dtype), xbc], axis=1)
    xbc = jax.nn.silu(lax.conv_general_dilated(
        xbc_pad, conv_w[:, None, :].astype(xbc.dtype), window_strides=(1,), padding='VALID',
        dimension_numbers=('NWC', 'WIO', 'NWC'), feature_group_count=CONV_DIM) + conv_b)
    new_conv = xbc_pad[:, -(CONV_WIDTH - 1):]
    xs = xbc[..., :D_SSM].reshape(b, l, N_SSM_HEADS, SSM_HEAD_DIM)
    b_mat = xbc[..., D_SSM:D_SSM + D_BC].reshape(b, l, N_SSM_GROUPS, D_STATE)
    c_mat = xbc[..., D_SSM + D_BC:].reshape(b, l, N_SSM_GROUPS, D_STATE)
    dt = jax.nn.softplus(dt_raw.astype(jnp.float32) + dt_bias.astype(jnp.float32))
    a = -jnp.exp(a_log.astype(jnp.float32))
    y, h_last = ssd_scan(xs, dt, a, b_mat, c_mat, ssm_h0.astype(jnp.float32))
    y = y + d_skip.astype(jnp.float32)[:, None] * xs.astype(jnp.float32)
    y = y.reshape(b, l, D_SSM) * jax.nn.silu(z.astype(jnp.float32))
    ssm = rmsnorm(y, ssm_norm_g).astype(hn.dtype)
    out = jnp.concatenate([att, ssm], axis=-1) @ w_out
    return out, new_k, new_v, h_last.astype(ssm_h0.dtype), new_conv


def trunk(x, p, pos, cache_k, cache_v, state_ssm, state_conv, params):
    (norm_mix_g, w_in, conv_w, conv_b, dt_bias, a_log, d_skip, ssm_norm_g, w_out,
     norm_ffn_g, w_ffn_gate, w_ffn_up, w_ffn_down, norm_ple_g, w_ple_gate, w_ple_proj,
     final_norm_g) = params
    h = x
    ks, vs, ss, cs = [], [], [], []
    for i in range(DEPTH):
        kv_past = None if cache_k is None else (cache_k[i], cache_v[i])
        mix, k_i, v_i, s_i, c_i = mixer_block(
            rmsnorm(h, norm_mix_g[i]), pos, kv_past, state_conv[i], state_ssm[i], w_in[i],
            conv_w[i], conv_b[i], dt_bias[i], a_log[i], d_skip[i], ssm_norm_g[i], w_out[i])
        h = h + mix
        hf = rmsnorm(h, norm_ffn_g[i])
        h = h + (jax.nn.silu(hf @ w_ffn_gate[i]) * (hf @ w_ffn_up[i])) @ w_ffn_down[i]
        gate = jax.nn.sigmoid(rmsnorm(h, norm_ple_g[i]) @ w_ple_gate[i])
        h = h + gate * (p[i] @ w_ple_proj[i])
        ks.append(k_i)
        vs.append(v_i)
        ss.append(s_i)
        cs.append(c_i)
    return rmsnorm(h, final_norm_g), jnp.stack(ks), jnp.stack(vs), jnp.stack(ss), jnp.stack(cs)


def _nrm(k, shape, scale):
    return jax.random.normal(k, shape, jnp.float32) * scale


def setup_inputs(seed: int = 0) -> dict:
    key = jax.random.key(seed)
    ks = jax.random.split(key, 25)
    l_win = min(ATT_WINDOW, PAST_LEN)
    dt0 = jnp.exp(jax.random.uniform(ks[12], (DEPTH, N_SSM_HEADS), jnp.float32,
                                     minval=math.log(1e-3), maxval=math.log(0.1)))
    return {
        'x_prompt': _nrm(ks[0], (BATCH, SEQ, D_MODEL), 1.0),
        'x_sample': _nrm(ks[1], (DEC_BATCH, DEC_SEQ, D_MODEL), 1.0),
        'cache_k': _nrm(ks[2], (DEPTH, DEC_BATCH, l_win, N_KV_HEADS, HEAD_DIM), 1.0),
        'cache_v': _nrm(ks[3], (DEPTH, DEC_BATCH, l_win, N_KV_HEADS, HEAD_DIM), 1.0),
        'state_ssm': _nrm(ks[4], (DEPTH, DEC_BATCH, N_SSM_HEADS, SSM_HEAD_DIM, D_STATE), 0.1),
        'state_conv': _nrm(ks[5], (DEPTH, DEC_BATCH, CONV_WIDTH - 1, CONV_DIM), 1.0),
        'p_prompt': _nrm(ks[6], (DEPTH, BATCH, SEQ, PLE_DIM), 1.0),
        'p_sample': _nrm(ks[7], (DEPTH, DEC_BATCH, DEC_SEQ, PLE_DIM), 1.0),
        'norm_mix_g': 1.0 + _nrm(ks[8], (DEPTH, D_MODEL), 0.01),
        'w_in': _nrm(ks[9], (DEPTH, D_MODEL, IN_PROJ), D_MODEL ** -0.5),
        'conv_w': _nrm(ks[10], (DEPTH, CONV_WIDTH, CONV_DIM), CONV_WIDTH ** -0.5),
        'conv_b': _nrm(ks[11], (DEPTH, CONV_DIM), 0.01),
        'dt_bias': dt0 + jnp.log(-jnp.expm1(-dt0)),
        'a_log': jnp.log(jax.random.uniform(ks[13], (DEPTH, N_SSM_HEADS), jnp.float32,
                                            minval=1.0, maxval=16.0)),
        'd_skip': 1.0 + _nrm(ks[14], (DEPTH, N_SSM_HEADS), 0.01),
        'ssm_norm_g': 1.0 + _nrm(ks[15], (DEPTH, D_SSM), 0.01),
        'w_out': _nrm(ks[16], (DEPTH, D_MIX, D_MODEL), D_MIX ** -0.5),
        'norm_ffn_g': 1.0 + _nrm(ks[17], (DEPTH, D_MODEL), 0.01),
        'w_ffn_gate': _nrm(ks[18], (DEPTH, D_MODEL, D_FF), D_MODEL ** -0.5),
        'w_ffn_up': _nrm(ks[19], (DEPTH, D_MODEL, D_FF), D_MODEL ** -0.5),
        'w_ffn_down': _nrm(ks[20], (DEPTH, D_FF, D_MODEL), D_FF ** -0.5),
        'norm_ple_g': 1.0 + _nrm(ks[21], (DEPTH, D_MODEL), 0.01),
        'w_ple_gate': _nrm(ks[22], (DEPTH, D_MODEL, D_MODEL), D_MODEL ** -0.5),
        'w_ple_proj': _nrm(ks[23], (DEPTH, PLE_DIM, D_MODEL), PLE_DIM ** -0.5),
        'final_norm_g': 1.0 + _nrm(ks[24], (D_MODEL,), 0.01),
    }


def reference(x_prompt, x_sample, cache_k, cache_v, state_ssm, state_conv, p_prompt, p_sample,
              norm_mix_g, w_in, conv_w, conv_b, dt_bias, a_log, d_skip, ssm_norm_g, w_out,
              norm_ffn_g, w_ffn_gate, w_ffn_up, w_ffn_down, norm_ple_g, w_ple_gate, w_ple_proj,
              final_norm_g):
    params = (norm_mix_g, w_in, conv_w, conv_b, dt_bias, a_log, d_skip, ssm_norm_g, w_out,
              norm_ffn_g, w_ffn_gate, w_ffn_up, w_ffn_down, norm_ple_g, w_ple_gate, w_ple_proj,
              final_norm_g)
    b, s = x_prompt.shape[:2]
    zero_ssm = jnp.zeros((DEPTH, b, N_SSM_HEADS, SSM_HEAD_DIM, D_STATE), state_ssm.dtype)
    zero_conv = jnp.zeros((DEPTH, b, CONV_WIDTH - 1, CONV_DIM), x_prompt.dtype)
    pos_prompt = jnp.arange(s, dtype=jnp.int32)
    pos_sample = PAST_LEN + jnp.arange(x_sample.shape[1], dtype=jnp.int32)
    y_prompt, new_k_prompt, new_v_prompt, new_ssm_prompt, new_conv_prompt = trunk(
        x_prompt, p_prompt, pos_prompt, None, None, zero_ssm, zero_conv, params)
    y_sample, new_k_sample, new_v_sample, new_ssm_sample, new_conv_sample = trunk(
        x_sample, p_sample, pos_sample, cache_k, cache_v, state_ssm, state_conv, params)
    return (y_prompt, y_sample, new_k_prompt, new_v_prompt, new_ssm_prompt, new_conv_prompt,
            new_k_sample, new_v_sample, new_ssm_sample, new_conv_sample)
```

```python
import functools

import numpy as np
import jax
import jax.numpy as jnp
from jax import lax
from jax.experimental import pallas as pl
from jax.experimental.pallas import tpu as pltpu

F32 = jnp.float32
BF16 = jnp.bfloat16

N_Q_HEADS = 8
N_KV_HEADS = 4
HEAD_DIM = 128
D_ATT = N_Q_HEADS * HEAD_DIM
D_KV = N_KV_HEADS * HEAD_DIM
DILATED_BRANCHES = ((128, 1), (512, 4), (2048, 16))
ATT_WINDOW = 2048
ROPE_THETA = 10000.0
ATT_SCALE = HEAD_DIM ** -0.5
N_SSM_HEADS = 16
SSM_HEAD_DIM = 64
D_SSM = N_SSM_HEADS * SSM_HEAD_DIM
N_SSM_GROUPS = 4
HEADS_PER_GROUP = N_SSM_HEADS // N_SSM_GROUPS
D_STATE = 128
D_BC = N_SSM_GROUPS * D_STATE
CONV_WIDTH = 4
CONV_DIM = D_SSM + 2 * D_BC
SSD_CHUNK = 128
PAST_LEN = 2048
EPS = 1e-6
NEG_INF = -1e30

LANES = 128
SUBLANES = 8
VMEM_LIMIT_BYTES = 56 * 1024 * 1024

COL_Q = 0
COL_K = D_ATT
COL_V = D_ATT + D_KV
COL_Z = D_ATT + 2 * D_KV
COL_X = COL_Z + D_SSM
COL_BC = COL_X + D_SSM
PROJ_COLS = COL_BC + 2 * D_BC

NT_DIMS = (((1,), (1,)), ((), ()))
TN_DIMS = (((0,), (0,)), ((), ()))


def _dot(a, b):
    return jnp.dot(a, b, preferred_element_type=F32)


def _dot_nt(a, b):
    return lax.dot_general(a, b, NT_DIMS, preferred_element_type=F32)


def _rms(x, g):
    return x * lax.rsqrt(jnp.mean(x * x, axis=-1, keepdims=True) + EPS) * g


def _silu(x):
    return x * jax.nn.sigmoid(x)


def _softplus(x):
    return jnp.maximum(x, 0.0) + jnp.log1p(jnp.exp(-jnp.abs(x)))


def _expand(x, r):
    hi = x.astype(BF16)
    r1 = x - hi.astype(F32)
    mid = r1.astype(BF16)
    lo = (r1 - mid.astype(F32)).astype(BF16)
    return _dot(hi, r) + _dot(mid, r) + _dot(lo, r)


def _imod(x, n):
    assert n & (n - 1) == 0, "power-of-two divisor expected"
    return x & (n - 1)


def _idiv(x, n):
    assert n & (n - 1) == 0, "power-of-two divisor expected"
    return x >> (n.bit_length() - 1)


def _params(sem):
    return pltpu.CompilerParams(dimension_semantics=sem, vmem_limit_bytes=VMEM_LIMIT_BYTES)


def _in_proj_kernel(x_ref, g_ref, w_ref, wdt_ref, cos_ref, sin_ref, o_ref, dt_ref, hn_ref, *,
                    rot_tiles):
    n = pl.program_id(1)

    @pl.when(n == 0)
    def _():
        hn = _rms(x_ref[...], g_ref[...]).astype(BF16)
        hn_ref[...] = hn
        dt_ref[...] = _dot(hn, wdt_ref[...])

    acc = _dot(hn_ref[...], w_ref[...].astype(BF16))

    @pl.when(n < rot_tiles)
    def _():
        cos = cos_ref[...]
        sin = sin_ref[...]
        for hh in range(acc.shape[1] // HEAD_DIM):
            xh = acc[:, hh * HEAD_DIM:(hh + 1) * HEAD_DIM]
            o_ref[:, hh * HEAD_DIM:(hh + 1) * HEAD_DIM] = (
                xh * cos + pltpu.roll(xh, HEAD_DIM // 2, axis=1) * sin)

    @pl.when(n >= rot_tiles)
    def _():
        o_ref[...] = acc


def _in_proj(h, g, w_in, layer, wdt, cos2, sin2, *, tm, tn):
    t, d = h.shape
    kern = functools.partial(_in_proj_kernel, rot_tiles=(D_ATT + D_KV) // tn)
    return pl.pallas_call(
        kern,
        out_shape=(jax.ShapeDtypeStruct((t, PROJ_COLS), F32),
                   jax.ShapeDtypeStruct((t, LANES), F32)),
        grid=(t // tm, PROJ_COLS // tn),
        in_specs=[
            pl.BlockSpec((tm, d), lambda m, n: (m, 0)),
            pl.BlockSpec((1, d), lambda m, n: (0, 0)),
            pl.BlockSpec((None, d, tn), lambda m, n: (layer, 0, n)),
            pl.BlockSpec((d, LANES), lambda m, n: (0, 0)),
            pl.BlockSpec((tm, HEAD_DIM), lambda m, n: (m, 0)),
            pl.BlockSpec((tm, HEAD_DIM), lambda m, n: (m, 0)),
        ],
        out_specs=(pl.BlockSpec((tm, tn), lambda m, n: (m, n)),
                   pl.BlockSpec((tm, LANES), lambda m, n: (m, 0))),
        scratch_shapes=[pltpu.VMEM((tm, d), BF16)],
        compiler_params=_params(("parallel", "arbitrary")),
        name="in_proj",
    )(h, g, w_in, wdt, cos2, sin2)


def _prompt_attn_kernel(q_ref, k_ref, v_ref, o_ref, ob0, ob1, ob2, lb0, lb1, lb2, *, seq):
    blk = 128
    obs = (ob0, ob1, ob2)
    lbs = (lb0, lb1, lb2)
    ii = lax.broadcasted_iota(jnp.int32, (blk, 2 * blk), 0)
    jj = lax.broadcasted_iota(jnp.int32, (blk, 2 * blk), 1)
    dist = ii + blk - jj

    for bi, (win, dil) in enumerate(DILATED_BRANCHES):
        assert win // dil == blk and seq % (dil * blk) == 0
        nblk = seq // (dil * blk)
        ob, lb = obs[bi], lbs[bi]

        def rows(start, dil=dil):
            return pl.ds(start, blk, stride=dil) if dil > 1 else pl.ds(start, blk)

        def block(r, n, ob=ob, lb=lb, rows=rows, dil=dil):
            base = r + n * (blk * dil)
            prev = r + jnp.maximum(n - 1, 0) * (blk * dil)
            q = q_ref[rows(base), :].astype(BF16)
            kb = jnp.concatenate([k_ref[rows(prev), :], k_ref[rows(base), :]], axis=0).astype(BF16)
            vb = jnp.concatenate([v_ref[rows(prev), :], v_ref[rows(base), :]], axis=0).astype(BF16)
            s = _dot_nt(q, kb) * ATT_SCALE
            hi = jnp.where(n == 0, ii, blk)
            mask = (dist >= 0) & (dist <= hi)
            s = jnp.where(mask, s, NEG_INF)
            m = jnp.max(s, axis=1, keepdims=True)
            e = jnp.exp(s - m)
            ssum = jnp.sum(e, axis=1, keepdims=True)
            p = (e / ssum).astype(BF16)
            ob[rows(base), :] = _dot(p, vb)
            lb[rows(base), :] = jnp.broadcast_to(m + jnp.log(ssum), (blk, HEAD_DIM))

        def outer(r, carry, block=block, nblk=nblk):
            def inner(n, c):
                block(r, n)
                return c
            return lax.fori_loop(0, nblk, inner, carry)

        lax.fori_loop(0, dil, outer, 0)

    rows_per = 512

    def merge(c, carry):
        sl = pl.ds(pl.multiple_of(c * rows_per, rows_per), rows_per)
        l0, l1, l2 = lb0[sl, :], lb1[sl, :], lb2[sl, :]
        mx = jnp.maximum(jnp.maximum(l0, l1), l2)
        w0, w1, w2 = jnp.exp(l0 - mx), jnp.exp(l1 - mx), jnp.exp(l2 - mx)
        o = (w0 * ob0[sl, :] + w1 * ob1[sl, :] + w2 * ob2[sl, :]) / (w0 + w1 + w2)
        o_ref[sl, :] = o.astype(o_ref.dtype)
        return carry

    lax.fori_loop(0, seq // rows_per, merge, 0)


def _prompt_attn(proj, batch, seq):
    kern = functools.partial(_prompt_attn_kernel, seq=seq)
    kq = COL_K // HEAD_DIM
    vq = COL_V // HEAD_DIM
    rep = N_Q_HEADS // N_KV_HEADS
    return pl.pallas_call(
        kern,
        out_shape=jax.ShapeDtypeStruct((batch * seq, D_ATT), BF16),
        grid=(batch, N_Q_HEADS),
        in_specs=[
            pl.BlockSpec((seq, HEAD_DIM), lambda b, h: (b, h)),
            pl.BlockSpec((seq, HEAD_DIM), lambda b, h: (b, kq + h // rep)),
            pl.BlockSpec((seq, HEAD_DIM), lambda b, h: (b, vq + h // rep)),
        ],
        out_specs=pl.BlockSpec((seq, HEAD_DIM), lambda b, h: (b, h)),
        scratch_shapes=[pltpu.VMEM((seq, HEAD_DIM), F32) for _ in range(6)],
        compiler_params=_params(("parallel", "arbitrary")),
        name="prompt_attn",
    )(proj, proj, proj)


def _sample_attn_kernel(*refs, n_past, t_new, aliased):
    if aliased:
        qkv_ref, ck_ref, cv_ref, _, _, att_ref, nk_ref, nv_ref = refs
    else:
        qkv_ref, ck_ref, cv_ref, att_ref, nk_ref, nv_ref = refs
    qkv = qkv_ref[...]
    k_new = qkv[:, COL_K:COL_K + D_KV]
    v_new = qkv[:, COL_V:COL_V + D_KV]

    assert n_past % SUBLANES == 0 and 0 < t_new < SUBLANES
    body = n_past - SUBLANES
    step = 256
    for src, new, dst in ((ck_ref, k_new, nk_ref), (cv_ref, v_new, nv_ref)):
        for r0 in range(0, body, step):
            nrows = min(step, body - r0)
            dst[r0:r0 + nrows, :] = src[r0 + t_new:r0 + t_new + nrows, :]
        dst[body:, :] = jnp.concatenate([src[body + t_new:, :], new], axis=0)

    rep = N_Q_HEADS // N_KV_HEADS
    nq = rep * t_new
    cidx = lax.broadcasted_iota(jnp.int32, (nq, n_past), 1)
    tok = _imod(lax.broadcasted_iota(jnp.int32, (nq, n_past), 0), t_new)
    dist_c = n_past + tok - cidx
    tok1 = _imod(lax.broadcasted_iota(jnp.int32, (nq, 1), 0), t_new)

    for g in range(N_KV_HEADS):
        qg = jnp.concatenate(
            [qkv[:, (g * rep + r) * HEAD_DIM:(g * rep + r + 1) * HEAD_DIM] for r in range(rep)],
            axis=0).astype(BF16)
        kc = ck_ref[:, g * HEAD_DIM:(g + 1) * HEAD_DIM].astype(BF16)
        vc = cv_ref[:, g * HEAD_DIM:(g + 1) * HEAD_DIM].astype(BF16)
        kn = k_new[:, g * HEAD_DIM:(g + 1) * HEAD_DIM].astype(BF16).astype(F32)
        vn = v_new[:, g * HEAD_DIM:(g + 1) * HEAD_DIM].astype(BF16).astype(F32)
        s_c = _dot_nt(qg, kc) * ATT_SCALE
        qf = qg.astype(F32)
        s_n = [jnp.sum(qf * kn[j:j + 1, :], axis=1, keepdims=True) * ATT_SCALE
               for j in range(t_new)]

        probs, new_terms, lses = [], [], []
        for win, dil in DILATED_BRANCHES:
            assert win <= n_past
            mask_c = ((dist_c & (dil - 1)) == 0) & (dist_c <= win)
            sc = jnp.where(mask_c, s_c, NEG_INF)
            m = jnp.max(sc, axis=1, keepdims=True)
            sn = []
            for j in range(t_new):
                dn = tok1 - j
                mask_n = (dn >= 0) & ((dn & (dil - 1)) == 0)
                snj = jnp.where(mask_n, s_n[j], NEG_INF)
                sn.append(snj)
                m = jnp.maximum(m, snj)
            ec = jnp.exp(sc - m)
            en = [jnp.exp(x - m) for x in sn]
            ssum = jnp.sum(ec, axis=1, keepdims=True)
            for x in en:
                ssum = ssum + x
            probs.append((ec / ssum).astype(BF16))
            o_new = jnp.zeros((nq, HEAD_DIM), F32)
            for j in range(t_new):
                o_new = o_new + (en[j] / ssum).astype(BF16).astype(F32) * vn[j:j + 1, :]
            new_terms.append(o_new)
            lses.append(m + jnp.log(ssum))

        o_all = _dot(jnp.concatenate(probs, axis=0), vc)
        mx = functools.reduce(jnp.maximum, lses)
        ws = [jnp.exp(l - mx) for l in lses]
        den = functools.reduce(lambda a, b: a + b, ws)
        o = jnp.zeros((nq, HEAD_DIM), F32)
        for i, w in enumerate(ws):
            o = o + w * (o_all[i * nq:(i + 1) * nq, :] + new_terms[i])
        o = o / den
        for r in range(rep):
            att_ref[:, (g * rep + r) * HEAD_DIM:(g * rep + r + 1) * HEAD_DIM] = (
                o[r * t_new:(r + 1) * t_new, :])


def _sample_attn(qkv_s, cache_k, cache_v, layer, prev_k, prev_v):
    depth, bsz, n_past, _ = cache_k.shape
    t_new = qkv_s.shape[1]
    aliased = prev_k is not None
    kern = functools.partial(_sample_attn_kernel, n_past=n_past, t_new=t_new, aliased=aliased)
    in_specs = [
        pl.BlockSpec((None, t_new, qkv_s.shape[2]), lambda b: (b, 0, 0)),
        pl.BlockSpec((None, None, n_past, D_KV), lambda b: (layer, b, 0, 0)),
        pl.BlockSpec((None, None, n_past, D_KV), lambda b: (layer, b, 0, 0)),
    ]
    args = [qkv_s, cache_k, cache_v]
    aliases = {}
    if aliased:
        in_specs += [pl.BlockSpec(memory_space=pl.ANY), pl.BlockSpec(memory_space=pl.ANY)]
        args += [prev_k, prev_v]
        aliases = {3: 1, 4: 2}
    return pl.pallas_call(
        kern,
        out_shape=(jax.ShapeDtypeStruct((bsz, t_new, D_ATT), F32),
                   jax.ShapeDtypeStruct(cache_k.shape, cache_k.dtype),
                   jax.ShapeDtypeStruct(cache_v.shape, cache_v.dtype)),
        grid=(bsz,),
        in_specs=in_specs,
        out_specs=(pl.BlockSpec((None, t_new, D_ATT), lambda b: (b, 0, 0)),
                   pl.BlockSpec((None, None, n_past, D_KV), lambda b: (layer, b, 0, 0)),
                   pl.BlockSpec((None, None, n_past, D_KV), lambda b: (layer, b, 0, 0))),
        input_output_aliases=aliases,
        compiler_params=_params(("parallel",)),
        name="sample_attn",
    )(*args)


def _expand_mats():
    r64 = np.zeros((LANES, D_SSM), np.float32)
    r128 = np.zeros((LANES, N_SSM_HEADS * LANES), np.float32)
    for h in range(N_SSM_HEADS):
        r64[h, h * SSM_HEAD_DIM:(h + 1) * SSM_HEAD_DIM] = 1.0
        r128[h, h * LANES:(h + 1) * LANES] = 1.0
    return jnp.asarray(r64, BF16), jnp.asarray(r128, BF16)


def _group_sum_mat():
    g = np.zeros((D_BC, D_SSM), np.float32)
    for grp in range(N_SSM_GROUPS):
        g[grp * D_STATE:(grp + 1) * D_STATE,
          grp * HEADS_PER_GROUP * SSM_HEAD_DIM:(grp + 1) * HEADS_PER_GROUP * SSM_HEAD_DIM] = 1.0
    return jnp.asarray(g, BF16)


def _conv_silu(x, tail, cw, cb):
    n = x.shape[0]
    xp = jnp.concatenate([tail, x], axis=0)
    out = cb + cw[3:4, :] * x
    for w in range(CONV_WIDTH - 1):
        off = SUBLANES - (CONV_WIDTH - 1) + w
        out = out + cw[w:w + 1, :] * xp[off:off + n, :]
    return _silu(out)


def _cumsum_rows(x, seg=None):
    n = x.shape[0]
    rows = lax.broadcasted_iota(jnp.int32, x.shape, 0)
    pos = rows if seg is None else _imod(rows, seg)
    limit = n if seg is None else seg
    sh = 1
    while sh < limit:
        x = x + jnp.where(pos >= sh, pltpu.roll(x, sh, axis=0), 0.0)
        sh *= 2
    return x


def _lane_col_block(row):
    return jnp.broadcast_to(row, (LANES, LANES)).T


def _ssd_prompt_kernel(z_ref, xs_ref, bc_ref, dt_ref, cw_ref, cb_ref, dtb_ref, alog_ref, dsk_ref,
                       ng_ref, r64_ref, r128_ref, y_ref, st_ref, cv_ref, h_scr, tail_scr):
    c = pl.program_id(1)
    last = pl.num_programs(1) - 1
    cs = SSD_CHUNK

    @pl.when(c == 0)
    def _():
        h_scr[...] = jnp.zeros_like(h_scr)
        tail_scr[...] = jnp.zeros_like(tail_scr)

    xraw = xs_ref[...]
    bcraw = bc_ref[...]
    tail = tail_scr[...]
    cw = cw_ref[...]
    cb = cb_ref[...]
    xs = _conv_silu(xraw, tail[:, :D_SSM], cw[:, :D_SSM], cb[:, :D_SSM])
    bcm = _conv_silu(bcraw, tail[:, D_SSM:], cw[:, D_SSM:], cb[:, D_SSM:])
    tail_scr[:, :D_SSM] = xraw[cs - SUBLANES:, :]
    tail_scr[:, D_SSM:] = bcraw[cs - SUBLANES:, :]

    @pl.when(c == last)
    def _():
        cv_ref[:, :D_SSM] = xraw[cs - (CONV_WIDTH - 1):, :]
        cv_ref[:, D_SSM:] = bcraw[cs - (CONV_WIDTH - 1):, :]

    dt = _softplus(dt_ref[...] + dtb_ref[...])
    a = -jnp.exp(alog_ref[...])
    acum = _cumsum_rows(dt * a)
    acum_t = acum.T
    r64 = r64_ref[...]
    dt_e = _expand(dt, r64)
    ac_e = _expand(acum, r64)
    col_b = _expand(acum, r128_ref[...])
    last_e = ac_e[cs - 1:cs, :]
    ea_e = jnp.exp(ac_e)
    xdt = xs * dt_e
    xte = (xdt * jnp.exp(last_e - ac_e)).astype(BF16)

    ti = lax.broadcasted_iota(jnp.int32, (cs, cs), 0)
    si = lax.broadcasted_iota(jnp.int32, (cs, cs), 1)
    causal = ti >= si
    lane = lax.broadcasted_iota(jnp.int32, (cs, LANES), 1)
    lo_half = lane < SSM_HEAD_DIM

    gw = HEADS_PER_GROUP * SSM_HEAD_DIM
    y_diag, y_off, states = [], [], []
    for g in range(N_SSM_GROUPS):
        bg = bcm[:, g * D_STATE:(g + 1) * D_STATE].astype(BF16)
        cg = bcm[:, D_BC + g * D_STATE:D_BC + (g + 1) * D_STATE].astype(BF16)
        cbt = _dot_nt(cg, bg)
        h_in = h_scr[g * gw:(g + 1) * gw, :]
        y_off.append(_dot_nt(cg, h_in.astype(BF16)))
        states.append(lax.dot_general(xte[:, g * gw:(g + 1) * gw], bg, TN_DIMS,
                                      preferred_element_type=F32))
        for k in range(HEADS_PER_GROUP // 2):
            pair = g * (HEADS_PER_GROUP // 2) + k
            xp = xdt[:, pair * LANES:(pair + 1) * LANES]
            yd = jnp.zeros((cs, LANES), F32)
            for e in range(2):
                h = 2 * pair + e
                seg = col_b[:, h * LANES:(h + 1) * LANES] - jnp.broadcast_to(acum_t[h:h + 1, :], (cs, cs))
                dec = jnp.exp(jnp.where(causal, seg, NEG_INF))
                cbh = (cbt * dec).astype(BF16)
                xh = jnp.where(lo_half if e == 0 else jnp.logical_not(lo_half), xp, 0.0).astype(BF16)
                yd = yd + _dot(cbh, xh)
            y_diag.append(yd)

    y = (jnp.concatenate(y_diag, axis=1) + jnp.concatenate(y_off, axis=1) * ea_e
         + dsk_ref[...] * xs)
    dec_rows = jnp.exp(jnp.concatenate(
        [_lane_col_block(last_e[:, k * LANES:(k + 1) * LANES]) for k in range(D_SSM // LANES)],
        axis=0))
    h_new = h_scr[...] * dec_rows + jnp.concatenate(states, axis=0)
    h_scr[...] = h_new

    @pl.when(c == last)
    def _():
        st_ref[...] = h_new

    y_ref[...] = _rms(y * _silu(z_ref[...]), ng_ref[...]).astype(y_ref.dtype)


def _ssd_prompt(proj, dt_all, layer, conv_w, conv_b3, dtb, alog, dsk_e, ng3, r64, r128, batch, seq):
    cs = SSD_CHUNK
    nc = seq // cs
    zc, xc, bcc = COL_Z // D_SSM, COL_X // D_SSM, COL_BC // D_SSM
    const2 = lambda b, c: (0, 0)
    return pl.pallas_call(
        _ssd_prompt_kernel,
        out_shape=(jax.ShapeDtypeStruct((batch * seq, D_SSM), BF16),
                   jax.ShapeDtypeStruct((batch, D_SSM, D_STATE), F32),
                   jax.ShapeDtypeStruct((batch, CONV_WIDTH - 1, CONV_DIM), F32)),
        grid=(batch, nc),
        in_specs=[
            pl.BlockSpec((cs, D_SSM), lambda b, c: (b * nc + c, zc)),
            pl.BlockSpec((cs, D_SSM), lambda b, c: (b * nc + c, xc)),
            pl.BlockSpec((cs, D_SSM), lambda b, c: (b * nc + c, bcc)),
            pl.BlockSpec((cs, LANES), lambda b, c: (b * nc + c, 0)),
            pl.BlockSpec((None, CONV_WIDTH, CONV_DIM), lambda b, c: (layer, 0, 0)),
            pl.BlockSpec((None, 1, CONV_DIM), lambda b, c: (layer, 0, 0)),
            pl.BlockSpec((None, 1, LANES), lambda b, c: (layer, 0, 0)),
            pl.BlockSpec((None, 1, LANES), lambda b, c: (layer, 0, 0)),
            pl.BlockSpec((None, 1, D_SSM), lambda b, c: (layer, 0, 0)),
            pl.BlockSpec((None, 1, D_SSM), lambda b, c: (layer, 0, 0)),
            pl.BlockSpec(r64.shape, const2),
            pl.BlockSpec(r128.shape, const2),
        ],
        out_specs=(pl.BlockSpec((cs, D_SSM), lambda b, c: (b * nc + c, 0)),
                   pl.BlockSpec((None, D_SSM, D_STATE), lambda b, c: (b, 0, 0)),
                   pl.BlockSpec((None, CONV_WIDTH - 1, CONV_DIM), lambda b, c: (b, 0, 0))),
        scratch_shapes=[pltpu.VMEM((D_SSM, D_STATE), F32), pltpu.VMEM((SUBLANES, CONV_DIM), F32)],
        compiler_params=_params(("parallel", "arbitrary")),
        name="ssd_prompt",
    )(proj, proj, proj, dt_all, conv_w, conv_b3, dtb, alog, dsk_e, ng3, r64, r128)


def _ssd_sample_kernel(*refs, bt, t_new, aliased):
    if aliased:
        (z_ref, xs_ref, bc_ref, dt_ref, cst_ref, h0_ref, cw_ref, cb_ref, dtb_ref, alog_ref, dsk_ref,
         ng_ref, r64_ref, gs_ref, _, _, y_ref, st_ref, cv_ref, xbc_scr) = refs
    else:
        (z_ref, xs_ref, bc_ref, dt_ref, cst_ref, h0_ref, cw_ref, cb_ref, dtb_ref, alog_ref, dsk_ref,
         ng_ref, r64_ref, gs_ref, y_ref, st_ref, cv_ref, xbc_scr) = refs
    rows = bt * t_new
    kw = CONV_WIDTH - 1
    cw = cw_ref[...]
    cb = cb_ref[...]

    for b in range(bt):
        xb = jnp.concatenate([xs_ref[b * t_new:(b + 1) * t_new, :],
                              bc_ref[b * t_new:(b + 1) * t_new, :]], axis=1)
        xp = jnp.concatenate([cst_ref[b], xb], axis=0)
        out = cb
        for w in range(CONV_WIDTH):
            out = out + cw[w:w + 1, :] * xp[w:w + t_new, :]
        xbc_scr[b * t_new:(b + 1) * t_new, :] = _silu(out)
        cv_ref[b] = xp[t_new:t_new + kw, :]

    xbc = xbc_scr[...]
    xs = xbc[:, :D_SSM]
    bm = xbc[:, D_SSM:D_SSM + D_BC]
    cm = xbc[:, D_SSM + D_BC:]

    tpos = _imod(lax.broadcasted_iota(jnp.int32, (rows, 1), 0), t_new)
    dt = _softplus(dt_ref[...] + dtb_ref[...])
    a = -jnp.exp(alog_ref[...])
    acum = _cumsum_rows(dt * a, seg=t_new)
    r64 = r64_ref[...]
    dt_e = _expand(dt, r64)
    ac_e = _expand(acum, r64)
    v = jnp.where(tpos == t_new - 1, ac_e, 0.0)
    last_e = v
    for d in range(1, t_new):
        last_e = last_e + pltpu.roll(v, rows - d, axis=0)
    ea_e = jnp.exp(ac_e)
    xdt = xs * dt_e
    xte = xdt * jnp.exp(last_e - ac_e)

    cmb = cm.astype(BF16).astype(F32)
    bmb = bm.astype(BF16).astype(F32)
    gs = gs_ref[...]
    y = dsk_ref[...] * xs
    for d in range(t_new):
        if d == 0:
            b_s, x_s, a_s = bmb, xdt, ac_e
        else:
            b_s = pltpu.roll(bmb, d, axis=0)
            x_s = pltpu.roll(xdt, d, axis=0)
            a_s = pltpu.roll(ac_e, d, axis=0)
        cb_e = _expand(cmb * b_s, gs)
        dec = jnp.exp(jnp.where(tpos >= d, ac_e - a_s, NEG_INF))
        y = y + cb_e * dec * x_s

    gw = HEADS_PER_GROUP * SSM_HEAD_DIM
    pad = jnp.zeros((LANES - rows, LANES), F32) if rows < LANES else None
    rowb = _idiv(lax.broadcasted_iota(jnp.int32, (rows, 1), 0), t_new)
    colb = _idiv(lax.broadcasted_iota(jnp.int32, (1, LANES), 1), t_new)
    xte_t = []
    for k in range(D_SSM // LANES):
        blk = xte[:, k * LANES:(k + 1) * LANES]
        if pad is not None:
            blk = jnp.concatenate([blk, pad], axis=0)
        xte_t.append(blk.T)
    e_last = jnp.exp(last_e)
    y_off = [jnp.zeros((rows, gw), F32) for _ in range(N_SSM_GROUPS)]
    for b in range(bt):
        h0 = h0_ref[b]
        new_rows = []
        for g in range(N_SSM_GROUPS):
            cg = cm[:, g * D_STATE:(g + 1) * D_STATE]
            bg = bm[:, g * D_STATE:(g + 1) * D_STATE]
            if pad is not None:
                bg = jnp.concatenate([bg, pad], axis=0)
            bg = bg.astype(BF16)
            h0g = h0[g * gw:(g + 1) * gw, :]
            cgb = jnp.where(rowb == b, cg, 0.0).astype(BF16)
            y_off[g] = y_off[g] + _dot_nt(cgb, h0g.astype(BF16))
            for k in range(gw // LANES):
                blk = g * (gw // LANES) + k
                lhs = jnp.where(colb == b, xte_t[blk], 0.0).astype(BF16)
                st = _dot(lhs, bg)
                r = b * t_new + t_new - 1
                dec = _lane_col_block(e_last[r:r + 1, blk * LANES:(blk + 1) * LANES])
                new_rows.append(h0[blk * LANES:(blk + 1) * LANES, :] * dec + st)
        st_ref[b] = jnp.concatenate(new_rows, axis=0)

    y = y + jnp.concatenate(y_off, axis=1) * ea_e
    y_ref[...] = _rms(y * _silu(z_ref[...]), ng_ref[...]).astype(y_ref.dtype)


def _ssd_sample(proj_s, dt_s, state_conv, state_ssm, layer, conv_w, conv_b3, dtb, alog, dsk_e, ng3,
                r64, gsum, prev_st, prev_cv, *, bt, t_new):
    depth, bsz = state_ssm.shape[:2]
    rows = bt * t_new
    zc, xc, bcc = COL_Z // D_SSM, COL_X // D_SSM, COL_BC // D_SSM
    aliased = prev_st is not None
    kern = functools.partial(_ssd_sample_kernel, bt=bt, t_new=t_new, aliased=aliased)
    const2 = lambda i: (0, 0)
    in_specs = [
        pl.BlockSpec((rows, D_SSM), lambda i: (i, zc)),
        pl.BlockSpec((rows, D_SSM), lambda i: (i, xc)),
        pl.BlockSpec((rows, D_SSM), lambda i: (i, bcc)),
        pl.BlockSpec((rows, LANES), lambda i: (i, 0)),
        pl.BlockSpec((None, bt, CONV_WIDTH - 1, CONV_DIM), lambda i: (layer, i, 0, 0)),
        pl.BlockSpec((None, bt, D_SSM, D_STATE), lambda i: (layer, i, 0, 0)),
        pl.BlockSpec((None, CONV_WIDTH, CONV_DIM), lambda i: (layer, 0, 0)),
        pl.BlockSpec((None, 1, CONV_DIM), lambda i: (layer, 0, 0)),
        pl.BlockSpec((None, 1, LANES), lambda i: (layer, 0, 0)),
        pl.BlockSpec((None, 1, LANES), lambda i: (layer, 0, 0)),
        pl.BlockSpec((None, 1, D_SSM), lambda i: (layer, 0, 0)),
        pl.BlockSpec((None, 1, D_SSM), lambda i: (layer, 0, 0)),
        pl.BlockSpec(r64.shape, const2),
        pl.BlockSpec(gsum.shape, const2),
    ]
    args = [proj_s, proj_s, proj_s, dt_s, state_conv, state_ssm, conv_w, conv_b3, dtb, alog, dsk_e,
            ng3, r64, gsum]
    aliases = {}
    if aliased:
        in_specs += [pl.BlockSpec(memory_space=pl.ANY), pl.BlockSpec(memory_space=pl.ANY)]
        args += [prev_st, prev_cv]
        aliases = {14: 1, 15: 2}
    return pl.pallas_call(
        kern,
        out_shape=(jax.ShapeDtypeStruct((bsz * t_new, D_SSM), BF16),
                   jax.ShapeDtypeStruct(state_ssm.shape, state_ssm.dtype),
                   jax.ShapeDtypeStruct(state_conv.shape, state_conv.dtype)),
        grid=(bsz // bt,),
        in_specs=in_specs,
        out_specs=(pl.BlockSpec((rows, D_SSM), lambda i: (i, 0)),
                   pl.BlockSpec((None, bt, D_SSM, D_STATE), lambda i: (layer, i, 0, 0)),
                   pl.BlockSpec((None, bt, CONV_WIDTH - 1, CONV_DIM), lambda i: (layer, i, 0, 0))),
        scratch_shapes=[pltpu.VMEM((rows, CONV_DIM), F32)],
        input_output_aliases=aliases,
        compiler_params=_params(("parallel",)),
        name="ssd_sample",
    )(*args)


def _out_proj_kernel(att_ref, ssm_ref, h_ref, w_ref, o_ref):
    acc = _dot(att_ref[...], w_ref[:D_ATT, :]) + _dot(ssm_ref[...], w_ref[D_ATT:, :])
    o_ref[...] = h_ref[...] + acc


def _out_proj(att, ssm, h, w, *, tm):
    t, d = h.shape
    return pl.pallas_call(
        _out_proj_kernel,
        out_shape=jax.ShapeDtypeStruct((t, d), F32),
        grid=(t // tm,),
        in_specs=[
            pl.BlockSpec((tm, D_ATT), lambda m: (m, 0)),
            pl.BlockSpec((tm, D_SSM), lambda m: (m, 0)),
            pl.BlockSpec((tm, d), lambda m: (m, 0)),
            pl.BlockSpec(w.shape, lambda m: (0, 0)),
        ],
        out_specs=pl.BlockSpec((tm, d), lambda m: (m, 0)),
        compiler_params=_params(("parallel",)),
        name="out_proj",
    )(att, ssm, h, w)


def _ffn_kernel(h_ref, g_ref, wg_ref, wu_ref, wd_ref, o_ref, hf_ref):
    f = pl.program_id(1)

    @pl.when(f == 0)
    def _():
        h = h_ref[...]
        hf_ref[...] = _rms(h, g_ref[...]).astype(BF16)
        o_ref[...] = h

    hf = hf_ref[...]
    act = _silu(_dot(hf, wg_ref[...])) * _dot(hf, wu_ref[...])
    o_ref[...] += _dot(act.astype(BF16), wd_ref[...])


def _ffn(h, g, wg, wu, wd, *, tm, tf):
    t, d = h.shape
    dff = wg.shape[1]
    return pl.pallas_call(
        _ffn_kernel,
        out_shape=jax.ShapeDtypeStruct((t, d), F32),
        grid=(t // tm, dff // tf),
        in_specs=[
            pl.BlockSpec((tm, d), lambda m, f: (m, 0)),
            pl.BlockSpec((1, d), lambda m, f: (0, 0)),
            pl.BlockSpec((d, tf), lambda m, f: (0, f)),
            pl.BlockSpec((d, tf), lambda m, f: (0, f)),
            pl.BlockSpec((tf, d), lambda m, f: (f, 0)),
        ],
        out_specs=pl.BlockSpec((tm, d), lambda m, f: (m, 0)),
        scratch_shapes=[pltpu.VMEM((tm, d), BF16)],
        compiler_params=_params(("parallel", "arbitrary")),
        name="ffn",
    )(h, g, wg, wu, wd)


def _ple_kernel(h_ref, p_ref, g_ref, wg_ref, wp_ref, gf_ref, o_ref, *, final, tn):
    hn = _rms(h_ref[...], g_ref[...]).astype(BF16)
    pb = p_ref[...].astype(BF16)
    for c in range(h_ref.shape[1] // tn):
        cols = slice(c * tn, (c + 1) * tn)
        gate = jax.nn.sigmoid(_dot(hn, wg_ref[:, cols]))
        o_ref[:, cols] = h_ref[:, cols] + gate * _dot(pb, wp_ref[:, cols])
    if final:
        o_ref[...] = _rms(o_ref[...], gf_ref[...])


def _ple(h, p, g, wg, wp, gf, *, final, tm):
    t, d = h.shape
    kern = functools.partial(_ple_kernel, final=final, tn=512)
    return pl.pallas_call(
        kern,
        out_shape=jax.ShapeDtypeStruct((t, d), F32),
        grid=(t // tm,),
        in_specs=[
            pl.BlockSpec((tm, d), lambda m: (m, 0)),
            pl.BlockSpec((tm, p.shape[1]), lambda m: (m, 0)),
            pl.BlockSpec((1, d), lambda m: (0, 0)),
            pl.BlockSpec(wg.shape, lambda m: (0, 0)),
            pl.BlockSpec(wp.shape, lambda m: (0, 0)),
            pl.BlockSpec((1, d), lambda m: (0, 0)),
        ],
        out_specs=pl.BlockSpec((tm, d), lambda m: (m, 0)),
        compiler_params=_params(("parallel",)),
        name="ple",
    )(h, p, g, wg, wp, gf)


def _rope_tables(batch, seq, dec_batch, dec_seq):
    half = HEAD_DIM // 2
    inv_freq = ROPE_THETA ** (-jnp.arange(half, dtype=F32) / half)
    pos = jnp.concatenate([jnp.tile(jnp.arange(seq, dtype=jnp.int32), batch),
                           jnp.tile(PAST_LEN + jnp.arange(dec_seq, dtype=jnp.int32), dec_batch)])
    ang = pos.astype(F32)[:, None] * inv_freq[None, :]
    cos, sin = jnp.cos(ang), jnp.sin(ang)
    return jnp.concatenate([cos, cos], axis=1), jnp.concatenate([-sin, sin], axis=1)


def _pad_lanes(x):
    return jnp.pad(x, [(0, 0)] * (x.ndim - 1) + [(0, LANES - x.shape[-1])])


def kernel(x_prompt, x_sample, cache_k, cache_v, state_ssm, state_conv, p_prompt, p_sample,
           norm_mix_g, w_in, conv_w, conv_b, dt_bias, a_log, d_skip, ssm_norm_g, w_out,
           norm_ffn_g, w_ffn_gate, w_ffn_up, w_ffn_down, norm_ple_g, w_ple_gate, w_ple_proj,
           final_norm_g):
    batch, seq, d = x_prompt.shape
    dec_batch, dec_seq, _ = x_sample.shape
    depth = w_in.shape[0]
    n_past = cache_k.shape[2]
    tp = batch * seq
    ts = dec_batch * dec_seq
    t = tp + ts
    tm_proj = t // 8
    tm = t // 16
    assert tm_proj % SUBLANES == 0 and tm % SUBLANES == 0 and w_in.shape[2] == PROJ_COLS + N_SSM_HEADS

    h = jnp.concatenate([x_prompt.reshape(tp, d), x_sample.reshape(ts, d)], axis=0)
    p_all = jnp.concatenate([p_prompt.reshape(depth, tp, -1), p_sample.reshape(depth, ts, -1)], axis=1)
    cos2, sin2 = _rope_tables(batch, seq, dec_batch, dec_seq)
    r64, r128 = _expand_mats()
    gsum = _group_sum_mat()

    ck = cache_k.reshape(depth, dec_batch, n_past, D_KV)
    cv = cache_v.reshape(depth, dec_batch, n_past, D_KV)
    st_in = state_ssm.reshape(depth, dec_batch, D_SSM, D_STATE)
    conv_b3 = conv_b.reshape(depth, 1, CONV_DIM)
    dtb = _pad_lanes(dt_bias).reshape(depth, 1, LANES)
    alog = _pad_lanes(a_log).reshape(depth, 1, LANES)
    dsk_e = jnp.repeat(d_skip, SSM_HEAD_DIM, axis=1).reshape(depth, 1, D_SSM)
    ng3 = ssm_norm_g.reshape(depth, 1, D_SSM)

    nk_s = nv_s = st_s = cv_s = None
    k_p, v_p, st_p, cv_p = [], [], [], []
    for i in range(depth):
        wdt = _pad_lanes(w_in[i, :, PROJ_COLS:]).astype(BF16)
        proj, dt_all = _in_proj(h, norm_mix_g[i].reshape(1, d), w_in, i, wdt, cos2, sin2,
                                tm=tm_proj, tn=512)

        att_p = _prompt_attn(proj, batch, seq)
        proj_s = proj[tp:]
        qkv_s = proj_s[:, :COL_Z].reshape(dec_batch, dec_seq, COL_Z)
        att_s, nk_s, nv_s = _sample_attn(qkv_s, ck, cv, i, nk_s, nv_s)

        y_p, st_i, cv_i = _ssd_prompt(proj, dt_all, i, conv_w, conv_b3, dtb, alog, dsk_e, ng3,
                                      r64, r128, batch, seq)
        y_s, st_s, cv_s = _ssd_sample(proj_s, dt_all[tp:], state_conv, st_in, i, conv_w, conv_b3,
                                      dtb, alog, dsk_e, ng3, r64, gsum, st_s, cv_s,
                                      bt=16, t_new=dec_seq)

        att = jnp.concatenate([att_p, att_s.reshape(ts, D_ATT).astype(BF16)], axis=0)
        ssm = jnp.concatenate([y_p, y_s], axis=0)
        h = _out_proj(att, ssm, h, w_out[i].astype(BF16), tm=tm)
        h = _ffn(h, norm_ffn_g[i].reshape(1, d), w_ffn_gate[i].astype(BF16),
                 w_ffn_up[i].astype(BF16), w_ffn_down[i].astype(BF16), tm=tm, tf=512)
        h = _ple(h, p_all[i], norm_ple_g[i].reshape(1, d), w_ple_gate[i].astype(BF16),
                 w_ple_proj[i].astype(BF16), final_norm_g.reshape(1, d),
                 final=(i == depth - 1), tm=tm)

        keep = min(ATT_WINDOW, seq)
        kv_p = proj[:tp].reshape(batch, seq, PROJ_COLS)[:, seq - keep:, COL_K:COL_Z]
        k_p.append(kv_p[..., :D_KV].reshape(batch, keep, N_KV_HEADS, HEAD_DIM))
        v_p.append(kv_p[..., D_KV:].reshape(batch, keep, N_KV_HEADS, HEAD_DIM))
        st_p.append(st_i.reshape(batch, N_SSM_HEADS, SSM_HEAD_DIM, D_STATE))
        cv_p.append(cv_i)

    y_prompt = h[:tp].reshape(batch, seq, d)
    y_sample = h[tp:].reshape(dec_batch, dec_seq, d)
    return (y_prompt, y_sample, jnp.stack(k_p), jnp.stack(v_p), jnp.stack(st_p), jnp.stack(cv_p),
            nk_s.reshape(cache_k.shape), nv_s.reshape(cache_v.shape),
            st_s.reshape(state_ssm.shape), cv_s)
```

```python
import functools

import numpy as np
import jax
import jax.numpy as jnp
from jax import lax
from jax.experimental import pallas as pl
from jax.experimental.pallas import tpu as pltpu

F32 = jnp.float32
BF16 = jnp.bfloat16

N_Q_HEADS = 8
N_KV_HEADS = 4
Q_PER_KV = N_Q_HEADS // N_KV_HEADS
HEAD_DIM = 128
D_ATT = N_Q_HEADS * HEAD_DIM
D_KV = N_KV_HEADS * HEAD_DIM
DILATED_BRANCHES = ((128, 1), (512, 4), (2048, 16))
ATT_WINDOW = 2048
ROPE_THETA = 10000.0
ATT_SCALE = HEAD_DIM ** -0.5
N_SSM_HEADS = 16
SSM_HEAD_DIM = 64
D_SSM = N_SSM_HEADS * SSM_HEAD_DIM
N_SSM_GROUPS = 4
HEADS_PER_GROUP = N_SSM_HEADS // N_SSM_GROUPS
D_STATE = 128
D_BC = N_SSM_GROUPS * D_STATE
CONV_WIDTH = 4
CONV_DIM = D_SSM + 2 * D_BC
SSD_CHUNK = 128
PAST_LEN = 2048
EPS = 1e-6
NEG_INF = -1e30

LANES = 128
SUBLANES = 8
VMEM_LIMIT_BYTES = 56 * 1024 * 1024

COL_Q = 0
COL_K = D_ATT
COL_V = D_ATT + D_KV
COL_Z = D_ATT + 2 * D_KV
COL_X = COL_Z + D_SSM
COL_BC = COL_X + D_SSM
PROJ_COLS = COL_BC + 2 * D_BC

NT_DIMS = (((1,), (1,)), ((), ()))
TN_DIMS = (((0,), (0,)), ((), ()))


def _dot(a, b):
    return jnp.dot(a, b, preferred_element_type=F32)


def _dot_nt(a, b):
    return lax.dot_general(a, b, NT_DIMS, preferred_element_type=F32)


def _rms(x, g):
    return x * lax.rsqrt(jnp.mean(x * x, axis=-1, keepdims=True) + EPS) * g


def _silu(x):
    return x * jax.nn.sigmoid(x)


def _softplus(x):
    return jnp.maximum(x, 0.0) + jnp.log1p(jnp.exp(-jnp.abs(x)))


def _expand(x, r):
    hi = x.astype(BF16)
    r1 = x - hi.astype(F32)
    mid = r1.astype(BF16)
    lo = (r1 - mid.astype(F32)).astype(BF16)
    return _dot(hi, r) + _dot(mid, r) + _dot(lo, r)


def _imod(x, n):
    assert n & (n - 1) == 0, "power-of-two divisor expected"
    return x & (n - 1)


def _idiv(x, n):
    assert n & (n - 1) == 0, "power-of-two divisor expected"
    return x >> (n.bit_length() - 1)


def _params(sem):
    return pltpu.CompilerParams(dimension_semantics=sem, vmem_limit_bytes=VMEM_LIMIT_BYTES)


def _row_tile(rows, cap):
    tile = min(rows, cap)
    while rows % tile:
        tile -= SUBLANES
    return tile


def _in_proj_kernel(x_ref, g_ref, w_ref, wdt_ref, cos_ref, sin_ref, o_ref, dt_ref, hn_ref, *,
                    rot_tiles):
    n = pl.program_id(1)

    @pl.when(n == 0)
    def _():
        hn = _rms(x_ref[...], g_ref[...]).astype(BF16)
        hn_ref[...] = hn
        lane = lax.broadcasted_iota(jnp.int32, wdt_ref.shape, 1)
        wdt = jnp.where(lane < N_SSM_HEADS, wdt_ref[...], 0.0).astype(BF16)
        dt_ref[...] = _dot(hn, wdt)

    acc = _dot(hn_ref[...], w_ref[...].astype(BF16))

    @pl.when(n < rot_tiles)
    def _():
        cos = cos_ref[...]
        sin = sin_ref[...]
        for hh in range(acc.shape[1] // HEAD_DIM):
            xh = acc[:, hh * HEAD_DIM:(hh + 1) * HEAD_DIM]
            o_ref[:, hh * HEAD_DIM:(hh + 1) * HEAD_DIM] = (
                xh * cos + pltpu.roll(xh, HEAD_DIM // 2, axis=1) * sin)

    @pl.when(n >= rot_tiles)
    def _():
        o_ref[...] = acc


def _in_proj(h, g, w_in, layer, cos2, sin2):
    t, d = h.shape
    tn = 512
    tm = _row_tile(cos2.shape[0], 1024)
    assert t % tm == 0 and PROJ_COLS % LANES == 0
    period_tiles = cos2.shape[0] // tm
    dt_block = PROJ_COLS // LANES
    kern = functools.partial(_in_proj_kernel, rot_tiles=(D_ATT + D_KV) // tn)
    return pl.pallas_call(
        kern,
        out_shape=(jax.ShapeDtypeStruct((t, PROJ_COLS), F32),
                   jax.ShapeDtypeStruct((t, LANES), F32)),
        grid=(t // tm, PROJ_COLS // tn),
        in_specs=[
            pl.BlockSpec((tm, d), lambda m, n: (m, 0)),
            pl.BlockSpec((1, d), lambda m, n: (0, 0)),
            pl.BlockSpec((None, d, tn), lambda m, n: (layer, 0, n)),
            pl.BlockSpec((None, d, LANES), lambda m, n: (layer, 0, dt_block)),
            pl.BlockSpec((tm, HEAD_DIM), lambda m, n: (m % period_tiles, 0)),
            pl.BlockSpec((tm, HEAD_DIM), lambda m, n: (m % period_tiles, 0)),
        ],
        out_specs=(pl.BlockSpec((tm, tn), lambda m, n: (m, n)),
                   pl.BlockSpec((tm, LANES), lambda m, n: (m, 0))),
        scratch_shapes=[pltpu.VMEM((tm, d), BF16)],
        compiler_params=_params(("parallel", "arbitrary")),
        name="in_proj",
    )(h, g, w_in, w_in, cos2, sin2)


def _prompt_attn_kernel(q0_ref, q1_ref, k_ref, v_ref, o_ref, *scr, seq, span):
    blk = 128
    nbr = len(DILATED_BRANCHES)
    q_refs = (q0_ref, q1_ref)
    ob = [[scr[(bi * Q_PER_KV + hd) * 2] for hd in range(Q_PER_KV)] for bi in range(nbr)]
    lb = [[scr[(bi * Q_PER_KV + hd) * 2 + 1] for hd in range(Q_PER_KV)] for bi in range(nbr)]
    ii = lax.broadcasted_iota(jnp.int32, (blk, 2 * blk), 0)
    jj = lax.broadcasted_iota(jnp.int32, (blk, 2 * blk), 1)
    dist = ii + blk - jj

    def rows(start, dil):
        return pl.ds(start, blk, stride=dil) if dil > 1 else pl.ds(start, blk)

    def piece(ref, start, dil):
        return ref[rows(start, dil), :].astype(BF16)

    def attend(bi, dil, base, local, kp, kc, vp, vc, first):
        kb = jnp.concatenate([kp, kc], axis=0)
        vb = jnp.concatenate([vp, vc], axis=0)
        hi = blk if first is None else jnp.where(first, ii, blk)
        mask = (dist >= 0) & (dist <= hi)
        for hd in range(Q_PER_KV):
            q = piece(q_refs[hd], base, dil)
            s = jnp.where(mask, _dot_nt(q, kb) * ATT_SCALE, NEG_INF)
            m = jnp.max(s, axis=1, keepdims=True)
            e = jnp.exp(s - m)
            ssum = jnp.sum(e, axis=1, keepdims=True)
            p = (e / ssum).astype(BF16)
            ob[bi][hd][rows(local, dil), :] = _dot(p, vb)
            lb[bi][hd][rows(local, dil), :] = jnp.broadcast_to(m + jnp.log(ssum), (blk, HEAD_DIM))

    def span_body(s, carry):
        s0 = pl.multiple_of(s * span, span)
        for bi, (win, dil) in enumerate(DILATED_BRANCHES):
            assert win // dil == blk and span % (dil * blk) == 0
            stride_rows = blk * dil
            per_class = span // stride_rows
            if per_class >= 2:
                half = per_class // 2

                def pair(idx, c, bi=bi, dil=dil, stride_rows=stride_rows, half=half):
                    r = idx // half if dil > 1 else 0
                    i = idx - r * half
                    local0 = r + (2 * i) * stride_rows
                    base0 = s0 + local0
                    first = (s == 0) & (i == 0)
                    prev = jnp.maximum(base0 - stride_rows, r)
                    km, vm = piece(k_ref, prev, dil), piece(v_ref, prev, dil)
                    k0, v0 = piece(k_ref, base0, dil), piece(v_ref, base0, dil)
                    k1 = piece(k_ref, base0 + stride_rows, dil)
                    v1 = piece(v_ref, base0 + stride_rows, dil)
                    attend(bi, dil, base0, local0, km, k0, vm, v0, first)
                    attend(bi, dil, base0 + stride_rows, local0 + stride_rows, k0, k1, v0, v1, None)
                    return c

                lax.fori_loop(0, dil * half, pair, 0)
            else:
                def two(idx, c, bi=bi, dil=dil, stride_rows=stride_rows):
                    for e in range(2):
                        r = 2 * idx + e
                        base = s0 + r
                        prev = jnp.maximum(base - stride_rows, r)
                        attend(bi, dil, base, r, piece(k_ref, prev, dil), piece(k_ref, base, dil),
                               piece(v_ref, prev, dil), piece(v_ref, base, dil), s == 0)
                    return c

                lax.fori_loop(0, dil // 2, two, 0)

        rows_per = 256

        def merge(c, carry2):
            loc = pl.multiple_of(c * rows_per, rows_per)
            sl = pl.ds(loc, rows_per)
            for hd in range(Q_PER_KV):
                ls = [lb[bi][hd][sl, :] for bi in range(nbr)]
                mx = functools.reduce(jnp.maximum, ls)
                ws = [jnp.exp(l - mx) for l in ls]
                num = functools.reduce(lambda a, b: a + b,
                                       [w * ob[bi][hd][sl, :] for bi, w in enumerate(ws)])
                den = functools.reduce(lambda a, b: a + b, ws)
                o_ref[pl.ds(s0 + loc, rows_per), hd * HEAD_DIM:(hd + 1) * HEAD_DIM] = (
                    (num / den).astype(o_ref.dtype))
            return carry2

        lax.fori_loop(0, span // rows_per, merge, 0)
        return carry

    lax.fori_loop(0, seq // span, span_body, 0)


def _prompt_attn(proj, batch, seq):
    span = max(w for w, _ in DILATED_BRANCHES)
    assert seq % span == 0 and Q_PER_KV == 2
    kern = functools.partial(_prompt_attn_kernel, seq=seq, span=span)
    kq = COL_K // HEAD_DIM
    vq = COL_V // HEAD_DIM
    n_scr = len(DILATED_BRANCHES) * Q_PER_KV * 2
    return pl.pallas_call(
        kern,
        out_shape=jax.ShapeDtypeStruct((batch * seq, D_ATT), BF16),
        grid=(batch, N_KV_HEADS),
        in_specs=[
            pl.BlockSpec((seq, HEAD_DIM), lambda b, g: (b, Q_PER_KV * g)),
            pl.BlockSpec((seq, HEAD_DIM), lambda b, g: (b, Q_PER_KV * g + 1)),
            pl.BlockSpec((seq, HEAD_DIM), lambda b, g: (b, kq + g)),
            pl.BlockSpec((seq, HEAD_DIM), lambda b, g: (b, vq + g)),
        ],
        out_specs=pl.BlockSpec((seq, Q_PER_KV * HEAD_DIM), lambda b, g: (b, g)),
        scratch_shapes=[pltpu.VMEM((span, HEAD_DIM), F32) for _ in range(n_scr)],
        compiler_params=_params(("parallel", "arbitrary")),
        name="prompt_attn",
    )(proj, proj, proj, proj)


def _sample_attn_kernel(*refs, n_past, t_new, aliased):
    if aliased:
        qkv_ref, ck_ref, cv_ref, _, _, att_ref, nk_ref, nv_ref = refs
    else:
        qkv_ref, ck_ref, cv_ref, att_ref, nk_ref, nv_ref = refs
    qkv = qkv_ref[...]
    k_new = qkv[:, COL_K:COL_K + D_KV]
    v_new = qkv[:, COL_V:COL_V + D_KV]

    nrow = n_past * N_KV_HEADS
    shift = t_new * N_KV_HEADS
    assert shift % SUBLANES == 0
    step = 1024
    for src, new, dst in ((ck_ref, k_new, nk_ref), (cv_ref, v_new, nv_ref)):
        for r0 in range(0, nrow - shift, step):
            n = min(step, nrow - shift - r0)
            dst[r0:r0 + n, :] = src[r0 + shift:r0 + shift + n, :]
        for j in range(t_new):
            for g in range(N_KV_HEADS):
                r = nrow - shift + j * N_KV_HEADS + g
                dst[r:r + 1, :] = new[j:j + 1, g * HEAD_DIM:(g + 1) * HEAD_DIM]

    nq = Q_PER_KV * t_new
    cidx = lax.broadcasted_iota(jnp.int32, (nq, n_past), 1)
    tok = _imod(lax.broadcasted_iota(jnp.int32, (nq, n_past), 0), t_new)
    dist_c = n_past + tok - cidx
    tok1 = _imod(lax.broadcasted_iota(jnp.int32, (nq, 1), 0), t_new)

    for g in range(N_KV_HEADS):
        qg = jnp.concatenate(
            [qkv[:, (g * Q_PER_KV + r) * HEAD_DIM:(g * Q_PER_KV + r + 1) * HEAD_DIM]
             for r in range(Q_PER_KV)], axis=0).astype(BF16)
        kc = ck_ref[pl.ds(g, n_past, stride=N_KV_HEADS), :].astype(BF16)
        vc = cv_ref[pl.ds(g, n_past, stride=N_KV_HEADS), :].astype(BF16)
        kn = k_new[:, g * HEAD_DIM:(g + 1) * HEAD_DIM].astype(BF16).astype(F32)
        vn = v_new[:, g * HEAD_DIM:(g + 1) * HEAD_DIM].astype(BF16).astype(F32)
        s_c = _dot_nt(qg, kc) * ATT_SCALE
        qf = qg.astype(F32)
        s_n = [jnp.sum(qf * kn[j:j + 1, :], axis=1, keepdims=True) * ATT_SCALE
               for j in range(t_new)]

        probs, new_terms, lses = [], [], []
        for win, dil in DILATED_BRANCHES:
            assert win <= n_past
            mask_c = (_imod(dist_c, dil) == 0) & (dist_c <= win)
            sc = jnp.where(mask_c, s_c, NEG_INF)
            m = jnp.max(sc, axis=1, keepdims=True)
            sn = []
            for j in range(t_new):
                dn = tok1 - j
                mask_n = (dn >= 0) & (_imod(dn, dil) == 0)
                snj = jnp.where(mask_n, s_n[j], NEG_INF)
                sn.append(snj)
                m = jnp.maximum(m, snj)
            ec = jnp.exp(sc - m)
            en = [jnp.exp(x - m) for x in sn]
            ssum = jnp.sum(ec, axis=1, keepdims=True)
            for x in en:
                ssum = ssum + x
            probs.append((ec / ssum).astype(BF16))
            o_new = jnp.zeros((nq, HEAD_DIM), F32)
            for j in range(t_new):
                o_new = o_new + (en[j] / ssum).astype(BF16).astype(F32) * vn[j:j + 1, :]
            new_terms.append(o_new)
            lses.append(m + jnp.log(ssum))

        o_all = _dot(jnp.concatenate(probs, axis=0), vc)
        mx = functools.reduce(jnp.maximum, lses)
        ws = [jnp.exp(l - mx) for l in lses]
        den = functools.reduce(lambda a, b: a + b, ws)
        o = jnp.zeros((nq, HEAD_DIM), F32)
        for i, w in enumerate(ws):
            o = o + w * (o_all[i * nq:(i + 1) * nq, :] + new_terms[i])
        o = o / den
        for r in range(Q_PER_KV):
            att_ref[:, (g * Q_PER_KV + r) * HEAD_DIM:(g * Q_PER_KV + r + 1) * HEAD_DIM] = (
                o[r * t_new:(r + 1) * t_new, :])


def _sample_attn(qkv_s, cache_k, cache_v, layer, prev_k, prev_v):
    depth, bsz, nrow, _ = cache_k.shape
    n_past = nrow // N_KV_HEADS
    t_new = qkv_s.shape[1]
    aliased = prev_k is not None
    kern = functools.partial(_sample_attn_kernel, n_past=n_past, t_new=t_new, aliased=aliased)
    win_spec = pl.BlockSpec((None, None, nrow, HEAD_DIM), lambda b: (layer, b, 0, 0))
    in_specs = [pl.BlockSpec((None, t_new, qkv_s.shape[2]), lambda b: (b, 0, 0)), win_spec, win_spec]
    args = [qkv_s, cache_k, cache_v]
    aliases = {}
    if aliased:
        in_specs += [pl.BlockSpec(memory_space=pl.ANY), pl.BlockSpec(memory_space=pl.ANY)]
        args += [prev_k, prev_v]
        aliases = {3: 1, 4: 2}
    return pl.pallas_call(
        kern,
        out_shape=(jax.ShapeDtypeStruct((bsz, t_new, D_ATT), F32),
                   jax.ShapeDtypeStruct(cache_k.shape, cache_k.dtype),
                   jax.ShapeDtypeStruct(cache_v.shape, cache_v.dtype)),
        grid=(bsz,),
        in_specs=in_specs,
        out_specs=(pl.BlockSpec((None, t_new, D_ATT), lambda b: (b, 0, 0)), win_spec, win_spec),
        input_output_aliases=aliases,
        compiler_params=_params(("parallel",)),
        name="sample_attn",
    )(*args)


def _expand_mats():
    r64 = np.zeros((LANES, D_SSM), np.float32)
    r128 = np.zeros((LANES, N_SSM_HEADS * LANES), np.float32)
    for h in range(N_SSM_HEADS):
        r64[h, h * SSM_HEAD_DIM:(h + 1) * SSM_HEAD_DIM] = 1.0
        r128[h, h * LANES:(h + 1) * LANES] = 1.0
    return jnp.asarray(r64, BF16), jnp.asarray(r128, BF16)


def _group_sum_mat():
    g = np.zeros((D_BC, D_SSM), np.float32)
    for grp in range(N_SSM_GROUPS):
        g[grp * D_STATE:(grp + 1) * D_STATE,
          grp * HEADS_PER_GROUP * SSM_HEAD_DIM:(grp + 1) * HEADS_PER_GROUP * SSM_HEAD_DIM] = 1.0
    return jnp.asarray(g, BF16)


def _conv_silu(x, tail, cw, cb):
    n = x.shape[0]
    xp = jnp.concatenate([tail, x], axis=0)
    out = cb + cw[3:4, :] * x
    for w in range(CONV_WIDTH - 1):
        off = SUBLANES - (CONV_WIDTH - 1) + w
        out = out + cw[w:w + 1, :] * xp[off:off + n, :]
    return _silu(out)


def _cumsum_rows(x, seg=None):
    n = x.shape[0]
    rows = lax.broadcasted_iota(jnp.int32, x.shape, 0)
    pos = rows if seg is None else _imod(rows, seg)
    limit = n if seg is None else seg
    sh = 1
    while sh < limit:
        x = x + jnp.where(pos >= sh, pltpu.roll(x, sh, axis=0), 0.0)
        sh *= 2
    return x


def _lane_col_block(row):
    return jnp.broadcast_to(row, (LANES, LANES)).T


def _ssd_prompt_kernel(z_ref, xs_ref, bc_ref, dt_ref, cw_ref, cb_ref, dtb_ref, alog_ref, dsk_ref,
                       ng_ref, r64_ref, r128_ref, y_ref, st_ref, cv_ref, h_scr, tail_scr):
    c = pl.program_id(1)
    last = pl.num_programs(1) - 1
    cs = SSD_CHUNK

    @pl.when(c == 0)
    def _():
        h_scr[...] = jnp.zeros_like(h_scr)
        tail_scr[...] = jnp.zeros_like(tail_scr)

    xraw = xs_ref[...]
    bcraw = bc_ref[...]
    tail = tail_scr[...]
    cw = cw_ref[...]
    cb = cb_ref[...]
    xs = _conv_silu(xraw, tail[:, :D_SSM], cw[:, :D_SSM], cb[:, :D_SSM])
    bcm = _conv_silu(bcraw, tail[:, D_SSM:], cw[:, D_SSM:], cb[:, D_SSM:])
    tail_scr[:, :D_SSM] = xraw[cs - SUBLANES:, :]
    tail_scr[:, D_SSM:] = bcraw[cs - SUBLANES:, :]

    @pl.when(c == last)
    def _():
        cv_ref[:, :D_SSM] = xraw[cs - (CONV_WIDTH - 1):, :]
        cv_ref[:, D_SSM:] = bcraw[cs - (CONV_WIDTH - 1):, :]

    dt = _softplus(dt_ref[...] + dtb_ref[...])
    a = -jnp.exp(alog_ref[...])
    acum = _cumsum_rows(dt * a)
    acum_t = acum.T
    r64 = r64_ref[...]
    dt_e = _expand(dt, r64)
    ac_e = _expand(acum, r64)
    col_b = _expand(acum, r128_ref[...])
    last_e = ac_e[cs - 1:cs, :]
    ea_e = jnp.exp(ac_e)
    xdt = xs * dt_e
    xte = (xdt * jnp.exp(last_e - ac_e)).astype(BF16)

    ti = lax.broadcasted_iota(jnp.int32, (cs, cs), 0)
    si = lax.broadcasted_iota(jnp.int32, (cs, cs), 1)
    causal = ti >= si
    lane = lax.broadcasted_iota(jnp.int32, (cs, LANES), 1)
    lo_half = lane < SSM_HEAD_DIM

    gw = HEADS_PER_GROUP * SSM_HEAD_DIM
    y_diag, y_off, states = [], [], []
    for g in range(N_SSM_GROUPS):
        bg = bcm[:, g * D_STATE:(g + 1) * D_STATE].astype(BF16)
        cg = bcm[:, D_BC + g * D_STATE:D_BC + (g + 1) * D_STATE].astype(BF16)
        cbt = _dot_nt(cg, bg)
        h_in = h_scr[g * gw:(g + 1) * gw, :]
        y_off.append(_dot_nt(cg, h_in.astype(BF16)))
        states.append(lax.dot_general(xte[:, g * gw:(g + 1) * gw], bg, TN_DIMS,
                                      preferred_element_type=F32))
        for k in range(HEADS_PER_GROUP // 2):
            pair = g * (HEADS_PER_GROUP // 2) + k
            xp = xdt[:, pair * LANES:(pair + 1) * LANES]
            yd = jnp.zeros((cs, LANES), F32)
            for e in range(2):
                h = 2 * pair + e
                seg = col_b[:, h * LANES:(h + 1) * LANES] - jnp.broadcast_to(acum_t[h:h + 1, :], (cs, cs))
                dec = jnp.exp(jnp.where(causal, seg, NEG_INF))
                cbh = (cbt * dec).astype(BF16)
                xh = jnp.where(lo_half if e == 0 else jnp.logical_not(lo_half), xp, 0.0).astype(BF16)
                yd = yd + _dot(cbh, xh)
            y_diag.append(yd)

    y = (jnp.concatenate(y_diag, axis=1) + jnp.concatenate(y_off, axis=1) * ea_e
         + dsk_ref[...] * xs)
    dec_rows = jnp.exp(jnp.concatenate(
        [_lane_col_block(last_e[:, k * LANES:(k + 1) * LANES]) for k in range(D_SSM // LANES)],
        axis=0))
    h_new = h_scr[...] * dec_rows + jnp.concatenate(states, axis=0)
    h_scr[...] = h_new

    @pl.when(c == last)
    def _():
        st_ref[...] = h_new

    y_ref[...] = _rms(y * _silu(z_ref[...]), ng_ref[...]).astype(y_ref.dtype)


def _ssd_prompt(proj, dt_all, layer, conv_w, conv_b3, dtb, alog, dsk_e, ng3, r64, r128, batch, seq):
    cs = SSD_CHUNK
    nc = seq // cs
    zc, xc, bcc = COL_Z // D_SSM, COL_X // D_SSM, COL_BC // D_SSM
    const2 = lambda b, c: (0, 0)
    return pl.pallas_call(
        _ssd_prompt_kernel,
        out_shape=(jax.ShapeDtypeStruct((batch * seq, D_SSM), BF16),
                   jax.ShapeDtypeStruct((batch, D_SSM, D_STATE), F32),
                   jax.ShapeDtypeStruct((batch, CONV_WIDTH - 1, CONV_DIM), F32)),
        grid=(batch, nc),
        in_specs=[
            pl.BlockSpec((cs, D_SSM), lambda b, c: (b * nc + c, zc)),
            pl.BlockSpec((cs, D_SSM), lambda b, c: (b * nc + c, xc)),
            pl.BlockSpec((cs, D_SSM), lambda b, c: (b * nc + c, bcc)),
            pl.BlockSpec((cs, LANES), lambda b, c: (b * nc + c, 0)),
            pl.BlockSpec((None, CONV_WIDTH, CONV_DIM), lambda b, c: (layer, 0, 0)),
            pl.BlockSpec((None, 1, CONV_DIM), lambda b, c: (layer, 0, 0)),
            pl.BlockSpec((None, 1, LANES), lambda b, c: (layer, 0, 0)),
            pl.BlockSpec((None, 1, LANES), lambda b, c: (layer, 0, 0)),
            pl.BlockSpec((None, 1, D_SSM), lambda b, c: (layer, 0, 0)),
            pl.BlockSpec((None, 1, D_SSM), lambda b, c: (layer, 0, 0)),
            pl.BlockSpec(r64.shape, const2),
            pl.BlockSpec(r128.shape, const2),
        ],
        out_specs=(pl.BlockSpec((cs, D_SSM), lambda b, c: (b * nc + c, 0)),
                   pl.BlockSpec((None, D_SSM, D_STATE), lambda b, c: (b, 0, 0)),
                   pl.BlockSpec((None, CONV_WIDTH - 1, CONV_DIM), lambda b, c: (b, 0, 0))),
        scratch_shapes=[pltpu.VMEM((D_SSM, D_STATE), F32), pltpu.VMEM((SUBLANES, CONV_DIM), F32)],
        compiler_params=_params(("parallel", "arbitrary")),
        name="ssd_prompt",
    )(proj, proj, proj, dt_all, conv_w, conv_b3, dtb, alog, dsk_e, ng3, r64, r128)


def _ssd_sample_kernel(*refs, bt, t_new, aliased):
    if aliased:
        (z_ref, xs_ref, bc_ref, dt_ref, cst_ref, h0_ref, cw_ref, cb_ref, dtb_ref, alog_ref, dsk_ref,
         ng_ref, r64_ref, gs_ref, _, _, y_ref, st_ref, cv_ref, xbc_scr) = refs
    else:
        (z_ref, xs_ref, bc_ref, dt_ref, cst_ref, h0_ref, cw_ref, cb_ref, dtb_ref, alog_ref, dsk_ref,
         ng_ref, r64_ref, gs_ref, y_ref, st_ref, cv_ref, xbc_scr) = refs
    rows = bt * t_new
    kw = CONV_WIDTH - 1
    cw = cw_ref[...]
    cb = cb_ref[...]

    for b in range(bt):
        xb = jnp.concatenate([xs_ref[b * t_new:(b + 1) * t_new, :],
                              bc_ref[b * t_new:(b + 1) * t_new, :]], axis=1)
        xp = jnp.concatenate([cst_ref[b], xb], axis=0)
        out = cb
        for w in range(CONV_WIDTH):
            out = out + cw[w:w + 1, :] * xp[w:w + t_new, :]
        xbc_scr[b * t_new:(b + 1) * t_new, :] = _silu(out)
        cv_ref[b] = xp[t_new:t_new + kw, :]

    xbc = xbc_scr[...]
    xs = xbc[:, :D_SSM]
    bm = xbc[:, D_SSM:D_SSM + D_BC]
    cm = xbc[:, D_SSM + D_BC:]

    tpos = _imod(lax.broadcasted_iota(jnp.int32, (rows, 1), 0), t_new)
    dt = _softplus(dt_ref[...] + dtb_ref[...])
    a = -jnp.exp(alog_ref[...])
    acum = _cumsum_rows(dt * a, seg=t_new)
    r64 = r64_ref[...]
    dt_e = _expand(dt, r64)
    ac_e = _expand(acum, r64)
    v = jnp.where(tpos == t_new - 1, ac_e, 0.0)
    last_e = v
    for d in range(1, t_new):
        last_e = last_e + pltpu.roll(v, rows - d, axis=0)
    ea_e = jnp.exp(ac_e)
    xdt = xs * dt_e
    xte = xdt * jnp.exp(last_e - ac_e)

    cmb = cm.astype(BF16).astype(F32)
    bmb = bm.astype(BF16).astype(F32)
    gs = gs_ref[...]
    y = dsk_ref[...] * xs
    for d in range(t_new):
        if d == 0:
            b_s, x_s, a_s = bmb, xdt, ac_e
        else:
            b_s = pltpu.roll(bmb, d, axis=0)
            x_s = pltpu.roll(xdt, d, axis=0)
            a_s = pltpu.roll(ac_e, d, axis=0)
        cb_e = _expand(cmb * b_s, gs)
        dec = jnp.exp(jnp.where(tpos >= d, ac_e - a_s, NEG_INF))
        y = y + cb_e * dec * x_s

    gw = HEADS_PER_GROUP * SSM_HEAD_DIM
    pad = jnp.zeros((LANES - rows, LANES), F32) if rows < LANES else None
    rowb = _idiv(lax.broadcasted_iota(jnp.int32, (rows, 1), 0), t_new)
    colb = _idiv(lax.broadcasted_iota(jnp.int32, (1, LANES), 1), t_new)
    xte_t = []
    for k in range(D_SSM // LANES):
        blk = xte[:, k * LANES:(k + 1) * LANES]
        if pad is not None:
            blk = jnp.concatenate([blk, pad], axis=0)
        xte_t.append(blk.T)
    e_last = jnp.exp(last_e)
    y_off = [jnp.zeros((rows, gw), F32) for _ in range(N_SSM_GROUPS)]
    for b in range(bt):
        h0 = h0_ref[b]
        new_rows = []
        for g in range(N_SSM_GROUPS):
            cg = cm[:, g * D_STATE:(g + 1) * D_STATE]
            bg = bm[:, g * D_STATE:(g + 1) * D_STATE]
            if pad is not None:
                bg = jnp.concatenate([bg, pad], axis=0)
            bg = bg.astype(BF16)
            h0g = h0[g * gw:(g + 1) * gw, :]
            cgb = jnp.where(rowb == b, cg, 0.0).astype(BF16)
            y_off[g] = y_off[g] + _dot_nt(cgb, h0g.astype(BF16))
            for k in range(gw // LANES):
                blk = g * (gw // LANES) + k
                lhs = jnp.where(colb == b, xte_t[blk], 0.0).astype(BF16)
                st = _dot(lhs, bg)
                r = b * t_new + t_new - 1
                dec = _lane_col_block(e_last[r:r + 1, blk * LANES:(blk + 1) * LANES])
                new_rows.append(h0[blk * LANES:(blk + 1) * LANES, :] * dec + st)
        st_ref[b] = jnp.concatenate(new_rows, axis=0)

    y = y + jnp.concatenate(y_off, axis=1) * ea_e
    y_ref[...] = _rms(y * _silu(z_ref[...]), ng_ref[...]).astype(y_ref.dtype)


def _ssd_sample(proj_s, dt_s, state_conv, state_ssm, layer, conv_w, conv_b3, dtb, alog, dsk_e, ng3,
                r64, gsum, prev_st, prev_cv, *, bt, t_new):
    depth, bsz = state_ssm.shape[:2]
    rows = bt * t_new
    zc, xc, bcc = COL_Z // D_SSM, COL_X // D_SSM, COL_BC // D_SSM
    aliased = prev_st is not None
    kern = functools.partial(_ssd_sample_kernel, bt=bt, t_new=t_new, aliased=aliased)
    const2 = lambda i: (0, 0)
    in_specs = [
        pl.BlockSpec((rows, D_SSM), lambda i: (i, zc)),
        pl.BlockSpec((rows, D_SSM), lambda i: (i, xc)),
        pl.BlockSpec((rows, D_SSM), lambda i: (i, bcc)),
        pl.BlockSpec((rows, LANES), lambda i: (i, 0)),
        pl.BlockSpec((None, bt, CONV_WIDTH - 1, CONV_DIM), lambda i: (layer, i, 0, 0)),
        pl.BlockSpec((None, bt, D_SSM, D_STATE), lambda i: (layer, i, 0, 0)),
        pl.BlockSpec((None, CONV_WIDTH, CONV_DIM), lambda i: (layer, 0, 0)),
        pl.BlockSpec((None, 1, CONV_DIM), lambda i: (layer, 0, 0)),
        pl.BlockSpec((None, 1, LANES), lambda i: (layer, 0, 0)),
        pl.BlockSpec((None, 1, LANES), lambda i: (layer, 0, 0)),
        pl.BlockSpec((None, 1, D_SSM), lambda i: (layer, 0, 0)),
        pl.BlockSpec((None, 1, D_SSM), lambda i: (layer, 0, 0)),
        pl.BlockSpec(r64.shape, const2),
        pl.BlockSpec(gsum.shape, const2),
    ]
    args = [proj_s, proj_s, proj_s, dt_s, state_conv, state_ssm, conv_w, conv_b3, dtb, alog, dsk_e,
            ng3, r64, gsum]
    aliases = {}
    if aliased:
        in_specs += [pl.BlockSpec(memory_space=pl.ANY), pl.BlockSpec(memory_space=pl.ANY)]
        args += [prev_st, prev_cv]
        aliases = {14: 1, 15: 2}
    return pl.pallas_call(
        kern,
        out_shape=(jax.ShapeDtypeStruct((bsz * t_new, D_SSM), BF16),
                   jax.ShapeDtypeStruct(state_ssm.shape, state_ssm.dtype),
                   jax.ShapeDtypeStruct(state_conv.shape, state_conv.dtype)),
        grid=(bsz // bt,),
        in_specs=in_specs,
        out_specs=(pl.BlockSpec((rows, D_SSM), lambda i: (i, 0)),
                   pl.BlockSpec((None, bt, D_SSM, D_STATE), lambda i: (layer, i, 0, 0)),
                   pl.BlockSpec((None, bt, CONV_WIDTH - 1, CONV_DIM), lambda i: (layer, i, 0, 0))),
        scratch_shapes=[pltpu.VMEM((rows, CONV_DIM), F32)],
        input_output_aliases=aliases,
        compiler_params=_params(("parallel",)),
        name="ssd_sample",
    )(*args)


def _resident(shape, index_map):
    return pl.BlockSpec(shape, index_map, pipeline_mode=pl.Buffered(1))


def _out_proj_kernel(att_ref, ssm_ref, h_ref, w_ref, o_ref, wb_ref):
    @pl.when(pl.program_id(0) == 0)
    def _():
        wb_ref[...] = w_ref[...].astype(BF16)

    acc = _dot(att_ref[...], wb_ref[:D_ATT, :]) + _dot(ssm_ref[...], wb_ref[D_ATT:, :])
    o_ref[...] = h_ref[...] + acc


def _out_proj(att, ssm, h, w_out, layer):
    t, d = h.shape
    tm = _row_tile(t, 512)
    return pl.pallas_call(
        _out_proj_kernel,
        out_shape=jax.ShapeDtypeStruct((t, d), F32),
        grid=(t // tm,),
        in_specs=[
            pl.BlockSpec((tm, D_ATT), lambda m: (m, 0)),
            pl.BlockSpec((tm, D_SSM), lambda m: (m, 0)),
            pl.BlockSpec((tm, d), lambda m: (m, 0)),
            _resident((None,) + w_out.shape[1:], lambda m: (layer, 0, 0)),
        ],
        out_specs=pl.BlockSpec((tm, d), lambda m: (m, 0)),
        scratch_shapes=[pltpu.VMEM(w_out.shape[1:], BF16)],
        compiler_params=_params(("arbitrary",)),
        name="out_proj",
    )(att, ssm, h, w_out)


def _ffn_kernel(h_ref, g_ref, wg_ref, wu_ref, wd_ref, o_ref, hf_ref):
    f = pl.program_id(1)

    @pl.when(f == 0)
    def _():
        h = h_ref[...]
        hf_ref[...] = _rms(h, g_ref[...]).astype(BF16)
        o_ref[...] = h

    hf = hf_ref[...]
    act = _silu(_dot(hf, wg_ref[...].astype(BF16))) * _dot(hf, wu_ref[...].astype(BF16))
    o_ref[...] += _dot(act.astype(BF16), wd_ref[...].astype(BF16))


def _ffn(h, g, wg, wu, wd, layer):
    t, d = h.shape
    dff = wg.shape[2]
    tm = _row_tile(t, 1024)
    tf = 256
    assert dff % tf == 0
    return pl.pallas_call(
        _ffn_kernel,
        out_shape=jax.ShapeDtypeStruct((t, d), F32),
        grid=(t // tm, dff // tf),
        in_specs=[
            pl.BlockSpec((tm, d), lambda m, f: (m, 0), pipeline_mode=pl.Buffered(1)),
            pl.BlockSpec((1, d), lambda m, f: (0, 0)),
            pl.BlockSpec((None, d, tf), lambda m, f: (layer, 0, f)),
            pl.BlockSpec((None, d, tf), lambda m, f: (layer, 0, f)),
            pl.BlockSpec((None, tf, d), lambda m, f: (layer, f, 0)),
        ],
        out_specs=pl.BlockSpec((tm, d), lambda m, f: (m, 0)),
        scratch_shapes=[pltpu.VMEM((tm, d), BF16)],
        compiler_params=_params(("parallel", "arbitrary")),
        name="ffn",
    )(h, g, wg, wu, wd)


def _ple_kernel(h_ref, p_ref, g_ref, wg_ref, wp_ref, gf_ref, o_ref, wgb_ref, wpb_ref, *, final, tn):
    @pl.when(pl.program_id(0) == 0)
    def _():
        wgb_ref[...] = wg_ref[...].astype(BF16)
        wpb_ref[...] = wp_ref[...].astype(BF16)

    hn = _rms(h_ref[...], g_ref[...]).astype(BF16)
    pb = p_ref[...].astype(BF16)
    for c in range(h_ref.shape[1] // tn):
        cols = slice(c * tn, (c + 1) * tn)
        gate = jax.nn.sigmoid(_dot(hn, wgb_ref[:, cols]))
        o_ref[:, cols] = h_ref[:, cols] + gate * _dot(pb, wpb_ref[:, cols])
    if final:
        o_ref[...] = _rms(o_ref[...], gf_ref[...])


def _ple(h, p, g, wg, wp, gf, layer, *, final):
    t, d = h.shape
    tm = _row_tile(t, 512)
    kern = functools.partial(_ple_kernel, final=final, tn=512)
    return pl.pallas_call(
        kern,
        out_shape=jax.ShapeDtypeStruct((t, d), F32),
        grid=(t // tm,),
        in_specs=[
            pl.BlockSpec((tm, d), lambda m: (m, 0)),
            pl.BlockSpec((None, tm, p.shape[2]), lambda m: (layer, m, 0)),
            pl.BlockSpec((1, d), lambda m: (0, 0)),
            _resident((None,) + wg.shape[1:], lambda m: (layer, 0, 0)),
            _resident((None,) + wp.shape[1:], lambda m: (layer, 0, 0)),
            pl.BlockSpec((1, d), lambda m: (0, 0)),
        ],
        out_specs=pl.BlockSpec((tm, d), lambda m: (m, 0)),
        scratch_shapes=[pltpu.VMEM(wg.shape[1:], BF16), pltpu.VMEM(wp.shape[1:], BF16)],
        compiler_params=_params(("arbitrary",)),
        name="ple",
    )(h, p, g, wg, wp, gf)


def _rope_tables(pos):
    half = HEAD_DIM // 2
    inv_freq = ROPE_THETA ** (-jnp.arange(half, dtype=F32) / half)
    ang = pos.astype(F32)[:, None] * inv_freq[None, :]
    cos, sin = jnp.cos(ang), jnp.sin(ang)
    return jnp.concatenate([cos, cos], axis=1), jnp.concatenate([-sin, sin], axis=1)


def _pad_lanes(x):
    return jnp.pad(x, [(0, 0)] * (x.ndim - 1) + [(0, LANES - x.shape[-1])])


def kernel(x_prompt, x_sample, cache_k, cache_v, state_ssm, state_conv, p_prompt, p_sample,
           norm_mix_g, w_in, conv_w, conv_b, dt_bias, a_log, d_skip, ssm_norm_g, w_out,
           norm_ffn_g, w_ffn_gate, w_ffn_up, w_ffn_down, norm_ple_g, w_ple_gate, w_ple_proj,
           final_norm_g):
    batch, seq, d = x_prompt.shape
    dec_batch, dec_seq, _ = x_sample.shape
    depth = w_in.shape[0]
    n_past = cache_k.shape[2]
    tp = batch * seq
    ts = dec_batch * dec_seq
    assert w_in.shape[2] == PROJ_COLS + N_SSM_HEADS and n_past == PAST_LEN

    cos_p, sin_p = _rope_tables(jnp.arange(seq, dtype=jnp.int32))
    cos_s, sin_s = _rope_tables(jnp.tile(PAST_LEN + jnp.arange(dec_seq, dtype=jnp.int32), dec_batch))
    r64, r128 = _expand_mats()
    gsum = _group_sum_mat()

    ck = cache_k.reshape(depth, dec_batch, n_past * N_KV_HEADS, HEAD_DIM)
    cv = cache_v.reshape(depth, dec_batch, n_past * N_KV_HEADS, HEAD_DIM)
    st_in = state_ssm.reshape(depth, dec_batch, D_SSM, D_STATE)
    pp = p_prompt.reshape(depth, tp, -1)
    ps = p_sample.reshape(depth, ts, -1)
    conv_b3 = conv_b.reshape(depth, 1, CONV_DIM)
    dtb = _pad_lanes(dt_bias).reshape(depth, 1, LANES)
    alog = _pad_lanes(a_log).reshape(depth, 1, LANES)
    dsk_e = jnp.repeat(d_skip, SSM_HEAD_DIM, axis=1).reshape(depth, 1, D_SSM)
    ng3 = ssm_norm_g.reshape(depth, 1, D_SSM)
    gf = final_norm_g.reshape(1, d)

    hp = x_prompt.reshape(tp, d)
    hs = x_sample.reshape(ts, d)
    nk_s = nv_s = st_s = cv_s = None
    k_p, v_p, st_p, cv_p = [], [], [], []
    for i in range(depth):
        g_mix = norm_mix_g[i].reshape(1, d)
        g_ffn = norm_ffn_g[i].reshape(1, d)
        g_ple = norm_ple_g[i].reshape(1, d)
        last = i == depth - 1

        proj, dt_p = _in_proj(hp, g_mix, w_in, i, cos_p, sin_p)
        att_p = _prompt_attn(proj, batch, seq)
        y_p, st_i, cv_i = _ssd_prompt(proj, dt_p, i, conv_w, conv_b3, dtb, alog, dsk_e, ng3,
                                      r64, r128, batch, seq)
        hp = _out_proj(att_p, y_p, hp, w_out, i)
        hp = _ffn(hp, g_ffn, w_ffn_gate, w_ffn_up, w_ffn_down, i)
        hp = _ple(hp, pp, g_ple, w_ple_gate, w_ple_proj, gf, i, final=last)

        keep = min(ATT_WINDOW, seq)
        kv_p = proj.reshape(batch, seq, PROJ_COLS)[:, seq - keep:, COL_K:COL_Z]
        k_p.append(kv_p[..., :D_KV].reshape(batch, keep, N_KV_HEADS, HEAD_DIM))
        v_p.append(kv_p[..., D_KV:].reshape(batch, keep, N_KV_HEADS, HEAD_DIM))
        st_p.append(st_i.reshape(batch, N_SSM_HEADS, SSM_HEAD_DIM, D_STATE))
        cv_p.append(cv_i)

        proj_s, dt_s = _in_proj(hs, g_mix, w_in, i, cos_s, sin_s)
        qkv_s = proj_s[:, :COL_Z].reshape(dec_batch, dec_seq, COL_Z)
        att_s, nk_s, nv_s = _sample_attn(qkv_s, ck, cv, i, nk_s, nv_s)
        y_s, st_s, cv_s = _ssd_sample(proj_s, dt_s, state_conv, st_in, i, conv_w, conv_b3,
                                      dtb, alog, dsk_e, ng3, r64, gsum, st_s, cv_s,
                                      bt=16, t_new=dec_seq)
        hs = _out_proj(att_s.reshape(ts, D_ATT).astype(BF16), y_s, hs, w_out, i)
        hs = _ffn(hs, g_ffn, w_ffn_gate, w_ffn_up, w_ffn_down, i)
        hs = _ple(hs, ps, g_ple, w_ple_gate, w_ple_proj, gf, i, final=last)

    return (hp.reshape(batch, seq, d), hs.reshape(dec_batch, dec_seq, d),
            jnp.stack(k_p), jnp.stack(v_p), jnp.stack(st_p), jnp.stack(cv_p),
            nk_s.reshape(cache_k.shape), nv_s.reshape(cache_v.shape),
            st_s.reshape(state_ssm.shape), cv_s)
```

```python
import functools

import numpy as np
import jax
import jax.numpy as jnp
from jax import lax
from jax.experimental import pallas as pl
from jax.experimental.pallas import tpu as pltpu

F32 = jnp.float32
BF16 = jnp.bfloat16

N_Q_HEADS = 8
N_KV_HEADS = 4
Q_PER_KV = N_Q_HEADS // N_KV_HEADS
HEAD_DIM = 128
D_ATT = N_Q_HEADS * HEAD_DIM
D_KV = N_KV_HEADS * HEAD_DIM
DILATED_BRANCHES = ((128, 1), (512, 4), (2048, 16))
ATT_WINDOW = 2048
ROPE_THETA = 10000.0
ATT_SCALE = HEAD_DIM ** -0.5
N_SSM_HEADS = 16
SSM_HEAD_DIM = 64
D_SSM = N_SSM_HEADS * SSM_HEAD_DIM
N_SSM_GROUPS = 4
HEADS_PER_GROUP = N_SSM_HEADS // N_SSM_GROUPS
D_STATE = 128
D_BC = N_SSM_GROUPS * D_STATE
CONV_WIDTH = 4
CONV_DIM = D_SSM + 2 * D_BC
SSD_CHUNK = 128
PAST_LEN = 2048
EPS = 1e-6
NEG_INF = -1e30

LANES = 128
SUBLANES = 8
VMEM_LIMIT_BYTES = 56 * 1024 * 1024
ATTN_VMEM_LIMIT_BYTES = 60 * 1024 * 1024

COL_Q = 0
COL_K = D_ATT
COL_V = D_ATT + D_KV
COL_Z = D_ATT + 2 * D_KV
COL_X = COL_Z + D_SSM
COL_BC = COL_X + D_SSM
PROJ_COLS = COL_BC + 2 * D_BC

NT_DIMS = (((1,), (1,)), ((), ()))
TN_DIMS = (((0,), (0,)), ((), ()))


def _dot(a, b):
    return jnp.dot(a, b, preferred_element_type=F32)


def _dot_nt(a, b):
    return lax.dot_general(a, b, NT_DIMS, preferred_element_type=F32)


def _rms(x, g):
    return x * lax.rsqrt(jnp.mean(x * x, axis=-1, keepdims=True) + EPS) * g


def _silu(x):
    return x * jax.nn.sigmoid(x)


def _softplus(x):
    return jnp.maximum(x, 0.0) + jnp.log1p(jnp.exp(-jnp.abs(x)))


def _expand(x, r):
    hi = x.astype(BF16)
    r1 = x - hi.astype(F32)
    mid = r1.astype(BF16)
    lo = (r1 - mid.astype(F32)).astype(BF16)
    return _dot(hi, r) + _dot(mid, r) + _dot(lo, r)


def _imod(x, n):
    assert n & (n - 1) == 0, "power-of-two divisor expected"
    return x & (n - 1)


def _idiv(x, n):
    assert n & (n - 1) == 0, "power-of-two divisor expected"
    return x >> (n.bit_length() - 1)


def _params(sem):
    return pltpu.CompilerParams(dimension_semantics=sem, vmem_limit_bytes=VMEM_LIMIT_BYTES)


def _row_tile(rows, cap):
    tile = min(rows, cap)
    while rows % tile:
        tile -= SUBLANES
    return tile


def _in_proj_kernel(x_ref, g_ref, w_ref, wdt_ref, cos_ref, sin_ref, o_ref, dt_ref, hn_ref, *,
                    rot_tiles):
    n = pl.program_id(1)

    @pl.when(n == 0)
    def _():
        hn = _rms(x_ref[...], g_ref[...]).astype(BF16)
        hn_ref[...] = hn
        lane = lax.broadcasted_iota(jnp.int32, wdt_ref.shape, 1)
        wdt = jnp.where(lane < N_SSM_HEADS, wdt_ref[...], 0.0).astype(BF16)
        dt_ref[...] = _dot(hn, wdt)

    acc = _dot(hn_ref[...], w_ref[...].astype(BF16))

    @pl.when(n < rot_tiles)
    def _():
        cos = cos_ref[...]
        sin = sin_ref[...]
        for hh in range(acc.shape[1] // HEAD_DIM):
            xh = acc[:, hh * HEAD_DIM:(hh + 1) * HEAD_DIM]
            o_ref[:, hh * HEAD_DIM:(hh + 1) * HEAD_DIM] = (
                xh * cos + pltpu.roll(xh, HEAD_DIM // 2, axis=1) * sin)

    @pl.when(n >= rot_tiles)
    def _():
        o_ref[...] = acc


def _in_proj(h, g, w_in, layer, cos2, sin2):
    t, d = h.shape
    tn = 512
    tm = _row_tile(cos2.shape[0], 1024)
    assert t % tm == 0 and PROJ_COLS % LANES == 0
    period_tiles = cos2.shape[0] // tm
    dt_block = PROJ_COLS // LANES
    kern = functools.partial(_in_proj_kernel, rot_tiles=(D_ATT + D_KV) // tn)
    return pl.pallas_call(
        kern,
        out_shape=(jax.ShapeDtypeStruct((t, PROJ_COLS), F32),
                   jax.ShapeDtypeStruct((t, LANES), F32)),
        grid=(t // tm, PROJ_COLS // tn),
        in_specs=[
            pl.BlockSpec((tm, d), lambda m, n: (m, 0)),
            pl.BlockSpec((1, d), lambda m, n: (0, 0)),
            pl.BlockSpec((None, d, tn), lambda m, n: (layer, 0, n)),
            pl.BlockSpec((None, d, LANES), lambda m, n: (layer, 0, dt_block)),
            pl.BlockSpec((tm, HEAD_DIM), lambda m, n: (m % period_tiles, 0)),
            pl.BlockSpec((tm, HEAD_DIM), lambda m, n: (m % period_tiles, 0)),
        ],
        out_specs=(pl.BlockSpec((tm, tn), lambda m, n: (m, n)),
                   pl.BlockSpec((tm, LANES), lambda m, n: (m, 0))),
        scratch_shapes=[pltpu.VMEM((tm, d), BF16)],
        compiler_params=_params(("parallel", "arbitrary")),
        name="in_proj",
    )(h, g, w_in, w_in, cos2, sin2)


ATT_BLOCK = 128
ATT_SPAN = max(w for w, _ in DILATED_BRANCHES)


def _prompt_attn_step(q_ref, k_ref, v_ref, o_ref, scr, w, *, seq, steps_per_unit):
    blk = ATT_BLOCK
    span = ATT_SPAN
    nbr = len(DILATED_BRANCHES)
    ob = [scr[2 * bi] for bi in range(nbr)]
    lb = [scr[2 * bi + 1] for bi in range(nbr)]
    n_spans = seq // span
    steps_per_span = steps_per_unit // n_spans
    assert seq % span == 0 and steps_per_unit % n_spans == 0
    s = _idiv(w, steps_per_span)
    ph = _imod(w, steps_per_span)
    s0 = pl.multiple_of(s * span, span)
    ii = lax.broadcasted_iota(jnp.int32, (blk, 2 * blk), 0)
    jj = lax.broadcasted_iota(jnp.int32, (blk, 2 * blk), 1)
    dist = ii + blk - jj

    def rows(start, dil):
        return pl.ds(start, blk, stride=dil) if dil > 1 else pl.ds(start, blk)

    def piece(ref, start, dil):
        return ref[rows(start, dil), :].astype(BF16)

    def attend(bi, dil, base, local, kp, kc, vp, vc, first):
        kb = jnp.concatenate([kp, kc], axis=0)
        vb = jnp.concatenate([vp, vc], axis=0)
        hi = blk if first is None else jnp.where(first, ii, blk)
        mask = (dist >= 0) & (dist <= hi)
        q = piece(q_ref, base, dil)
        sc = jnp.where(mask, _dot_nt(q, kb) * ATT_SCALE, NEG_INF)
        m = jnp.max(sc, axis=1, keepdims=True)
        e = jnp.exp(sc - m)
        ssum = jnp.sum(e, axis=1, keepdims=True)
        p = (e / ssum).astype(BF16)
        ob[bi][rows(local, dil), :] = _dot(p, vb)
        lb[bi][rows(local, dil), :] = jnp.broadcast_to(m + jnp.log(ssum), (blk, HEAD_DIM))

    for bi, (win, dil) in enumerate(DILATED_BRANCHES):
        assert win // dil == blk and span % (dil * blk) == 0
        stride_rows = blk * dil
        per_class = span // stride_rows
        if per_class >= 2:
            half = per_class // 2
            n_iter = dil * half
        else:
            n_iter = dil // 2
        assert n_iter % steps_per_span == 0
        per_step = n_iter // steps_per_span
        for j in range(per_step):
            idx = ph * per_step + j
            if per_class >= 2:
                r = _idiv(idx, half) if dil > 1 else 0
                i = idx - r * half
                local0 = r + (2 * i) * stride_rows
                base0 = s0 + local0
                first = (s == 0) & (i == 0)
                prev = jnp.maximum(base0 - stride_rows, r)
                km, vm = piece(k_ref, prev, dil), piece(v_ref, prev, dil)
                k0, v0 = piece(k_ref, base0, dil), piece(v_ref, base0, dil)
                k1 = piece(k_ref, base0 + stride_rows, dil)
                v1 = piece(v_ref, base0 + stride_rows, dil)
                attend(bi, dil, base0, local0, km, k0, vm, v0, first)
                attend(bi, dil, base0 + stride_rows, local0 + stride_rows, k0, k1, v0, v1, None)
            else:
                for e in range(2):
                    r = 2 * idx + e
                    base = s0 + r
                    prev = jnp.maximum(base - stride_rows, r)
                    attend(bi, dil, base, r, piece(k_ref, prev, dil), piece(k_ref, base, dil),
                           piece(v_ref, prev, dil), piece(v_ref, base, dil), s == 0)

    @pl.when(ph == steps_per_span - 1)
    def _():
        rows_per = 256

        def merge(c, carry):
            loc = pl.multiple_of(c * rows_per, rows_per)
            sl = pl.ds(loc, rows_per)
            ls = [lb[bi][sl, :] for bi in range(nbr)]
            mx = functools.reduce(jnp.maximum, ls)
            ws = [jnp.exp(l - mx) for l in ls]
            num = functools.reduce(lambda a, b: a + b,
                                   [wgt * ob[bi][sl, :] for bi, wgt in enumerate(ws)])
            den = functools.reduce(lambda a, b: a + b, ws)
            o_ref[pl.ds(s0 + loc, rows_per), :] = (num / den).astype(o_ref.dtype)
            return carry

        lax.fori_loop(0, span // rows_per, merge, 0)


def _sample_attn_step(qkv_ref, ck_ref, cv_ref, att_ref, nk_ref, nv_ref, *, n_past, t_new):
    qkv = qkv_ref[...]
    k_new = qkv[:, COL_K:COL_K + D_KV]
    v_new = qkv[:, COL_V:COL_V + D_KV]

    nrow = n_past * N_KV_HEADS
    shift = t_new * N_KV_HEADS
    assert shift % SUBLANES == 0
    step = 1024
    for src, new, dst in ((ck_ref, k_new, nk_ref), (cv_ref, v_new, nv_ref)):
        for r0 in range(0, nrow - shift, step):
            n = min(step, nrow - shift - r0)
            dst[r0:r0 + n, :] = src[r0 + shift:r0 + shift + n, :]
        for j in range(t_new):
            for g in range(N_KV_HEADS):
                r = nrow - shift + j * N_KV_HEADS + g
                dst[r:r + 1, :] = new[j:j + 1, g * HEAD_DIM:(g + 1) * HEAD_DIM]

    nq = Q_PER_KV * t_new
    cidx = lax.broadcasted_iota(jnp.int32, (nq, n_past), 1)
    tok = _imod(lax.broadcasted_iota(jnp.int32, (nq, n_past), 0), t_new)
    dist_c = n_past + tok - cidx
    tok1 = _imod(lax.broadcasted_iota(jnp.int32, (nq, 1), 0), t_new)

    for g in range(N_KV_HEADS):
        qg = jnp.concatenate(
            [qkv[:, (g * Q_PER_KV + r) * HEAD_DIM:(g * Q_PER_KV + r + 1) * HEAD_DIM]
             for r in range(Q_PER_KV)], axis=0).astype(BF16)
        kc = ck_ref[pl.ds(g, n_past, stride=N_KV_HEADS), :].astype(BF16)
        vc = cv_ref[pl.ds(g, n_past, stride=N_KV_HEADS), :].astype(BF16)
        kn = k_new[:, g * HEAD_DIM:(g + 1) * HEAD_DIM].astype(BF16).astype(F32)
        vn = v_new[:, g * HEAD_DIM:(g + 1) * HEAD_DIM].astype(BF16).astype(F32)
        s_c = _dot_nt(qg, kc) * ATT_SCALE
        qf = qg.astype(F32)
        s_n = [jnp.sum(qf * kn[j:j + 1, :], axis=1, keepdims=True) * ATT_SCALE
               for j in range(t_new)]

        probs, new_terms, lses = [], [], []
        for win, dil in DILATED_BRANCHES:
            assert win <= n_past
            mask_c = (_imod(dist_c, dil) == 0) & (dist_c <= win)
            sc = jnp.where(mask_c, s_c, NEG_INF)
            m = jnp.max(sc, axis=1, keepdims=True)
            sn = []
            for j in range(t_new):
                dn = tok1 - j
                mask_n = (dn >= 0) & (_imod(dn, dil) == 0)
                snj = jnp.where(mask_n, s_n[j], NEG_INF)
                sn.append(snj)
                m = jnp.maximum(m, snj)
            ec = jnp.exp(sc - m)
            en = [jnp.exp(x - m) for x in sn]
            ssum = jnp.sum(ec, axis=1, keepdims=True)
            for x in en:
                ssum = ssum + x
            probs.append((ec / ssum).astype(BF16))
            o_new = jnp.zeros((nq, HEAD_DIM), F32)
            for j in range(t_new):
                o_new = o_new + (en[j] / ssum).astype(BF16).astype(F32) * vn[j:j + 1, :]
            new_terms.append(o_new)
            lses.append(m + jnp.log(ssum))

        o_all = _dot(jnp.concatenate(probs, axis=0), vc)
        mx = functools.reduce(jnp.maximum, lses)
        ws = [jnp.exp(l - mx) for l in lses]
        den = functools.reduce(lambda a, b: a + b, ws)
        o = jnp.zeros((nq, HEAD_DIM), F32)
        for i, w in enumerate(ws):
            o = o + w * (o_all[i * nq:(i + 1) * nq, :] + new_terms[i])
        o = o / den
        for r in range(Q_PER_KV):
            att_ref[:, (g * Q_PER_KV + r) * HEAD_DIM:(g * Q_PER_KV + r + 1) * HEAD_DIM] = (
                o[r * t_new:(r + 1) * t_new, :])


def _attn_kernel(*refs, n_past, t_new, aliased, seq, steps_per_unit):
    if aliased:
        (qkv_ref, ck_ref, cv_ref, q_ref, k_ref, v_ref, _, _,
         att_s_ref, nk_ref, nv_ref, att_p_ref, *scr) = refs
    else:
        (qkv_ref, ck_ref, cv_ref, q_ref, k_ref, v_ref,
         att_s_ref, nk_ref, nv_ref, att_p_ref, *scr) = refs
    _sample_attn_step(qkv_ref, ck_ref, cv_ref, att_s_ref, nk_ref, nv_ref, n_past=n_past, t_new=t_new)
    w = _imod(pl.program_id(0), steps_per_unit)
    _prompt_attn_step(q_ref, k_ref, v_ref, att_p_ref, scr, w, seq=seq, steps_per_unit=steps_per_unit)


def _attn(qkv_s, cache_k, cache_v, proj, layer, prev_k, prev_v, *, batch, seq):
    depth, bsz, nrow, _ = cache_k.shape
    n_past = nrow // N_KV_HEADS
    t_new = qkv_s.shape[1]
    units = batch * N_Q_HEADS
    steps_per_unit = bsz // units
    assert bsz == units * steps_per_unit
    aliased = prev_k is not None
    kern = functools.partial(_attn_kernel, n_past=n_past, t_new=t_new, aliased=aliased, seq=seq,
                             steps_per_unit=steps_per_unit)
    kq = COL_K // HEAD_DIM
    vq = COL_V // HEAD_DIM

    def unit(i):
        u = i // steps_per_unit
        return u // N_Q_HEADS, u % N_Q_HEADS

    win_spec = pl.BlockSpec((None, None, nrow, HEAD_DIM), lambda i: (layer, i, 0, 0))
    in_specs = [
        pl.BlockSpec((None, t_new, qkv_s.shape[2]), lambda i: (i, 0, 0)), win_spec, win_spec,
        pl.BlockSpec((seq, HEAD_DIM), lambda i: unit(i)),
        pl.BlockSpec((seq, HEAD_DIM), lambda i: (unit(i)[0], kq + unit(i)[1] // Q_PER_KV)),
        pl.BlockSpec((seq, HEAD_DIM), lambda i: (unit(i)[0], vq + unit(i)[1] // Q_PER_KV)),
    ]
    args = [qkv_s, cache_k, cache_v, proj, proj, proj]
    aliases = {}
    if aliased:
        in_specs += [pl.BlockSpec(memory_space=pl.ANY), pl.BlockSpec(memory_space=pl.ANY)]
        args += [prev_k, prev_v]
        aliases = {6: 1, 7: 2}
    n_scr = len(DILATED_BRANCHES) * 2
    return pl.pallas_call(
        kern,
        out_shape=(jax.ShapeDtypeStruct((bsz, t_new, D_ATT), F32),
                   jax.ShapeDtypeStruct(cache_k.shape, cache_k.dtype),
                   jax.ShapeDtypeStruct(cache_v.shape, cache_v.dtype),
                   jax.ShapeDtypeStruct((batch * seq, D_ATT), BF16)),
        grid=(bsz,),
        in_specs=in_specs,
        out_specs=(pl.BlockSpec((None, t_new, D_ATT), lambda i: (i, 0, 0)), win_spec, win_spec,
                   pl.BlockSpec((seq, HEAD_DIM), lambda i: unit(i))),
        scratch_shapes=[pltpu.VMEM((ATT_SPAN, HEAD_DIM), F32) for _ in range(n_scr)],
        input_output_aliases=aliases,
        compiler_params=pltpu.CompilerParams(dimension_semantics=("arbitrary",),
                                             vmem_limit_bytes=ATTN_VMEM_LIMIT_BYTES),
        name="attn",
    )(*args)


def _expand_mats():
    r64 = np.zeros((LANES, D_SSM), np.float32)
    r128 = np.zeros((LANES, N_SSM_HEADS * LANES), np.float32)
    for h in range(N_SSM_HEADS):
        r64[h, h * SSM_HEAD_DIM:(h + 1) * SSM_HEAD_DIM] = 1.0
        r128[h, h * LANES:(h + 1) * LANES] = 1.0
    return jnp.asarray(r64, BF16), jnp.asarray(r128, BF16)


def _group_sum_mat():
    g = np.zeros((D_BC, D_SSM), np.float32)
    for grp in range(N_SSM_GROUPS):
        g[grp * D_STATE:(grp + 1) * D_STATE,
          grp * HEADS_PER_GROUP * SSM_HEAD_DIM:(grp + 1) * HEADS_PER_GROUP * SSM_HEAD_DIM] = 1.0
    return jnp.asarray(g, BF16)


def _conv_silu(x, tail, cw, cb):
    n = x.shape[0]
    xp = jnp.concatenate([tail, x], axis=0)
    out = cb + cw[3:4, :] * x
    for w in range(CONV_WIDTH - 1):
        off = SUBLANES - (CONV_WIDTH - 1) + w
        out = out + cw[w:w + 1, :] * xp[off:off + n, :]
    return _silu(out)


def _cumsum_rows(x, seg=None):
    n = x.shape[0]
    rows = lax.broadcasted_iota(jnp.int32, x.shape, 0)
    pos = rows if seg is None else _imod(rows, seg)
    limit = n if seg is None else seg
    sh = 1
    while sh < limit:
        x = x + jnp.where(pos >= sh, pltpu.roll(x, sh, axis=0), 0.0)
        sh *= 2
    return x


def _lane_col_block(row):
    return jnp.broadcast_to(row, (LANES, LANES)).T


def _ssd_prompt_kernel(z_ref, xs_ref, bc_ref, dt_ref, cw_ref, cb_ref, dtb_ref, alog_ref, dsk_ref,
                       ng_ref, r64_ref, r128_ref, y_ref, st_ref, cv_ref, h_scr, tail_scr):
    c = pl.program_id(1)
    last = pl.num_programs(1) - 1
    cs = SSD_CHUNK

    @pl.when(c == 0)
    def _():
        h_scr[...] = jnp.zeros_like(h_scr)
        tail_scr[...] = jnp.zeros_like(tail_scr)

    xraw = xs_ref[...]
    bcraw = bc_ref[...]
    tail = tail_scr[...]
    cw = cw_ref[...]
    cb = cb_ref[...]
    xs = _conv_silu(xraw, tail[:, :D_SSM], cw[:, :D_SSM], cb[:, :D_SSM])
    bcm = _conv_silu(bcraw, tail[:, D_SSM:], cw[:, D_SSM:], cb[:, D_SSM:])
    tail_scr[:, :D_SSM] = xraw[cs - SUBLANES:, :]
    tail_scr[:, D_SSM:] = bcraw[cs - SUBLANES:, :]

    @pl.when(c == last)
    def _():
        cv_ref[:, :D_SSM] = xraw[cs - (CONV_WIDTH - 1):, :]
        cv_ref[:, D_SSM:] = bcraw[cs - (CONV_WIDTH - 1):, :]

    dt = _softplus(dt_ref[...] + dtb_ref[...])
    a = -jnp.exp(alog_ref[...])
    acum = _cumsum_rows(dt * a)
    acum_t = acum.T
    r64 = r64_ref[...]
    dt_e = _expand(dt, r64)
    ac_e = _expand(acum, r64)
    col_b = _expand(acum, r128_ref[...])
    last_e = ac_e[cs - 1:cs, :]
    ea_e = jnp.exp(ac_e)
    xdt = xs * dt_e
    xte = (xdt * jnp.exp(last_e - ac_e)).astype(BF16)

    ti = lax.broadcasted_iota(jnp.int32, (cs, cs), 0)
    si = lax.broadcasted_iota(jnp.int32, (cs, cs), 1)
    causal = ti >= si
    lane = lax.broadcasted_iota(jnp.int32, (cs, LANES), 1)
    lo_half = lane < SSM_HEAD_DIM

    gw = HEADS_PER_GROUP * SSM_HEAD_DIM
    y_diag, y_off, states = [], [], []
    for g in range(N_SSM_GROUPS):
        bg = bcm[:, g * D_STATE:(g + 1) * D_STATE].astype(BF16)
        cg = bcm[:, D_BC + g * D_STATE:D_BC + (g + 1) * D_STATE].astype(BF16)
        cbt = _dot_nt(cg, bg)
        h_in = h_scr[g * gw:(g + 1) * gw, :]
        y_off.append(_dot_nt(cg, h_in.astype(BF16)))
        states.append(lax.dot_general(xte[:, g * gw:(g + 1) * gw], bg, TN_DIMS,
                                      preferred_element_type=F32))
        for k in range(HEADS_PER_GROUP // 2):
            pair = g * (HEADS_PER_GROUP // 2) + k
            xp = xdt[:, pair * LANES:(pair + 1) * LANES]
            yd = jnp.zeros((cs, LANES), F32)
            for e in range(2):
                h = 2 * pair + e
                seg = col_b[:, h * LANES:(h + 1) * LANES] - jnp.broadcast_to(acum_t[h:h + 1, :], (cs, cs))
                dec = jnp.exp(jnp.where(causal, seg, NEG_INF))
                cbh = (cbt * dec).astype(BF16)
                xh = jnp.where(lo_half if e == 0 else jnp.logical_not(lo_half), xp, 0.0).astype(BF16)
                yd = yd + _dot(cbh, xh)
            y_diag.append(yd)

    y = (jnp.concatenate(y_diag, axis=1) + jnp.concatenate(y_off, axis=1) * ea_e
         + dsk_ref[...] * xs)
    dec_rows = jnp.exp(jnp.concatenate(
        [_lane_col_block(last_e[:, k * LANES:(k + 1) * LANES]) for k in range(D_SSM // LANES)],
        axis=0))
    h_new = h_scr[...] * dec_rows + jnp.concatenate(states, axis=0)
    h_scr[...] = h_new

    @pl.when(c == last)
    def _():
        st_ref[...] = h_new

    y_ref[...] = _rms(y * _silu(z_ref[...]), ng_ref[...]).astype(y_ref.dtype)


def _ssd_prompt(proj, dt_all, layer, conv_w, conv_b3, dtb, alog, dsk_e, ng3, r64, r128, batch, seq):
    cs = SSD_CHUNK
    nc = seq // cs
    zc, xc, bcc = COL_Z // D_SSM, COL_X // D_SSM, COL_BC // D_SSM
    const2 = lambda b, c: (0, 0)
    return pl.pallas_call(
        _ssd_prompt_kernel,
        out_shape=(jax.ShapeDtypeStruct((batch * seq, D_SSM), BF16),
                   jax.ShapeDtypeStruct((batch, D_SSM, D_STATE), F32),
                   jax.ShapeDtypeStruct((batch, CONV_WIDTH - 1, CONV_DIM), F32)),
        grid=(batch, nc),
        in_specs=[
            pl.BlockSpec((cs, D_SSM), lambda b, c: (b * nc + c, zc)),
            pl.BlockSpec((cs, D_SSM), lambda b, c: (b * nc + c, xc)),
            pl.BlockSpec((cs, D_SSM), lambda b, c: (b * nc + c, bcc)),
            pl.BlockSpec((cs, LANES), lambda b, c: (b * nc + c, 0)),
            pl.BlockSpec((None, CONV_WIDTH, CONV_DIM), lambda b, c: (layer, 0, 0)),
            pl.BlockSpec((None, 1, CONV_DIM), lambda b, c: (layer, 0, 0)),
            pl.BlockSpec((None, 1, LANES), lambda b, c: (layer, 0, 0)),
            pl.BlockSpec((None, 1, LANES), lambda b, c: (layer, 0, 0)),
            pl.BlockSpec((None, 1, D_SSM), lambda b, c: (layer, 0, 0)),
            pl.BlockSpec((None, 1, D_SSM), lambda b, c: (layer, 0, 0)),
            pl.BlockSpec(r64.shape, const2),
            pl.BlockSpec(r128.shape, const2),
        ],
        out_specs=(pl.BlockSpec((cs, D_SSM), lambda b, c: (b * nc + c, 0)),
                   pl.BlockSpec((None, D_SSM, D_STATE), lambda b, c: (b, 0, 0)),
                   pl.BlockSpec((None, CONV_WIDTH - 1, CONV_DIM), lambda b, c: (b, 0, 0))),
        scratch_shapes=[pltpu.VMEM((D_SSM, D_STATE), F32), pltpu.VMEM((SUBLANES, CONV_DIM), F32)],
        compiler_params=_params(("parallel", "arbitrary")),
        name="ssd_prompt",
    )(proj, proj, proj, dt_all, conv_w, conv_b3, dtb, alog, dsk_e, ng3, r64, r128)


def _ssd_sample_kernel(*refs, bt, t_new, aliased):
    if aliased:
        (z_ref, xs_ref, bc_ref, dt_ref, cst_ref, h0_ref, cw_ref, cb_ref, dtb_ref, alog_ref, dsk_ref,
         ng_ref, r64_ref, gs_ref, _, _, y_ref, st_ref, cv_ref, xbc_scr) = refs
    else:
        (z_ref, xs_ref, bc_ref, dt_ref, cst_ref, h0_ref, cw_ref, cb_ref, dtb_ref, alog_ref, dsk_ref,
         ng_ref, r64_ref, gs_ref, y_ref, st_ref, cv_ref, xbc_scr) = refs
    rows = bt * t_new
    kw = CONV_WIDTH - 1
    cw = cw_ref[...]
    cb = cb_ref[...]

    for b in range(bt):
        xb = jnp.concatenate([xs_ref[b * t_new:(b + 1) * t_new, :],
                              bc_ref[b * t_new:(b + 1) * t_new, :]], axis=1)
        xp = jnp.concatenate([cst_ref[b], xb], axis=0)
        out = cb
        for w in range(CONV_WIDTH):
            out = out + cw[w:w + 1, :] * xp[w:w + t_new, :]
        xbc_scr[b * t_new:(b + 1) * t_new, :] = _silu(out)
        cv_ref[b] = xp[t_new:t_new + kw, :]

    xbc = xbc_scr[...]
    xs = xbc[:, :D_SSM]
    bm = xbc[:, D_SSM:D_SSM + D_BC]
    cm = xbc[:, D_SSM + D_BC:]

    tpos = _imod(lax.broadcasted_iota(jnp.int32, (rows, 1), 0), t_new)
    dt = _softplus(dt_ref[...] + dtb_ref[...])
    a = -jnp.exp(alog_ref[...])
    acum = _cumsum_rows(dt * a, seg=t_new)
    r64 = r64_ref[...]
    dt_e = _expand(dt, r64)
    ac_e = _expand(acum, r64)
    v = jnp.where(tpos == t_new - 1, ac_e, 0.0)
    last_e = v
    for d in range(1, t_new):
        last_e = last_e + pltpu.roll(v, rows - d, axis=0)
    ea_e = jnp.exp(ac_e)
    xdt = xs * dt_e
    xte = xdt * jnp.exp(last_e - ac_e)

    cmb = cm.astype(BF16).astype(F32)
    bmb = bm.astype(BF16).astype(F32)
    gs = gs_ref[...]
    y = dsk_ref[...] * xs
    for d in range(t_new):
        if d == 0:
            b_s, x_s, a_s = bmb, xdt, ac_e
        else:
            b_s = pltpu.roll(bmb, d, axis=0)
            x_s = pltpu.roll(xdt, d, axis=0)
            a_s = pltpu.roll(ac_e, d, axis=0)
        cb_e = _expand(cmb * b_s, gs)
        dec = jnp.exp(jnp.where(tpos >= d, ac_e - a_s, NEG_INF))
        y = y + cb_e * dec * x_s

    gw = HEADS_PER_GROUP * SSM_HEAD_DIM
    pad = jnp.zeros((LANES - rows, LANES), F32) if rows < LANES else None
    rowb = _idiv(lax.broadcasted_iota(jnp.int32, (rows, 1), 0), t_new)
    colb = _idiv(lax.broadcasted_iota(jnp.int32, (1, LANES), 1), t_new)
    xte_t = []
    for k in range(D_SSM // LANES):
        blk = xte[:, k * LANES:(k + 1) * LANES]
        if pad is not None:
            blk = jnp.concatenate([blk, pad], axis=0)
        xte_t.append(blk.T)
    e_last = jnp.exp(last_e)
    y_off = [jnp.zeros((rows, gw), F32) for _ in range(N_SSM_GROUPS)]
    for b in range(bt):
        h0 = h0_ref[b]
        new_rows = []
        for g in range(N_SSM_GROUPS):
            cg = cm[:, g * D_STATE:(g + 1) * D_STATE]
            bg = bm[:, g * D_STATE:(g + 1) * D_STATE]
            if pad is not None:
                bg = jnp.concatenate([bg, pad], axis=0)
            bg = bg.astype(BF16)
            h0g = h0[g * gw:(g + 1) * gw, :]
            cgb = jnp.where(rowb == b, cg, 0.0).astype(BF16)
            y_off[g] = y_off[g] + _dot_nt(cgb, h0g.astype(BF16))
            for k in range(gw // LANES):
                blk = g * (gw // LANES) + k
                lhs = jnp.where(colb == b, xte_t[blk], 0.0).astype(BF16)
                st = _dot(lhs, bg)
                r = b * t_new + t_new - 1
                dec = _lane_col_block(e_last[r:r + 1, blk * LANES:(blk + 1) * LANES])
                new_rows.append(h0[blk * LANES:(blk + 1) * LANES, :] * dec + st)
        st_ref[b] = jnp.concatenate(new_rows, axis=0)

    y = y + jnp.concatenate(y_off, axis=1) * ea_e
    y_ref[...] = _rms(y * _silu(z_ref[...]), ng_ref[...]).astype(y_ref.dtype)


def _ssd_sample(proj_s, dt_s, state_conv, state_ssm, layer, conv_w, conv_b3, dtb, alog, dsk_e, ng3,
                r64, gsum, prev_st, prev_cv, *, bt, t_new):
    depth, bsz = state_ssm.shape[:2]
    rows = bt * t_new
    zc, xc, bcc = COL_Z // D_SSM, COL_X // D_SSM, COL_BC // D_SSM
    aliased = prev_st is not None
    kern = functools.partial(_ssd_sample_kernel, bt=bt, t_new=t_new, aliased=aliased)
    const2 = lambda i: (0, 0)
    in_specs = [
        pl.BlockSpec((rows, D_SSM), lambda i: (i, zc)),
        pl.BlockSpec((rows, D_SSM), lambda i: (i, xc)),
        pl.BlockSpec((rows, D_SSM), lambda i: (i, bcc)),
        pl.BlockSpec((rows, LANES), lambda i: (i, 0)),
        pl.BlockSpec((None, bt, CONV_WIDTH - 1, CONV_DIM), lambda i: (layer, i, 0, 0)),
        pl.BlockSpec((None, bt, D_SSM, D_STATE), lambda i: (layer, i, 0, 0)),
        pl.BlockSpec((None, CONV_WIDTH, CONV_DIM), lambda i: (layer, 0, 0)),
        pl.BlockSpec((None, 1, CONV_DIM), lambda i: (layer, 0, 0)),
        pl.BlockSpec((None, 1, LANES), lambda i: (layer, 0, 0)),
        pl.BlockSpec((None, 1, LANES), lambda i: (layer, 0, 0)),
        pl.BlockSpec((None, 1, D_SSM), lambda i: (layer, 0, 0)),
        pl.BlockSpec((None, 1, D_SSM), lambda i: (layer, 0, 0)),
        pl.BlockSpec(r64.shape, const2),
        pl.BlockSpec(gsum.shape, const2),
    ]
    args = [proj_s, proj_s, proj_s, dt_s, state_conv, state_ssm, conv_w, conv_b3, dtb, alog, dsk_e,
            ng3, r64, gsum]
    aliases = {}
    if aliased:
        in_specs += [pl.BlockSpec(memory_space=pl.ANY), pl.BlockSpec(memory_space=pl.ANY)]
        args += [prev_st, prev_cv]
        aliases = {14: 1, 15: 2}
    return pl.pallas_call(
        kern,
        out_shape=(jax.ShapeDtypeStruct((bsz * t_new, D_SSM), BF16),
                   jax.ShapeDtypeStruct(state_ssm.shape, state_ssm.dtype),
                   jax.ShapeDtypeStruct(state_conv.shape, state_conv.dtype)),
        grid=(bsz // bt,),
        in_specs=in_specs,
        out_specs=(pl.BlockSpec((rows, D_SSM), lambda i: (i, 0)),
                   pl.BlockSpec((None, bt, D_SSM, D_STATE), lambda i: (layer, i, 0, 0)),
                   pl.BlockSpec((None, bt, CONV_WIDTH - 1, CONV_DIM), lambda i: (layer, i, 0, 0))),
        scratch_shapes=[pltpu.VMEM((rows, CONV_DIM), F32)],
        input_output_aliases=aliases,
        compiler_params=_params(("parallel",)),
        name="ssd_sample",
    )(*args)


def _resident(shape, index_map):
    return pl.BlockSpec(shape, index_map, pipeline_mode=pl.Buffered(1))


def _out_proj_kernel(att_ref, ssm_ref, h_ref, w_ref, o_ref, wb_ref):
    @pl.when(pl.program_id(0) == 0)
    def _():
        wb_ref[...] = w_ref[...].astype(BF16)

    acc = _dot(att_ref[...], wb_ref[:D_ATT, :]) + _dot(ssm_ref[...], wb_ref[D_ATT:, :])
    o_ref[...] = h_ref[...] + acc


def _out_proj(att, ssm, h, w_out, layer):
    t, d = h.shape
    tm = _row_tile(t, 512)
    return pl.pallas_call(
        _out_proj_kernel,
        out_shape=jax.ShapeDtypeStruct((t, d), F32),
        grid=(t // tm,),
        in_specs=[
            pl.BlockSpec((tm, D_ATT), lambda m: (m, 0)),
            pl.BlockSpec((tm, D_SSM), lambda m: (m, 0)),
            pl.BlockSpec((tm, d), lambda m: (m, 0)),
            _resident((None,) + w_out.shape[1:], lambda m: (layer, 0, 0)),
        ],
        out_specs=pl.BlockSpec((tm, d), lambda m: (m, 0)),
        scratch_shapes=[pltpu.VMEM(w_out.shape[1:], BF16)],
        compiler_params=_params(("arbitrary",)),
        name="out_proj",
    )(att, ssm, h, w_out)


def _ffn_kernel(h_ref, g_ref, wg_ref, wu_ref, wd_ref, o_ref, hf_ref):
    f = pl.program_id(1)

    @pl.when(f == 0)
    def _():
        h = h_ref[...]
        hf_ref[...] = _rms(h, g_ref[...]).astype(BF16)
        o_ref[...] = h

    hf = hf_ref[...]
    act = _silu(_dot(hf, wg_ref[...].astype(BF16))) * _dot(hf, wu_ref[...].astype(BF16))
    o_ref[...] += _dot(act.astype(BF16), wd_ref[...].astype(BF16))


def _ffn(h, g, wg, wu, wd, layer):
    t, d = h.shape
    dff = wg.shape[2]
    tm = _row_tile(t, 1024)
    tf = 256
    assert dff % tf == 0
    return pl.pallas_call(
        _ffn_kernel,
        out_shape=jax.ShapeDtypeStruct((t, d), F32),
        grid=(t // tm, dff // tf),
        in_specs=[
            pl.BlockSpec((tm, d), lambda m, f: (m, 0), pipeline_mode=pl.Buffered(1)),
            pl.BlockSpec((1, d), lambda m, f: (0, 0)),
            pl.BlockSpec((None, d, tf), lambda m, f: (layer, 0, f)),
            pl.BlockSpec((None, d, tf), lambda m, f: (layer, 0, f)),
            pl.BlockSpec((None, tf, d), lambda m, f: (layer, f, 0)),
        ],
        out_specs=pl.BlockSpec((tm, d), lambda m, f: (m, 0)),
        scratch_shapes=[pltpu.VMEM((tm, d), BF16)],
        compiler_params=_params(("parallel", "arbitrary")),
        name="ffn",
    )(h, g, wg, wu, wd)


def _ple_kernel(h_ref, p_ref, g_ref, wg_ref, wp_ref, gf_ref, o_ref, wgb_ref, wpb_ref, *, final, tn):
    @pl.when(pl.program_id(0) == 0)
    def _():
        wgb_ref[...] = wg_ref[...].astype(BF16)
        wpb_ref[...] = wp_ref[...].astype(BF16)

    hn = _rms(h_ref[...], g_ref[...]).astype(BF16)
    pb = p_ref[...].astype(BF16)
    for c in range(h_ref.shape[1] // tn):
        cols = slice(c * tn, (c + 1) * tn)
        gate = jax.nn.sigmoid(_dot(hn, wgb_ref[:, cols]))
        o_ref[:, cols] = h_ref[:, cols] + gate * _dot(pb, wpb_ref[:, cols])
    if final:
        o_ref[...] = _rms(o_ref[...], gf_ref[...])


def _ple(h, p, g, wg, wp, gf, layer, *, final):
    t, d = h.shape
    tm = _row_tile(t, 512)
    kern = functools.partial(_ple_kernel, final=final, tn=512)
    return pl.pallas_call(
        kern,
        out_shape=jax.ShapeDtypeStruct((t, d), F32),
        grid=(t // tm,),
        in_specs=[
            pl.BlockSpec((tm, d), lambda m: (m, 0)),
            pl.BlockSpec((None, tm, p.shape[2]), lambda m: (layer, m, 0)),
            pl.BlockSpec((1, d), lambda m: (0, 0)),
            _resident((None,) + wg.shape[1:], lambda m: (layer, 0, 0)),
            _resident((None,) + wp.shape[1:], lambda m: (layer, 0, 0)),
            pl.BlockSpec((1, d), lambda m: (0, 0)),
        ],
        out_specs=pl.BlockSpec((tm, d), lambda m: (m, 0)),
        scratch_shapes=[pltpu.VMEM(wg.shape[1:], BF16), pltpu.VMEM(wp.shape[1:], BF16)],
        compiler_params=_params(("arbitrary",)),
        name="ple",
    )(h, p, g, wg, wp, gf)


def _rope_tables(pos):
    half = HEAD_DIM // 2
    inv_freq = ROPE_THETA ** (-jnp.arange(half, dtype=F32) / half)
    ang = pos.astype(F32)[:, None] * inv_freq[None, :]
    cos, sin = jnp.cos(ang), jnp.sin(ang)
    return jnp.concatenate([cos, cos], axis=1), jnp.concatenate([-sin, sin], axis=1)


def _pad_lanes(x):
    return jnp.pad(x, [(0, 0)] * (x.ndim - 1) + [(0, LANES - x.shape[-1])])


def kernel(x_prompt, x_sample, cache_k, cache_v, state_ssm, state_conv, p_prompt, p_sample,
           norm_mix_g, w_in, conv_w, conv_b, dt_bias, a_log, d_skip, ssm_norm_g, w_out,
           norm_ffn_g, w_ffn_gate, w_ffn_up, w_ffn_down, norm_ple_g, w_ple_gate, w_ple_proj,
           final_norm_g):
    batch, seq, d = x_prompt.shape
    dec_batch, dec_seq, _ = x_sample.shape
    depth = w_in.shape[0]
    n_past = cache_k.shape[2]
    tp = batch * seq
    ts = dec_batch * dec_seq
    assert w_in.shape[2] == PROJ_COLS + N_SSM_HEADS and n_past == PAST_LEN

    cos_p, sin_p = _rope_tables(jnp.arange(seq, dtype=jnp.int32))
    cos_s, sin_s = _rope_tables(jnp.tile(PAST_LEN + jnp.arange(dec_seq, dtype=jnp.int32), dec_batch))
    r64, r128 = _expand_mats()
    gsum = _group_sum_mat()

    ck = cache_k.reshape(depth, dec_batch, n_past * N_KV_HEADS, HEAD_DIM)
    cv = cache_v.reshape(depth, dec_batch, n_past * N_KV_HEADS, HEAD_DIM)
    st_in = state_ssm.reshape(depth, dec_batch, D_SSM, D_STATE)
    pp = p_prompt.reshape(depth, tp, -1)
    ps = p_sample.reshape(depth, ts, -1)
    conv_b3 = conv_b.reshape(depth, 1, CONV_DIM)
    dtb = _pad_lanes(dt_bias).reshape(depth, 1, LANES)
    alog = _pad_lanes(a_log).reshape(depth, 1, LANES)
    dsk_e = jnp.repeat(d_skip, SSM_HEAD_DIM, axis=1).reshape(depth, 1, D_SSM)
    ng3 = ssm_norm_g.reshape(depth, 1, D_SSM)
    gf = final_norm_g.reshape(1, d)

    hp = x_prompt.reshape(tp, d)
    hs = x_sample.reshape(ts, d)
    nk_s = nv_s = st_s = cv_s = None
    k_p, v_p, st_p, cv_p = [], [], [], []
    for i in range(depth):
        g_mix = norm_mix_g[i].reshape(1, d)
        g_ffn = norm_ffn_g[i].reshape(1, d)
        g_ple = norm_ple_g[i].reshape(1, d)
        last = i == depth - 1

        proj, dt_p = _in_proj(hp, g_mix, w_in, i, cos_p, sin_p)
        proj_s, dt_s = _in_proj(hs, g_mix, w_in, i, cos_s, sin_s)
        qkv_s = proj_s[:, :COL_Z].reshape(dec_batch, dec_seq, COL_Z)
        att_s, nk_s, nv_s, att_p = _attn(qkv_s, ck, cv, proj, i, nk_s, nv_s, batch=batch, seq=seq)

        y_p, st_i, cv_i = _ssd_prompt(proj, dt_p, i, conv_w, conv_b3, dtb, alog, dsk_e, ng3,
                                      r64, r128, batch, seq)
        hp = _out_proj(att_p, y_p, hp, w_out, i)
        hp = _ffn(hp, g_ffn, w_ffn_gate, w_ffn_up, w_ffn_down, i)
        hp = _ple(hp, pp, g_ple, w_ple_gate, w_ple_proj, gf, i, final=last)

        keep = min(ATT_WINDOW, seq)
        kv_p = proj.reshape(batch, seq, PROJ_COLS)[:, seq - keep:, COL_K:COL_Z]
        k_p.append(kv_p[..., :D_KV].reshape(batch, keep, N_KV_HEADS, HEAD_DIM))
        v_p.append(kv_p[..., D_KV:].reshape(batch, keep, N_KV_HEADS, HEAD_DIM))
        st_p.append(st_i.reshape(batch, N_SSM_HEADS, SSM_HEAD_DIM, D_STATE))
        cv_p.append(cv_i)

        y_s, st_s, cv_s = _ssd_sample(proj_s, dt_s, state_conv, st_in, i, conv_w, conv_b3,
                                      dtb, alog, dsk_e, ng3, r64, gsum, st_s, cv_s,
                                      bt=16, t_new=dec_seq)
        hs = _out_proj(att_s.reshape(ts, D_ATT).astype(BF16), y_s, hs, w_out, i)
        hs = _ffn(hs, g_ffn, w_ffn_gate, w_ffn_up, w_ffn_down, i)
        hs = _ple(hs, ps, g_ple, w_ple_gate, w_ple_proj, gf, i, final=last)

    return (hp.reshape(batch, seq, d), hs.reshape(dec_batch, dec_seq, d),
            jnp.stack(k_p), jnp.stack(v_p), jnp.stack(st_p), jnp.stack(cv_p),
            nk_s.reshape(cache_k.shape), nv_s.reshape(cache_v.shape),
            st_s.reshape(state_ssm.shape), cv_s)
```

```python
import functools

import numpy as np
import jax
import jax.numpy as jnp
from jax import lax
from jax.experimental import pallas as pl
from jax.experimental.pallas import tpu as pltpu

F32 = jnp.float32
BF16 = jnp.bfloat16

N_Q_HEADS = 8
N_KV_HEADS = 4
Q_PER_KV = N_Q_HEADS // N_KV_HEADS
HEAD_DIM = 128
D_ATT = N_Q_HEADS * HEAD_DIM
D_KV = N_KV_HEADS * HEAD_DIM
DILATED_BRANCHES = ((128, 1), (512, 4), (2048, 16))
ATT_WINDOW = 2048
ROPE_THETA = 10000.0
ATT_SCALE = HEAD_DIM ** -0.5
N_SSM_HEADS = 16
SSM_HEAD_DIM = 64
D_SSM = N_SSM_HEADS * SSM_HEAD_DIM
N_SSM_GROUPS = 4
HEADS_PER_GROUP = N_SSM_HEADS // N_SSM_GROUPS
D_STATE = 128
D_BC = N_SSM_GROUPS * D_STATE
CONV_WIDTH = 4
CONV_DIM = D_SSM + 2 * D_BC
SSD_CHUNK = 128
PAST_LEN = 2048
EPS = 1e-6
NEG_INF = -1e30

LANES = 128
SUBLANES = 8
VMEM_LIMIT_BYTES = 56 * 1024 * 1024
ATTN_VMEM_LIMIT_BYTES = 60 * 1024 * 1024

COL_Q = 0
COL_K = D_ATT
COL_V = D_ATT + D_KV
COL_Z = D_ATT + 2 * D_KV
COL_X = COL_Z + D_SSM
COL_BC = COL_X + D_SSM
PROJ_COLS = COL_BC + 2 * D_BC

NT_DIMS = (((1,), (1,)), ((), ()))
TN_DIMS = (((0,), (0,)), ((), ()))


def _dot(a, b):
    return jnp.dot(a, b, preferred_element_type=F32)


def _dot_nt(a, b):
    return lax.dot_general(a, b, NT_DIMS, preferred_element_type=F32)


def _rms(x, g):
    return x * lax.rsqrt(jnp.mean(x * x, axis=-1, keepdims=True) + EPS) * g


def _silu(x):
    return x * jax.nn.sigmoid(x)


def _softplus(x):
    return jnp.maximum(x, 0.0) + jnp.log1p(jnp.exp(-jnp.abs(x)))


def _expand(x, r):
    hi = x.astype(BF16)
    r1 = x - hi.astype(F32)
    mid = r1.astype(BF16)
    lo = (r1 - mid.astype(F32)).astype(BF16)
    return _dot(hi, r) + _dot(mid, r) + _dot(lo, r)


def _imod(x, n):
    assert n & (n - 1) == 0, "power-of-two divisor expected"
    return x & (n - 1)


def _idiv(x, n):
    assert n & (n - 1) == 0, "power-of-two divisor expected"
    return x >> (n.bit_length() - 1)


def _params(sem):
    return pltpu.CompilerParams(dimension_semantics=sem, vmem_limit_bytes=VMEM_LIMIT_BYTES)


def _row_tile(rows, cap):
    tile = min(rows, cap)
    while rows % tile:
        tile -= SUBLANES
    return tile


def _in_proj_kernel(x_ref, g_ref, w_ref, wdt_ref, cos_ref, sin_ref, o_ref, dt_ref, hn_ref, *,
                    rot_tiles):
    n = pl.program_id(1)

    @pl.when(n == 0)
    def _():
        hn = _rms(x_ref[...], g_ref[...]).astype(BF16)
        hn_ref[...] = hn
        row = lax.broadcasted_iota(jnp.int32, wdt_ref.shape, 0)
        wdt = jnp.where(row < N_SSM_HEADS, wdt_ref[...], 0.0).astype(BF16)
        dt_ref[...] = _dot_nt(hn, wdt)

    acc = _dot_nt(hn_ref[...], w_ref[...].astype(BF16))

    @pl.when(n < rot_tiles)
    def _():
        cos = cos_ref[...]
        sin = sin_ref[...]
        for hh in range(acc.shape[1] // HEAD_DIM):
            xh = acc[:, hh * HEAD_DIM:(hh + 1) * HEAD_DIM]
            o_ref[:, hh * HEAD_DIM:(hh + 1) * HEAD_DIM] = (
                xh * cos + pltpu.roll(xh, HEAD_DIM // 2, axis=1) * sin)

    @pl.when(n >= rot_tiles)
    def _():
        o_ref[...] = acc


def _in_proj(h, g, w_in_t, layer, cos2, sin2):
    t, d = h.shape
    tn = 512
    tm = _row_tile(cos2.shape[0], 1024)
    assert t % tm == 0 and PROJ_COLS % LANES == 0
    period_tiles = cos2.shape[0] // tm
    dt_block = PROJ_COLS // LANES
    kern = functools.partial(_in_proj_kernel, rot_tiles=(D_ATT + D_KV) // tn)
    return pl.pallas_call(
        kern,
        out_shape=(jax.ShapeDtypeStruct((t, PROJ_COLS), F32),
                   jax.ShapeDtypeStruct((t, LANES), F32)),
        grid=(t // tm, PROJ_COLS // tn),
        in_specs=[
            pl.BlockSpec((tm, d), lambda m, n: (m, 0)),
            pl.BlockSpec((1, d), lambda m, n: (0, 0)),
            pl.BlockSpec((None, tn, d), lambda m, n: (layer, n, 0)),
            pl.BlockSpec((None, LANES, d), lambda m, n: (layer, dt_block, 0)),
            pl.BlockSpec((tm, HEAD_DIM), lambda m, n: (m % period_tiles, 0)),
            pl.BlockSpec((tm, HEAD_DIM), lambda m, n: (m % period_tiles, 0)),
        ],
        out_specs=(pl.BlockSpec((tm, tn), lambda m, n: (m, n)),
                   pl.BlockSpec((tm, LANES), lambda m, n: (m, 0))),
        scratch_shapes=[pltpu.VMEM((tm, d), BF16)],
        compiler_params=_params(("parallel", "arbitrary")),
        name="in_proj",
    )(h, g, w_in_t, w_in_t, cos2, sin2)


ATT_BLOCK = 128
ATT_SPAN = max(w for w, _ in DILATED_BRANCHES)


def _prompt_attn_step(q_ref, k_ref, v_ref, o_ref, scr, w, *, seq, steps_per_unit):
    blk = ATT_BLOCK
    span = ATT_SPAN
    nbr = len(DILATED_BRANCHES)
    ob = [scr[2 * bi] for bi in range(nbr)]
    lb = [scr[2 * bi + 1] for bi in range(nbr)]
    n_spans = seq // span
    steps_per_span = steps_per_unit // n_spans
    assert seq % span == 0 and steps_per_unit % n_spans == 0
    s = _idiv(w, steps_per_span)
    ph = _imod(w, steps_per_span)
    s0 = pl.multiple_of(s * span, span)
    ii = lax.broadcasted_iota(jnp.int32, (blk, 2 * blk), 0)
    jj = lax.broadcasted_iota(jnp.int32, (blk, 2 * blk), 1)
    dist = ii + blk - jj

    def rows(start, dil):
        return pl.ds(start, blk, stride=dil) if dil > 1 else pl.ds(start, blk)

    def piece(ref, start, dil):
        return ref[rows(start, dil), :].astype(BF16)

    def attend(bi, dil, base, local, kp, kc, vp, vc, first):
        kb = jnp.concatenate([kp, kc], axis=0)
        vb = jnp.concatenate([vp, vc], axis=0)
        hi = blk if first is None else jnp.where(first, ii, blk)
        mask = (dist >= 0) & (dist <= hi)
        q = piece(q_ref, base, dil)
        sc = jnp.where(mask, _dot_nt(q, kb) * ATT_SCALE, NEG_INF)
        m = jnp.max(sc, axis=1, keepdims=True)
        e = jnp.exp(sc - m)
        ssum = jnp.sum(e, axis=1, keepdims=True)
        p = (e / ssum).astype(BF16)
        ob[bi][rows(local, dil), :] = _dot(p, vb)
        lb[bi][rows(local, dil), :] = jnp.broadcast_to(m + jnp.log(ssum), (blk, HEAD_DIM))

    for bi, (win, dil) in enumerate(DILATED_BRANCHES):
        assert win // dil == blk and span % (dil * blk) == 0
        stride_rows = blk * dil
        per_class = span // stride_rows
        if per_class >= 2:
            half = per_class // 2
            n_iter = dil * half
        else:
            n_iter = dil // 2
        assert n_iter % steps_per_span == 0
        per_step = n_iter // steps_per_span
        for j in range(per_step):
            idx = ph * per_step + j
            if per_class >= 2:
                r = _idiv(idx, half) if dil > 1 else 0
                i = idx - r * half
                local0 = r + (2 * i) * stride_rows
                base0 = s0 + local0
                first = (s == 0) & (i == 0)
                prev = jnp.maximum(base0 - stride_rows, r)
                km, vm = piece(k_ref, prev, dil), piece(v_ref, prev, dil)
                k0, v0 = piece(k_ref, base0, dil), piece(v_ref, base0, dil)
                k1 = piece(k_ref, base0 + stride_rows, dil)
                v1 = piece(v_ref, base0 + stride_rows, dil)
                attend(bi, dil, base0, local0, km, k0, vm, v0, first)
                attend(bi, dil, base0 + stride_rows, local0 + stride_rows, k0, k1, v0, v1, None)
            else:
                for e in range(2):
                    r = 2 * idx + e
                    base = s0 + r
                    prev = jnp.maximum(base - stride_rows, r)
                    attend(bi, dil, base, r, piece(k_ref, prev, dil), piece(k_ref, base, dil),
                           piece(v_ref, prev, dil), piece(v_ref, base, dil), s == 0)

    @pl.when(ph == steps_per_span - 1)
    def _():
        rows_per = 256

        def merge(c, carry):
            loc = pl.multiple_of(c * rows_per, rows_per)
            sl = pl.ds(loc, rows_per)
            ls = [lb[bi][sl, :] for bi in range(nbr)]
            mx = functools.reduce(jnp.maximum, ls)
            ws = [jnp.exp(l - mx) for l in ls]
            num = functools.reduce(lambda a, b: a + b,
                                   [wgt * ob[bi][sl, :] for bi, wgt in enumerate(ws)])
            den = functools.reduce(lambda a, b: a + b, ws)
            o_ref[pl.ds(s0 + loc, rows_per), :] = (num / den).astype(o_ref.dtype)
            return carry

        lax.fori_loop(0, span // rows_per, merge, 0)


def _sample_attn_step(qkv_ref, ck_ref, cv_ref, att_ref, newk_scr, newv_scr, window_copies, *,
                      n_past, t_new):
    qkv = qkv_ref[...]
    k_new = qkv[:, COL_K:COL_K + D_KV]
    v_new = qkv[:, COL_V:COL_V + D_KV]

    copies = window_copies(ck_ref, cv_ref)
    for cp in copies[:2]:
        cp.start()
    for new, stage in ((k_new, newk_scr), (v_new, newv_scr)):
        for j in range(t_new):
            for g in range(N_KV_HEADS):
                r = j * N_KV_HEADS + g
                stage[r:r + 1, :] = new[j:j + 1, g * HEAD_DIM:(g + 1) * HEAD_DIM]
    for cp in copies[2:]:
        cp.start()

    nq = Q_PER_KV * t_new
    cidx = lax.broadcasted_iota(jnp.int32, (nq, n_past), 1)
    tok = _imod(lax.broadcasted_iota(jnp.int32, (nq, n_past), 0), t_new)
    dist_c = n_past + tok - cidx
    tok1 = _imod(lax.broadcasted_iota(jnp.int32, (nq, 1), 0), t_new)

    for g in range(N_KV_HEADS):
        qg = jnp.concatenate(
            [qkv[:, (g * Q_PER_KV + r) * HEAD_DIM:(g * Q_PER_KV + r + 1) * HEAD_DIM]
             for r in range(Q_PER_KV)], axis=0).astype(BF16)
        kc = ck_ref[0, 0, pl.ds(g, n_past, stride=N_KV_HEADS), :].astype(BF16)
        vc = cv_ref[0, 0, pl.ds(g, n_past, stride=N_KV_HEADS), :].astype(BF16)
        kn = k_new[:, g * HEAD_DIM:(g + 1) * HEAD_DIM].astype(BF16).astype(F32)
        vn = v_new[:, g * HEAD_DIM:(g + 1) * HEAD_DIM].astype(BF16).astype(F32)
        s_c = _dot_nt(qg, kc) * ATT_SCALE
        qf = qg.astype(F32)
        s_n = [jnp.sum(qf * kn[j:j + 1, :], axis=1, keepdims=True) * ATT_SCALE
               for j in range(t_new)]

        probs, new_terms, lses = [], [], []
        for win, dil in DILATED_BRANCHES:
            assert win <= n_past
            mask_c = (_imod(dist_c, dil) == 0) & (dist_c <= win)
            sc = jnp.where(mask_c, s_c, NEG_INF)
            m = jnp.max(sc, axis=1, keepdims=True)
            sn = []
            for j in range(t_new):
                dn = tok1 - j
                mask_n = (dn >= 0) & (_imod(dn, dil) == 0)
                snj = jnp.where(mask_n, s_n[j], NEG_INF)
                sn.append(snj)
                m = jnp.maximum(m, snj)
            ec = jnp.exp(sc - m)
            en = [jnp.exp(x - m) for x in sn]
            ssum = jnp.sum(ec, axis=1, keepdims=True)
            for x in en:
                ssum = ssum + x
            probs.append((ec / ssum).astype(BF16))
            o_new = jnp.zeros((nq, HEAD_DIM), F32)
            for j in range(t_new):
                o_new = o_new + (en[j] / ssum).astype(BF16).astype(F32) * vn[j:j + 1, :]
            new_terms.append(o_new)
            lses.append(m + jnp.log(ssum))

        o_all = _dot(jnp.concatenate(probs, axis=0), vc)
        mx = functools.reduce(jnp.maximum, lses)
        ws = [jnp.exp(l - mx) for l in lses]
        den = functools.reduce(lambda a, b: a + b, ws)
        o = jnp.zeros((nq, HEAD_DIM), F32)
        for i, w in enumerate(ws):
            o = o + w * (o_all[i * nq:(i + 1) * nq, :] + new_terms[i])
        o = o / den
        for r in range(Q_PER_KV):
            att_ref[:, (g * Q_PER_KV + r) * HEAD_DIM:(g * Q_PER_KV + r + 1) * HEAD_DIM] = (
                o[r * t_new:(r + 1) * t_new, :])
    return copies


def _attn_kernel(*refs, layer, n_past, t_new, aliased, seq, steps_per_unit):
    if aliased:
        (qkv_ref, ck_ref, cv_ref, q_ref, k_ref, v_ref, _, _,
         att_s_ref, nk_hbm, nv_hbm, att_p_ref, newk_scr, newv_scr, sems, *scr) = refs
    else:
        (qkv_ref, ck_ref, cv_ref, q_ref, k_ref, v_ref,
         att_s_ref, nk_hbm, nv_hbm, att_p_ref, newk_scr, newv_scr, sems, *scr) = refs
    step = pl.program_id(0)
    nrow = n_past * N_KV_HEADS
    shift = t_new * N_KV_HEADS
    assert shift % SUBLANES == 0

    def window_copies(ck, cv):
        kept = pl.ds(0, nrow - shift)
        tail = pl.ds(nrow - shift, shift)
        return [
            pltpu.make_async_copy(ck.at[0, 0, pl.ds(shift, nrow - shift), :],
                                  nk_hbm.at[layer, step, kept, :], sems.at[0]),
            pltpu.make_async_copy(cv.at[0, 0, pl.ds(shift, nrow - shift), :],
                                  nv_hbm.at[layer, step, kept, :], sems.at[1]),
            pltpu.make_async_copy(newk_scr, nk_hbm.at[layer, step, tail, :], sems.at[2]),
            pltpu.make_async_copy(newv_scr, nv_hbm.at[layer, step, tail, :], sems.at[3]),
        ]

    copies = _sample_attn_step(qkv_ref, ck_ref, cv_ref, att_s_ref, newk_scr, newv_scr, window_copies,
                               n_past=n_past, t_new=t_new)
    w = _imod(step, steps_per_unit)
    _prompt_attn_step(q_ref, k_ref, v_ref, att_p_ref, scr, w, seq=seq, steps_per_unit=steps_per_unit)
    for cp in copies:
        cp.wait()


def _attn(qkv_s, cache_k, cache_v, proj, layer, prev_k, prev_v, *, batch, seq):
    depth, bsz, nrow, _ = cache_k.shape
    n_past = nrow // N_KV_HEADS
    t_new = qkv_s.shape[1]
    units = batch * N_Q_HEADS
    steps_per_unit = bsz // units
    assert bsz == units * steps_per_unit
    aliased = prev_k is not None
    kern = functools.partial(_attn_kernel, layer=layer, n_past=n_past, t_new=t_new, aliased=aliased,
                             seq=seq, steps_per_unit=steps_per_unit)
    kq = COL_K // HEAD_DIM
    vq = COL_V // HEAD_DIM

    def unit(i):
        u = i // steps_per_unit
        return u // N_Q_HEADS, u % N_Q_HEADS

    win_spec = pl.BlockSpec((1, 1, nrow, HEAD_DIM), lambda i: (layer, i, 0, 0))
    in_specs = [
        pl.BlockSpec((None, t_new, qkv_s.shape[2]), lambda i: (i, 0, 0)), win_spec, win_spec,
        pl.BlockSpec((seq, HEAD_DIM), lambda i: unit(i)),
        pl.BlockSpec((seq, HEAD_DIM), lambda i: (unit(i)[0], kq + unit(i)[1] // Q_PER_KV)),
        pl.BlockSpec((seq, HEAD_DIM), lambda i: (unit(i)[0], vq + unit(i)[1] // Q_PER_KV)),
    ]
    args = [qkv_s, cache_k, cache_v, proj, proj, proj]
    aliases = {}
    if aliased:
        in_specs += [pl.BlockSpec(memory_space=pl.ANY), pl.BlockSpec(memory_space=pl.ANY)]
        args += [prev_k, prev_v]
        aliases = {6: 1, 7: 2}
    n_scr = len(DILATED_BRANCHES) * 2
    return pl.pallas_call(
        kern,
        out_shape=(jax.ShapeDtypeStruct((bsz, t_new, D_ATT), F32),
                   jax.ShapeDtypeStruct(cache_k.shape, cache_k.dtype),
                   jax.ShapeDtypeStruct(cache_v.shape, cache_v.dtype),
                   jax.ShapeDtypeStruct((batch * seq, D_ATT), BF16)),
        grid=(bsz,),
        in_specs=in_specs,
        out_specs=(pl.BlockSpec((None, t_new, D_ATT), lambda i: (i, 0, 0)),
                   pl.BlockSpec(memory_space=pl.ANY), pl.BlockSpec(memory_space=pl.ANY),
                   pl.BlockSpec((seq, HEAD_DIM), lambda i: unit(i))),
        scratch_shapes=([pltpu.VMEM((t_new * N_KV_HEADS, HEAD_DIM), F32) for _ in range(2)]
                        + [pltpu.SemaphoreType.DMA((4,))]
                        + [pltpu.VMEM((ATT_SPAN, HEAD_DIM), F32) for _ in range(n_scr)]),
        input_output_aliases=aliases,
        compiler_params=pltpu.CompilerParams(dimension_semantics=("arbitrary",),
                                             vmem_limit_bytes=ATTN_VMEM_LIMIT_BYTES),
        name="attn",
    )(*args)


def _expand_mats():
    r64 = np.zeros((LANES, D_SSM), np.float32)
    r128 = np.zeros((LANES, N_SSM_HEADS * LANES), np.float32)
    for h in range(N_SSM_HEADS):
        r64[h, h * SSM_HEAD_DIM:(h + 1) * SSM_HEAD_DIM] = 1.0
        r128[h, h * LANES:(h + 1) * LANES] = 1.0
    return jnp.asarray(r64, BF16), jnp.asarray(r128, BF16)


def _group_sum_mat():
    g = np.zeros((D_BC, D_SSM), np.float32)
    for grp in range(N_SSM_GROUPS):
        g[grp * D_STATE:(grp + 1) * D_STATE,
          grp * HEADS_PER_GROUP * SSM_HEAD_DIM:(grp + 1) * HEADS_PER_GROUP * SSM_HEAD_DIM] = 1.0
    return jnp.asarray(g, BF16)


def _conv_silu(x, tail, cw, cb):
    n = x.shape[0]
    xp = jnp.concatenate([tail, x], axis=0)
    out = cb + cw[3:4, :] * x
    for w in range(CONV_WIDTH - 1):
        off = SUBLANES - (CONV_WIDTH - 1) + w
        out = out + cw[w:w + 1, :] * xp[off:off + n, :]
    return _silu(out)


def _cumsum_rows(x, seg=None):
    n = x.shape[0]
    rows = lax.broadcasted_iota(jnp.int32, x.shape, 0)
    pos = rows if seg is None else _imod(rows, seg)
    limit = n if seg is None else seg
    sh = 1
    while sh < limit:
        x = x + jnp.where(pos >= sh, pltpu.roll(x, sh, axis=0), 0.0)
        sh *= 2
    return x


def _lane_col_block(row):
    return jnp.broadcast_to(row, (LANES, LANES)).T


def _ssd_prompt_kernel(z_ref, xs_ref, bc_ref, dt_ref, cw_ref, cb_ref, dtb_ref, alog_ref, dsk_ref,
                       ng_ref, r64_ref, r128_ref, y_ref, st_ref, cv_ref, h_scr, tail_scr):
    c = pl.program_id(1)
    last = pl.num_programs(1) - 1
    cs = SSD_CHUNK

    @pl.when(c == 0)
    def _():
        h_scr[...] = jnp.zeros_like(h_scr)
        tail_scr[...] = jnp.zeros_like(tail_scr)

    xraw = xs_ref[...]
    bcraw = bc_ref[...]
    tail = tail_scr[...]
    cw = cw_ref[...]
    cb = cb_ref[...]
    xs = _conv_silu(xraw, tail[:, :D_SSM], cw[:, :D_SSM], cb[:, :D_SSM])
    bcm = _conv_silu(bcraw, tail[:, D_SSM:], cw[:, D_SSM:], cb[:, D_SSM:])
    tail_scr[:, :D_SSM] = xraw[cs - SUBLANES:, :]
    tail_scr[:, D_SSM:] = bcraw[cs - SUBLANES:, :]

    @pl.when(c == last)
    def _():
        cv_ref[:, :D_SSM] = xraw[cs - (CONV_WIDTH - 1):, :]
        cv_ref[:, D_SSM:] = bcraw[cs - (CONV_WIDTH - 1):, :]

    dt = _softplus(dt_ref[...] + dtb_ref[...])
    a = -jnp.exp(alog_ref[...])
    acum = _cumsum_rows(dt * a)
    acum_t = acum.T
    r64 = r64_ref[...]
    dt_e = _expand(dt, r64)
    ac_e = _expand(acum, r64)
    col_b = _expand(acum, r128_ref[...])
    last_e = ac_e[cs - 1:cs, :]
    ea_e = jnp.exp(ac_e)
    xdt = xs * dt_e
    xte = (xdt * jnp.exp(last_e - ac_e)).astype(BF16)

    ti = lax.broadcasted_iota(jnp.int32, (cs, cs), 0)
    si = lax.broadcasted_iota(jnp.int32, (cs, cs), 1)
    causal = ti >= si
    lane = lax.broadcasted_iota(jnp.int32, (cs, LANES), 1)
    lo_half = lane < SSM_HEAD_DIM

    gw = HEADS_PER_GROUP * SSM_HEAD_DIM
    y_diag, y_off, states = [], [], []
    for g in range(N_SSM_GROUPS):
        bg = bcm[:, g * D_STATE:(g + 1) * D_STATE].astype(BF16)
        cg = bcm[:, D_BC + g * D_STATE:D_BC + (g + 1) * D_STATE].astype(BF16)
        cbt = _dot_nt(cg, bg)
        h_in = h_scr[g * gw:(g + 1) * gw, :]
        y_off.append(_dot_nt(cg, h_in.astype(BF16)))
        states.append(lax.dot_general(xte[:, g * gw:(g + 1) * gw], bg, TN_DIMS,
                                      preferred_element_type=F32))
        for k in range(HEADS_PER_GROUP // 2):
            pair = g * (HEADS_PER_GROUP // 2) + k
            xp = xdt[:, pair * LANES:(pair + 1) * LANES]
            yd = jnp.zeros((cs, LANES), F32)
            for e in range(2):
                h = 2 * pair + e
                seg = col_b[:, h * LANES:(h + 1) * LANES] - jnp.broadcast_to(acum_t[h:h + 1, :], (cs, cs))
                dec = jnp.exp(jnp.where(causal, seg, NEG_INF))
                cbh = (cbt * dec).astype(BF16)
                xh = jnp.where(lo_half if e == 0 else jnp.logical_not(lo_half), xp, 0.0).astype(BF16)
                yd = yd + _dot(cbh, xh)
            y_diag.append(yd)

    y = (jnp.concatenate(y_diag, axis=1) + jnp.concatenate(y_off, axis=1) * ea_e
         + dsk_ref[...] * xs)
    dec_rows = jnp.exp(jnp.concatenate(
        [_lane_col_block(last_e[:, k * LANES:(k + 1) * LANES]) for k in range(D_SSM // LANES)],
        axis=0))
    h_new = h_scr[...] * dec_rows + jnp.concatenate(states, axis=0)
    h_scr[...] = h_new

    @pl.when(c == last)
    def _():
        st_ref[...] = h_new

    y_ref[...] = _rms(y * _silu(z_ref[...]), ng_ref[...]).astype(y_ref.dtype)


def _ssd_prompt(proj, dt_all, layer, conv_w, conv_b3, dtb, alog, dsk_e, ng3, r64, r128, batch, seq):
    cs = SSD_CHUNK
    nc = seq // cs
    zc, xc, bcc = COL_Z // D_SSM, COL_X // D_SSM, COL_BC // D_SSM
    const2 = lambda b, c: (0, 0)
    return pl.pallas_call(
        _ssd_prompt_kernel,
        out_shape=(jax.ShapeDtypeStruct((batch * seq, D_SSM), BF16),
                   jax.ShapeDtypeStruct((batch, D_SSM, D_STATE), F32),
                   jax.ShapeDtypeStruct((batch, CONV_WIDTH - 1, CONV_DIM), F32)),
        grid=(batch, nc),
        in_specs=[
            pl.BlockSpec((cs, D_SSM), lambda b, c: (b * nc + c, zc)),
            pl.BlockSpec((cs, D_SSM), lambda b, c: (b * nc + c, xc)),
            pl.BlockSpec((cs, D_SSM), lambda b, c: (b * nc + c, bcc)),
            pl.BlockSpec((cs, LANES), lambda b, c: (b * nc + c, 0)),
            pl.BlockSpec((None, CONV_WIDTH, CONV_DIM), lambda b, c: (layer, 0, 0)),
            pl.BlockSpec((None, 1, CONV_DIM), lambda b, c: (layer, 0, 0)),
            pl.BlockSpec((None, 1, LANES), lambda b, c: (layer, 0, 0)),
            pl.BlockSpec((None, 1, LANES), lambda b, c: (layer, 0, 0)),
            pl.BlockSpec((None, 1, D_SSM), lambda b, c: (layer, 0, 0)),
            pl.BlockSpec((None, 1, D_SSM), lambda b, c: (layer, 0, 0)),
            pl.BlockSpec(r64.shape, const2),
            pl.BlockSpec(r128.shape, const2),
        ],
        out_specs=(pl.BlockSpec((cs, D_SSM), lambda b, c: (b * nc + c, 0)),
                   pl.BlockSpec((None, D_SSM, D_STATE), lambda b, c: (b, 0, 0)),
                   pl.BlockSpec((None, CONV_WIDTH - 1, CONV_DIM), lambda b, c: (b, 0, 0))),
        scratch_shapes=[pltpu.VMEM((D_SSM, D_STATE), F32), pltpu.VMEM((SUBLANES, CONV_DIM), F32)],
        compiler_params=_params(("parallel", "arbitrary")),
        name="ssd_prompt",
    )(proj, proj, proj, dt_all, conv_w, conv_b3, dtb, alog, dsk_e, ng3, r64, r128)


def _ssd_sample_kernel(*refs, bt, t_new, aliased):
    if aliased:
        (z_ref, xs_ref, bc_ref, dt_ref, cst_ref, h0_ref, cw_ref, cb_ref, dtb_ref, alog_ref, dsk_ref,
         ng_ref, r64_ref, gs_ref, _, _, y_ref, st_ref, cv_ref, xbc_scr) = refs
    else:
        (z_ref, xs_ref, bc_ref, dt_ref, cst_ref, h0_ref, cw_ref, cb_ref, dtb_ref, alog_ref, dsk_ref,
         ng_ref, r64_ref, gs_ref, y_ref, st_ref, cv_ref, xbc_scr) = refs
    rows = bt * t_new
    kw = CONV_WIDTH - 1
    cw = cw_ref[...]
    cb = cb_ref[...]

    for b in range(bt):
        xb = jnp.concatenate([xs_ref[b * t_new:(b + 1) * t_new, :],
                              bc_ref[b * t_new:(b + 1) * t_new, :]], axis=1)
        xp = jnp.concatenate([cst_ref[b], xb], axis=0)
        out = cb
        for w in range(CONV_WIDTH):
            out = out + cw[w:w + 1, :] * xp[w:w + t_new, :]
        xbc_scr[b * t_new:(b + 1) * t_new, :] = _silu(out)
        cv_ref[b] = xp[t_new:t_new + kw, :]

    xbc = xbc_scr[...]
    xs = xbc[:, :D_SSM]
    bm = xbc[:, D_SSM:D_SSM + D_BC]
    cm = xbc[:, D_SSM + D_BC:]

    tpos = _imod(lax.broadcasted_iota(jnp.int32, (rows, 1), 0), t_new)
    dt = _softplus(dt_ref[...] + dtb_ref[...])
    a = -jnp.exp(alog_ref[...])
    acum = _cumsum_rows(dt * a, seg=t_new)
    r64 = r64_ref[...]
    dt_e = _expand(dt, r64)
    ac_e = _expand(acum, r64)
    v = jnp.where(tpos == t_new - 1, ac_e, 0.0)
    last_e = v
    for d in range(1, t_new):
        last_e = last_e + pltpu.roll(v, rows - d, axis=0)
    ea_e = jnp.exp(ac_e)
    xdt = xs * dt_e
    xte = xdt * jnp.exp(last_e - ac_e)

    cmb = cm.astype(BF16).astype(F32)
    bmb = bm.astype(BF16).astype(F32)
    gs = gs_ref[...]
    y = dsk_ref[...] * xs
    for d in range(t_new):
        if d == 0:
            b_s, x_s, a_s = bmb, xdt, ac_e
        else:
            b_s = pltpu.roll(bmb, d, axis=0)
            x_s = pltpu.roll(xdt, d, axis=0)
            a_s = pltpu.roll(ac_e, d, axis=0)
        cb_e = _expand(cmb * b_s, gs)
        dec = jnp.exp(jnp.where(tpos >= d, ac_e - a_s, NEG_INF))
        y = y + cb_e * dec * x_s

    gw = HEADS_PER_GROUP * SSM_HEAD_DIM
    pad = jnp.zeros((LANES - rows, LANES), F32) if rows < LANES else None
    rowb = _idiv(lax.broadcasted_iota(jnp.int32, (rows, 1), 0), t_new)
    colb = _idiv(lax.broadcasted_iota(jnp.int32, (1, LANES), 1), t_new)
    xte_t = []
    for k in range(D_SSM // LANES):
        blk = xte[:, k * LANES:(k + 1) * LANES]
        if pad is not None:
            blk = jnp.concatenate([blk, pad], axis=0)
        xte_t.append(blk.T)
    e_last = jnp.exp(last_e)
    y_off = [jnp.zeros((rows, gw), F32) for _ in range(N_SSM_GROUPS)]
    for b in range(bt):
        h0 = h0_ref[b]
        new_rows = []
        for g in range(N_SSM_GROUPS):
            cg = cm[:, g * D_STATE:(g + 1) * D_STATE]
            bg = bm[:, g * D_STATE:(g + 1) * D_STATE]
            if pad is not None:
                bg = jnp.concatenate([bg, pad], axis=0)
            bg = bg.astype(BF16)
            h0g = h0[g * gw:(g + 1) * gw, :]
            cgb = jnp.where(rowb == b, cg, 0.0).astype(BF16)
            y_off[g] = y_off[g] + _dot_nt(cgb, h0g.astype(BF16))
            for k in range(gw // LANES):
                blk = g * (gw // LANES) + k
                lhs = jnp.where(colb == b, xte_t[blk], 0.0).astype(BF16)
                st = _dot(lhs, bg)
                r = b * t_new + t_new - 1
                dec = _lane_col_block(e_last[r:r + 1, blk * LANES:(blk + 1) * LANES])
                new_rows.append(h0[blk * LANES:(blk + 1) * LANES, :] * dec + st)
        st_ref[b] = jnp.concatenate(new_rows, axis=0)

    y = y + jnp.concatenate(y_off, axis=1) * ea_e
    y_ref[...] = _rms(y * _silu(z_ref[...]), ng_ref[...]).astype(y_ref.dtype)


def _ssd_sample(proj_s, dt_s, state_conv, state_ssm, layer, conv_w, conv_b3, dtb, alog, dsk_e, ng3,
                r64, gsum, prev_st, prev_cv, *, bt, t_new):
    depth, bsz = state_ssm.shape[:2]
    rows = bt * t_new
    zc, xc, bcc = COL_Z // D_SSM, COL_X // D_SSM, COL_BC // D_SSM
    aliased = prev_st is not None
    kern = functools.partial(_ssd_sample_kernel, bt=bt, t_new=t_new, aliased=aliased)
    const2 = lambda i: (0, 0)
    in_specs = [
        pl.BlockSpec((rows, D_SSM), lambda i: (i, zc)),
        pl.BlockSpec((rows, D_SSM), lambda i: (i, xc)),
        pl.BlockSpec((rows, D_SSM), lambda i: (i, bcc)),
        pl.BlockSpec((rows, LANES), lambda i: (i, 0)),
        pl.BlockSpec((None, bt, CONV_WIDTH - 1, CONV_DIM), lambda i: (layer, i, 0, 0)),
        pl.BlockSpec((None, bt, D_SSM, D_STATE), lambda i: (layer, i, 0, 0)),
        pl.BlockSpec((None, CONV_WIDTH, CONV_DIM), lambda i: (layer, 0, 0)),
        pl.BlockSpec((None, 1, CONV_DIM), lambda i: (layer, 0, 0)),
        pl.BlockSpec((None, 1, LANES), lambda i: (layer, 0, 0)),
        pl.BlockSpec((None, 1, LANES), lambda i: (layer, 0, 0)),
        pl.BlockSpec((None, 1, D_SSM), lambda i: (layer, 0, 0)),
        pl.BlockSpec((None, 1, D_SSM), lambda i: (layer, 0, 0)),
        pl.BlockSpec(r64.shape, const2),
        pl.BlockSpec(gsum.shape, const2),
    ]
    args = [proj_s, proj_s, proj_s, dt_s, state_conv, state_ssm, conv_w, conv_b3, dtb, alog, dsk_e,
            ng3, r64, gsum]
    aliases = {}
    if aliased:
        in_specs += [pl.BlockSpec(memory_space=pl.ANY), pl.BlockSpec(memory_space=pl.ANY)]
        args += [prev_st, prev_cv]
        aliases = {14: 1, 15: 2}
    return pl.pallas_call(
        kern,
        out_shape=(jax.ShapeDtypeStruct((bsz * t_new, D_SSM), BF16),
                   jax.ShapeDtypeStruct(state_ssm.shape, state_ssm.dtype),
                   jax.ShapeDtypeStruct(state_conv.shape, state_conv.dtype)),
        grid=(bsz // bt,),
        in_specs=in_specs,
        out_specs=(pl.BlockSpec((rows, D_SSM), lambda i: (i, 0)),
                   pl.BlockSpec((None, bt, D_SSM, D_STATE), lambda i: (layer, i, 0, 0)),
                   pl.BlockSpec((None, bt, CONV_WIDTH - 1, CONV_DIM), lambda i: (layer, i, 0, 0))),
        scratch_shapes=[pltpu.VMEM((rows, CONV_DIM), F32)],
        input_output_aliases=aliases,
        compiler_params=_params(("parallel",)),
        name="ssd_sample",
    )(*args)


def _resident(shape, index_map):
    return pl.BlockSpec(shape, index_map, pipeline_mode=pl.Buffered(1))


def _out_proj_kernel(att_ref, ssm_ref, h_ref, w_ref, o_ref, wb_ref):
    @pl.when(pl.program_id(0) == 0)
    def _():
        wb_ref[...] = w_ref[...].astype(BF16)

    acc = _dot(att_ref[...], wb_ref[:D_ATT, :]) + _dot(ssm_ref[...], wb_ref[D_ATT:, :])
    o_ref[...] = h_ref[...] + acc


def _out_proj(att, ssm, h, w_out, layer):
    t, d = h.shape
    tm = _row_tile(t, 512)
    return pl.pallas_call(
        _out_proj_kernel,
        out_shape=jax.ShapeDtypeStruct((t, d), F32),
        grid=(t // tm,),
        in_specs=[
            pl.BlockSpec((tm, D_ATT), lambda m: (m, 0)),
            pl.BlockSpec((tm, D_SSM), lambda m: (m, 0)),
            pl.BlockSpec((tm, d), lambda m: (m, 0)),
            _resident((None,) + w_out.shape[1:], lambda m: (layer, 0, 0)),
        ],
        out_specs=pl.BlockSpec((tm, d), lambda m: (m, 0)),
        scratch_shapes=[pltpu.VMEM(w_out.shape[1:], BF16)],
        compiler_params=_params(("arbitrary",)),
        name="out_proj",
    )(att, ssm, h, w_out)


def _ffn_kernel(h_ref, g_ref, wg_ref, wu_ref, wd_ref, o_ref, hf_ref):
    f = pl.program_id(1)

    @pl.when(f == 0)
    def _():
        h = h_ref[...]
        hf_ref[...] = _rms(h, g_ref[...]).astype(BF16)
        o_ref[...] = h

    hf = hf_ref[...]
    act = _silu(_dot(hf, wg_ref[...].astype(BF16))) * _dot(hf, wu_ref[...].astype(BF16))
    o_ref[...] += _dot(act.astype(BF16), wd_ref[...].astype(BF16))


def _ffn(h, g, wg, wu, wd, layer):
    t, d = h.shape
    dff = wg.shape[2]
    tm = _row_tile(t, 1024)
    tf = 256
    assert dff % tf == 0
    return pl.pallas_call(
        _ffn_kernel,
        out_shape=jax.ShapeDtypeStruct((t, d), F32),
        grid=(t // tm, dff // tf),
        in_specs=[
            pl.BlockSpec((tm, d), lambda m, f: (m, 0), pipeline_mode=pl.Buffered(1)),
            pl.BlockSpec((1, d), lambda m, f: (0, 0)),
            pl.BlockSpec((None, d, tf), lambda m, f: (layer, 0, f)),
            pl.BlockSpec((None, d, tf), lambda m, f: (layer, 0, f)),
            pl.BlockSpec((None, tf, d), lambda m, f: (layer, f, 0)),
        ],
        out_specs=pl.BlockSpec((tm, d), lambda m, f: (m, 0)),
        scratch_shapes=[pltpu.VMEM((tm, d), BF16)],
        compiler_params=_params(("parallel", "arbitrary")),
        name="ffn",
    )(h, g, wg, wu, wd)


def _ple_kernel(h_ref, p_ref, g_ref, wg_ref, wp_ref, gf_ref, o_ref, wgb_ref, wpb_ref, *, final, tn):
    @pl.when(pl.program_id(0) == 0)
    def _():
        wgb_ref[...] = wg_ref[...].astype(BF16)
        wpb_ref[...] = wp_ref[...].astype(BF16)

    hn = _rms(h_ref[...], g_ref[...]).astype(BF16)
    pb = p_ref[...].astype(BF16)
    for c in range(h_ref.shape[1] // tn):
        cols = slice(c * tn, (c + 1) * tn)
        gate = jax.nn.sigmoid(_dot(hn, wgb_ref[:, cols]))
        o_ref[:, cols] = h_ref[:, cols] + gate * _dot(pb, wpb_ref[:, cols])
    if final:
        o_ref[...] = _rms(o_ref[...], gf_ref[...])


def _ple(h, p, g, wg, wp, gf, layer, *, final):
    t, d = h.shape
    tm = _row_tile(t, 512)
    kern = functools.partial(_ple_kernel, final=final, tn=512)
    return pl.pallas_call(
        kern,
        out_shape=jax.ShapeDtypeStruct((t, d), F32),
        grid=(t // tm,),
        in_specs=[
            pl.BlockSpec((tm, d), lambda m: (m, 0)),
            pl.BlockSpec((None, tm, p.shape[2]), lambda m: (layer, m, 0)),
            pl.BlockSpec((1, d), lambda m: (0, 0)),
            _resident((None,) + wg.shape[1:], lambda m: (layer, 0, 0)),
            _resident((None,) + wp.shape[1:], lambda m: (layer, 0, 0)),
            pl.BlockSpec((1, d), lambda m: (0, 0)),
        ],
        out_specs=pl.BlockSpec((tm, d), lambda m: (m, 0)),
        scratch_shapes=[pltpu.VMEM(wg.shape[1:], BF16), pltpu.VMEM(wp.shape[1:], BF16)],
        compiler_params=_params(("arbitrary",)),
        name="ple",
    )(h, p, g, wg, wp, gf)


def _rope_tables(pos):
    half = HEAD_DIM // 2
    inv_freq = ROPE_THETA ** (-jnp.arange(half, dtype=F32) / half)
    ang = pos.astype(F32)[:, None] * inv_freq[None, :]
    cos, sin = jnp.cos(ang), jnp.sin(ang)
    return jnp.concatenate([cos, cos], axis=1), jnp.concatenate([-sin, sin], axis=1)


def _pad_lanes(x):
    return jnp.pad(x, [(0, 0)] * (x.ndim - 1) + [(0, LANES - x.shape[-1])])


def kernel(x_prompt, x_sample, cache_k, cache_v, state_ssm, state_conv, p_prompt, p_sample,
           norm_mix_g, w_in, conv_w, conv_b, dt_bias, a_log, d_skip, ssm_norm_g, w_out,
           norm_ffn_g, w_ffn_gate, w_ffn_up, w_ffn_down, norm_ple_g, w_ple_gate, w_ple_proj,
           final_norm_g):
    batch, seq, d = x_prompt.shape
    dec_batch, dec_seq, _ = x_sample.shape
    depth = w_in.shape[0]
    n_past = cache_k.shape[2]
    tp = batch * seq
    ts = dec_batch * dec_seq
    assert w_in.shape[2] == PROJ_COLS + N_SSM_HEADS and n_past == PAST_LEN

    cos_p, sin_p = _rope_tables(jnp.arange(seq, dtype=jnp.int32))
    cos_s, sin_s = _rope_tables(jnp.tile(PAST_LEN + jnp.arange(dec_seq, dtype=jnp.int32), dec_batch))
    r64, r128 = _expand_mats()
    w_in_t = jnp.swapaxes(w_in, 1, 2)
    gsum = _group_sum_mat()

    ck = cache_k.reshape(depth, dec_batch, n_past * N_KV_HEADS, HEAD_DIM)
    cv = cache_v.reshape(depth, dec_batch, n_past * N_KV_HEADS, HEAD_DIM)
    st_in = state_ssm.reshape(depth, dec_batch, D_SSM, D_STATE)
    pp = p_prompt.reshape(depth, tp, -1)
    ps = p_sample.reshape(depth, ts, -1)
    conv_b3 = conv_b.reshape(depth, 1, CONV_DIM)
    dtb = _pad_lanes(dt_bias).reshape(depth, 1, LANES)
    alog = _pad_lanes(a_log).reshape(depth, 1, LANES)
    dsk_e = jnp.repeat(d_skip, SSM_HEAD_DIM, axis=1).reshape(depth, 1, D_SSM)
    ng3 = ssm_norm_g.reshape(depth, 1, D_SSM)
    gf = final_norm_g.reshape(1, d)

    hp = x_prompt.reshape(tp, d)
    hs = x_sample.reshape(ts, d)
    nk_s = nv_s = st_s = cv_s = None
    k_p, v_p, st_p, cv_p = [], [], [], []
    for i in range(depth):
        g_mix = norm_mix_g[i].reshape(1, d)
        g_ffn = norm_ffn_g[i].reshape(1, d)
        g_ple = norm_ple_g[i].reshape(1, d)
        last = i == depth - 1

        proj, dt_p = _in_proj(hp, g_mix, w_in_t, i, cos_p, sin_p)
        proj_s, dt_s = _in_proj(hs, g_mix, w_in_t, i, cos_s, sin_s)
        qkv_s = proj_s[:, :COL_Z].reshape(dec_batch, dec_seq, COL_Z)
        att_s, nk_s, nv_s, att_p = _attn(qkv_s, ck, cv, proj, i, nk_s, nv_s, batch=batch, seq=seq)

        y_p, st_i, cv_i = _ssd_prompt(proj, dt_p, i, conv_w, conv_b3, dtb, alog, dsk_e, ng3,
                                      r64, r128, batch, seq)
        hp = _out_proj(att_p, y_p, hp, w_out, i)
        hp = _ffn(hp, g_ffn, w_ffn_gate, w_ffn_up, w_ffn_down, i)
        hp = _ple(hp, pp, g_ple, w_ple_gate, w_ple_proj, gf, i, final=last)

        keep = min(ATT_WINDOW, seq)
        kv_p = proj.reshape(batch, seq, PROJ_COLS)[:, seq - keep:, COL_K:COL_Z]
        k_p.append(kv_p[..., :D_KV].reshape(batch, keep, N_KV_HEADS, HEAD_DIM))
        v_p.append(kv_p[..., D_KV:].reshape(batch, keep, N_KV_HEADS, HEAD_DIM))
        st_p.append(st_i.reshape(batch, N_SSM_HEADS, SSM_HEAD_DIM, D_STATE))
        cv_p.append(cv_i)

        y_s, st_s, cv_s = _ssd_sample(proj_s, dt_s, state_conv, st_in, i, conv_w, conv_b3,
                                      dtb, alog, dsk_e, ng3, r64, gsum, st_s, cv_s,
                                      bt=16, t_new=dec_seq)
        hs = _out_proj(att_s.reshape(ts, D_ATT).astype(BF16), y_s, hs, w_out, i)
        hs = _ffn(hs, g_ffn, w_ffn_gate, w_ffn_up, w_ffn_down, i)
        hs = _ple(hs, ps, g_ple, w_ple_gate, w_ple_proj, gf, i, final=last)

    return (hp.reshape(batch, seq, d), hs.reshape(dec_batch, dec_seq, d),
            jnp.stack(k_p), jnp.stack(v_p), jnp.stack(st_p), jnp.stack(cv_p),
            nk_s.reshape(cache_k.shape), nv_s.reshape(cache_v.shape),
            st_s.reshape(state_ssm.shape), cv_s)
```

```python
import functools

import numpy as np
import jax
import jax.numpy as jnp
from jax import lax
from jax.experimental import pallas as pl
from jax.experimental.pallas import tpu as pltpu

F32 = jnp.float32
BF16 = jnp.bfloat16

N_Q_HEADS = 8
N_KV_HEADS = 4
Q_PER_KV = N_Q_HEADS // N_KV_HEADS
HEAD_DIM = 128
D_ATT = N_Q_HEADS * HEAD_DIM
D_KV = N_KV_HEADS * HEAD_DIM
DILATED_BRANCHES = ((128, 1), (512, 4), (2048, 16))
ATT_WINDOW = 2048
ROPE_THETA = 10000.0
ATT_SCALE = HEAD_DIM ** -0.5
N_SSM_HEADS = 16
SSM_HEAD_DIM = 64
D_SSM = N_SSM_HEADS * SSM_HEAD_DIM
N_SSM_GROUPS = 4
HEADS_PER_GROUP = N_SSM_HEADS // N_SSM_GROUPS
D_STATE = 128
D_BC = N_SSM_GROUPS * D_STATE
CONV_WIDTH = 4
CONV_DIM = D_SSM + 2 * D_BC
SSD_CHUNK = 128
PAST_LEN = 2048
EPS = 1e-6
NEG_INF = -1e30

LANES = 128
SUBLANES = 8
VMEM_LIMIT_BYTES = 56 * 1024 * 1024
ATTN_VMEM_LIMIT_BYTES = 60 * 1024 * 1024

COL_Q = 0
COL_K = D_ATT
COL_V = D_ATT + D_KV
COL_Z = D_ATT + 2 * D_KV
COL_X = COL_Z + D_SSM
COL_BC = COL_X + D_SSM
PROJ_COLS = COL_BC + 2 * D_BC

NT_DIMS = (((1,), (1,)), ((), ()))
TN_DIMS = (((0,), (0,)), ((), ()))


def _dot(a, b):
    return jnp.dot(a, b, preferred_element_type=F32)


def _dot_nt(a, b):
    return lax.dot_general(a, b, NT_DIMS, preferred_element_type=F32)


def _rms(x, g):
    return x * lax.rsqrt(jnp.mean(x * x, axis=-1, keepdims=True) + EPS) * g


def _silu(x):
    return x * jax.nn.sigmoid(x)


def _softplus(x):
    return jnp.maximum(x, 0.0) + jnp.log1p(jnp.exp(-jnp.abs(x)))


def _expand(x, r):
    hi = x.astype(BF16)
    r1 = x - hi.astype(F32)
    mid = r1.astype(BF16)
    lo = (r1 - mid.astype(F32)).astype(BF16)
    return _dot(hi, r) + _dot(mid, r) + _dot(lo, r)


def _imod(x, n):
    assert n & (n - 1) == 0, "power-of-two divisor expected"
    return x & (n - 1)


def _idiv(x, n):
    assert n & (n - 1) == 0, "power-of-two divisor expected"
    return x >> (n.bit_length() - 1)


def _params(sem):
    return pltpu.CompilerParams(dimension_semantics=sem, vmem_limit_bytes=VMEM_LIMIT_BYTES)


def _row_tile(rows, cap):
    tile = min(rows, cap)
    while rows % tile:
        tile -= SUBLANES
    return tile


def _in_proj_kernel(x_ref, g_ref, w_ref, wdt_ref, cos_ref, sin_ref, o_ref, dt_ref, hn_ref, *,
                    rot_tiles):
    n = pl.program_id(1)

    @pl.when(n == 0)
    def _():
        hn = _rms(x_ref[...], g_ref[...]).astype(BF16)
        hn_ref[...] = hn
        row = lax.broadcasted_iota(jnp.int32, wdt_ref.shape, 0)
        wdt = jnp.where(row < N_SSM_HEADS, wdt_ref[...], 0.0).astype(BF16)
        dt_ref[...] = _dot_nt(hn, wdt)

    acc = _dot_nt(hn_ref[...], w_ref[...].astype(BF16))

    @pl.when(n < rot_tiles)
    def _():
        cos = cos_ref[...]
        sin = sin_ref[...]
        for hh in range(acc.shape[1] // HEAD_DIM):
            xh = acc[:, hh * HEAD_DIM:(hh + 1) * HEAD_DIM]
            o_ref[:, hh * HEAD_DIM:(hh + 1) * HEAD_DIM] = (
                xh * cos + pltpu.roll(xh, HEAD_DIM // 2, axis=1) * sin)

    @pl.when(n >= rot_tiles)
    def _():
        o_ref[...] = acc


def _in_proj(h, g, w_in_t, layer, cos2, sin2):
    t, d = h.shape
    tn = 512
    tm = _row_tile(cos2.shape[0], 1024)
    assert t % tm == 0 and PROJ_COLS % LANES == 0
    period_tiles = cos2.shape[0] // tm
    dt_block = PROJ_COLS // LANES
    kern = functools.partial(_in_proj_kernel, rot_tiles=(D_ATT + D_KV) // tn)
    return pl.pallas_call(
        kern,
        out_shape=(jax.ShapeDtypeStruct((t, PROJ_COLS), F32),
                   jax.ShapeDtypeStruct((t, LANES), F32)),
        grid=(t // tm, PROJ_COLS // tn),
        in_specs=[
            pl.BlockSpec((tm, d), lambda m, n: (m, 0)),
            pl.BlockSpec((1, d), lambda m, n: (0, 0)),
            pl.BlockSpec((None, tn, d), lambda m, n: (layer, n, 0)),
            pl.BlockSpec((None, LANES, d), lambda m, n: (layer, dt_block, 0)),
            pl.BlockSpec((tm, HEAD_DIM), lambda m, n: (m % period_tiles, 0)),
            pl.BlockSpec((tm, HEAD_DIM), lambda m, n: (m % period_tiles, 0)),
        ],
        out_specs=(pl.BlockSpec((tm, tn), lambda m, n: (m, n)),
                   pl.BlockSpec((tm, LANES), lambda m, n: (m, 0))),
        scratch_shapes=[pltpu.VMEM((tm, d), BF16)],
        compiler_params=_params(("parallel", "arbitrary")),
        name="in_proj",
    )(h, g, w_in_t, w_in_t, cos2, sin2)


ATT_BLOCK = 128
ATT_SPAN = max(w for w, _ in DILATED_BRANCHES)


def _prompt_attn_step(q_ref, k_ref, v_ref, o_ref, scr, w, *, seq, steps_per_unit):
    blk = ATT_BLOCK
    span = ATT_SPAN
    nbr = len(DILATED_BRANCHES)
    ob = [scr[2 * bi] for bi in range(nbr)]
    lb = [scr[2 * bi + 1] for bi in range(nbr)]
    n_spans = seq // span
    steps_per_span = steps_per_unit // n_spans
    assert seq % span == 0 and steps_per_unit % n_spans == 0
    s = _idiv(w, steps_per_span)
    ph = _imod(w, steps_per_span)
    s0 = pl.multiple_of(s * span, span)
    ii = lax.broadcasted_iota(jnp.int32, (blk, 2 * blk), 0)
    jj = lax.broadcasted_iota(jnp.int32, (blk, 2 * blk), 1)
    dist = ii + blk - jj

    def rows(start, dil):
        return pl.ds(start, blk, stride=dil) if dil > 1 else pl.ds(start, blk)

    def piece(ref, start, dil):
        return ref[rows(start, dil), :].astype(BF16)

    def attend(bi, dil, blocks):
        scs = []
        for base, _, kp, kc, _, _, first in blocks:
            kb = jnp.concatenate([kp, kc], axis=0)
            hi = blk if first is None else jnp.where(first, ii, blk)
            mask = (dist >= 0) & (dist <= hi)
            q = piece(q_ref, base, dil)
            scs.append(jnp.where(mask, _dot_nt(q, kb) * ATT_SCALE, NEG_INF))
        sc = jnp.concatenate(scs, axis=0)
        m = jnp.max(sc, axis=1, keepdims=True)
        e = jnp.exp(sc - m)
        ssum = jnp.sum(e, axis=1, keepdims=True)
        p = (e / ssum).astype(BF16)
        lse = jnp.broadcast_to(m + jnp.log(ssum), (len(blocks) * blk, HEAD_DIM))
        for n, (_, local, _, _, vp, vc, _) in enumerate(blocks):
            vb = jnp.concatenate([vp, vc], axis=0)
            ob[bi][rows(local, dil), :] = _dot(p[n * blk:(n + 1) * blk, :], vb)
            lb[bi][rows(local, dil), :] = lse[n * blk:(n + 1) * blk, :]

    for bi, (win, dil) in enumerate(DILATED_BRANCHES):
        assert win // dil == blk and span % (dil * blk) == 0
        stride_rows = blk * dil
        per_class = span // stride_rows
        if per_class >= 2:
            half = per_class // 2
            n_iter = dil * half
        else:
            n_iter = dil // 2
        assert n_iter % steps_per_span == 0
        per_step = n_iter // steps_per_span
        for j in range(per_step):
            idx = ph * per_step + j
            if per_class >= 2:
                r = _idiv(idx, half) if dil > 1 else 0
                i = idx - r * half
                local0 = r + (2 * i) * stride_rows
                base0 = s0 + local0
                first = (s == 0) & (i == 0)
                prev = jnp.maximum(base0 - stride_rows, r)
                km, vm = piece(k_ref, prev, dil), piece(v_ref, prev, dil)
                k0, v0 = piece(k_ref, base0, dil), piece(v_ref, base0, dil)
                k1 = piece(k_ref, base0 + stride_rows, dil)
                v1 = piece(v_ref, base0 + stride_rows, dil)
                attend(bi, dil, [
                    (base0, local0, km, k0, vm, v0, first),
                    (base0 + stride_rows, local0 + stride_rows, k0, k1, v0, v1, None)])
            else:
                blocks = []
                for e in range(2):
                    r = 2 * idx + e
                    base = s0 + r
                    prev = jnp.maximum(base - stride_rows, r)
                    blocks.append((base, r, piece(k_ref, prev, dil), piece(k_ref, base, dil),
                                   piece(v_ref, prev, dil), piece(v_ref, base, dil), s == 0))
                attend(bi, dil, blocks)

    @pl.when(ph == steps_per_span - 1)
    def _():
        rows_per = 256

        def merge(c, carry):
            loc = pl.multiple_of(c * rows_per, rows_per)
            sl = pl.ds(loc, rows_per)
            ls = [lb[bi][sl, :] for bi in range(nbr)]
            mx = functools.reduce(jnp.maximum, ls)
            ws = [jnp.exp(l - mx) for l in ls]
            num = functools.reduce(lambda a, b: a + b,
                                   [wgt * ob[bi][sl, :] for bi, wgt in enumerate(ws)])
            den = functools.reduce(lambda a, b: a + b, ws)
            o_ref[pl.ds(s0 + loc, rows_per), :] = (num / den).astype(o_ref.dtype)
            return carry

        lax.fori_loop(0, span // rows_per, merge, 0)


def _sample_attn_step(qkv_ref, ck_ref, cv_ref, att_ref, nk_ref, nv_ref, *, n_past, t_new):
    qkv = qkv_ref[...]
    k_new = qkv[:, COL_K:COL_K + D_KV]
    v_new = qkv[:, COL_V:COL_V + D_KV]

    nrow = n_past * N_KV_HEADS
    shift = t_new * N_KV_HEADS
    assert shift % SUBLANES == 0
    step = 1024
    for src, new, dst in ((ck_ref, k_new, nk_ref), (cv_ref, v_new, nv_ref)):
        for r0 in range(0, nrow - shift, step):
            n = min(step, nrow - shift - r0)
            dst[r0:r0 + n, :] = src[r0 + shift:r0 + shift + n, :]
        for j in range(t_new):
            for g in range(N_KV_HEADS):
                r = nrow - shift + j * N_KV_HEADS + g
                dst[r:r + 1, :] = new[j:j + 1, g * HEAD_DIM:(g + 1) * HEAD_DIM]

    nq = Q_PER_KV * t_new
    nall = N_KV_HEADS * nq
    vcs, vns, s_c, s_n = [], [], [], [[] for _ in range(t_new)]
    for g in range(N_KV_HEADS):
        qg = jnp.concatenate(
            [qkv[:, (g * Q_PER_KV + r) * HEAD_DIM:(g * Q_PER_KV + r + 1) * HEAD_DIM]
             for r in range(Q_PER_KV)], axis=0).astype(BF16)
        kc = ck_ref[pl.ds(g, n_past, stride=N_KV_HEADS), :].astype(BF16)
        vcs.append(cv_ref[pl.ds(g, n_past, stride=N_KV_HEADS), :].astype(BF16))
        kn = k_new[:, g * HEAD_DIM:(g + 1) * HEAD_DIM].astype(BF16).astype(F32)
        vns.append(v_new[:, g * HEAD_DIM:(g + 1) * HEAD_DIM].astype(BF16).astype(F32))
        s_c.append(_dot_nt(qg, kc))
        qf = qg.astype(F32)
        for j in range(t_new):
            s_n[j].append(jnp.sum(qf * kn[j:j + 1, :], axis=1, keepdims=True))
    s_c = jnp.concatenate(s_c, axis=0) * ATT_SCALE
    s_n = [jnp.concatenate(x, axis=0) * ATT_SCALE for x in s_n]

    cidx = lax.broadcasted_iota(jnp.int32, (nall, n_past), 1)
    tok = _imod(lax.broadcasted_iota(jnp.int32, (nall, n_past), 0), t_new)
    dist_c = n_past + tok - cidx
    tok1 = _imod(lax.broadcasted_iota(jnp.int32, (nall, 1), 0), t_new)

    probs, new_probs, lses = [], [], []
    for win, dil in DILATED_BRANCHES:
        assert win <= n_past
        mask_c = (_imod(dist_c, dil) == 0) & (dist_c <= win)
        sc = jnp.where(mask_c, s_c, NEG_INF)
        m = jnp.max(sc, axis=1, keepdims=True)
        sn = []
        for j in range(t_new):
            dn = tok1 - j
            mask_n = (dn >= 0) & (_imod(dn, dil) == 0)
            snj = jnp.where(mask_n, s_n[j], NEG_INF)
            sn.append(snj)
            m = jnp.maximum(m, snj)
        ec = jnp.exp(sc - m)
        en = [jnp.exp(x - m) for x in sn]
        ssum = jnp.sum(ec, axis=1, keepdims=True)
        for x in en:
            ssum = ssum + x
        probs.append((ec / ssum).astype(BF16))
        new_probs.append([(x / ssum).astype(BF16).astype(F32) for x in en])
        lses.append(m + jnp.log(ssum))

    mx = functools.reduce(jnp.maximum, lses)
    ws = [jnp.exp(l - mx) for l in lses]
    den = functools.reduce(lambda a, b: a + b, ws)
    for g in range(N_KV_HEADS):
        rs = slice(g * nq, (g + 1) * nq)
        o_win = _dot(jnp.concatenate([p[rs, :] for p in probs], axis=0), vcs[g])
        o = jnp.zeros((nq, HEAD_DIM), F32)
        for i, wgt in enumerate(ws):
            o_br = o_win[i * nq:(i + 1) * nq, :]
            for j in range(t_new):
                o_br = o_br + new_probs[i][j][rs, :] * vns[g][j:j + 1, :]
            o = o + wgt[rs, :] * o_br
        o = o / den[rs, :]
        for r in range(Q_PER_KV):
            att_ref[:, (g * Q_PER_KV + r) * HEAD_DIM:(g * Q_PER_KV + r + 1) * HEAD_DIM] = (
                o[r * t_new:(r + 1) * t_new, :])


def _attn_kernel(*refs, n_past, t_new, aliased, seq, steps_per_unit):
    if aliased:
        (qkv_ref, ck_ref, cv_ref, q_ref, k_ref, v_ref, _, _,
         att_s_ref, nk_ref, nv_ref, att_p_ref, *scr) = refs
    else:
        (qkv_ref, ck_ref, cv_ref, q_ref, k_ref, v_ref,
         att_s_ref, nk_ref, nv_ref, att_p_ref, *scr) = refs
    _sample_attn_step(qkv_ref, ck_ref, cv_ref, att_s_ref, nk_ref, nv_ref, n_past=n_past, t_new=t_new)
    w = _imod(pl.program_id(0), steps_per_unit)
    _prompt_attn_step(q_ref, k_ref, v_ref, att_p_ref, scr, w, seq=seq, steps_per_unit=steps_per_unit)


def _attn(qkv_s, cache_k, cache_v, proj, layer, prev_k, prev_v, *, batch, seq):
    depth, bsz, nrow, _ = cache_k.shape
    n_past = nrow // N_KV_HEADS
    t_new = qkv_s.shape[1]
    units = batch * N_Q_HEADS
    steps_per_unit = bsz // units
    assert bsz == units * steps_per_unit
    aliased = prev_k is not None
    kern = functools.partial(_attn_kernel, n_past=n_past, t_new=t_new, aliased=aliased, seq=seq,
                             steps_per_unit=steps_per_unit)
    kq = COL_K // HEAD_DIM
    vq = COL_V // HEAD_DIM

    def unit(i):
        u = i // steps_per_unit
        return u // N_Q_HEADS, u % N_Q_HEADS

    win_spec = pl.BlockSpec((None, None, nrow, HEAD_DIM), lambda i: (layer, i, 0, 0))
    in_specs = [
        pl.BlockSpec((None, t_new, qkv_s.shape[2]), lambda i: (i, 0, 0)), win_spec, win_spec,
        pl.BlockSpec((seq, HEAD_DIM), lambda i: unit(i)),
        pl.BlockSpec((seq, HEAD_DIM), lambda i: (unit(i)[0], kq + unit(i)[1] // Q_PER_KV)),
        pl.BlockSpec((seq, HEAD_DIM), lambda i: (unit(i)[0], vq + unit(i)[1] // Q_PER_KV)),
    ]
    args = [qkv_s, cache_k, cache_v, proj, proj, proj]
    aliases = {}
    if aliased:
        in_specs += [pl.BlockSpec(memory_space=pl.ANY), pl.BlockSpec(memory_space=pl.ANY)]
        args += [prev_k, prev_v]
        aliases = {6: 1, 7: 2}
    n_scr = len(DILATED_BRANCHES) * 2
    return pl.pallas_call(
        kern,
        out_shape=(jax.ShapeDtypeStruct((bsz, t_new, D_ATT), F32),
                   jax.ShapeDtypeStruct(cache_k.shape, cache_k.dtype),
                   jax.ShapeDtypeStruct(cache_v.shape, cache_v.dtype),
                   jax.ShapeDtypeStruct((batch * seq, D_ATT), BF16)),
        grid=(bsz,),
        in_specs=in_specs,
        out_specs=(pl.BlockSpec((None, t_new, D_ATT), lambda i: (i, 0, 0)), win_spec, win_spec,
                   pl.BlockSpec((seq, HEAD_DIM), lambda i: unit(i))),
        scratch_shapes=[pltpu.VMEM((ATT_SPAN, HEAD_DIM), F32) for _ in range(n_scr)],
        input_output_aliases=aliases,
        compiler_params=pltpu.CompilerParams(dimension_semantics=("arbitrary",),
                                             vmem_limit_bytes=ATTN_VMEM_LIMIT_BYTES),
        name="attn",
    )(*args)


def _expand_mats():
    r64 = np.zeros((LANES, D_SSM), np.float32)
    r128 = np.zeros((LANES, N_SSM_HEADS * LANES), np.float32)
    for h in range(N_SSM_HEADS):
        r64[h, h * SSM_HEAD_DIM:(h + 1) * SSM_HEAD_DIM] = 1.0
        r128[h, h * LANES:(h + 1) * LANES] = 1.0
    return jnp.asarray(r64, BF16), jnp.asarray(r128, BF16)


def _group_sum_mat():
    g = np.zeros((D_BC, D_SSM), np.float32)
    for grp in range(N_SSM_GROUPS):
        g[grp * D_STATE:(grp + 1) * D_STATE,
          grp * HEADS_PER_GROUP * SSM_HEAD_DIM:(grp + 1) * HEADS_PER_GROUP * SSM_HEAD_DIM] = 1.0
    return jnp.asarray(g, BF16)


def _conv_silu(x, tail, cw, cb):
    n = x.shape[0]
    xp = jnp.concatenate([tail, x], axis=0)
    out = cb + cw[3:4, :] * x
    for w in range(CONV_WIDTH - 1):
        off = SUBLANES - (CONV_WIDTH - 1) + w
        out = out + cw[w:w + 1, :] * xp[off:off + n, :]
    return _silu(out)


def _cumsum_rows(x, seg=None):
    n = x.shape[0]
    rows = lax.broadcasted_iota(jnp.int32, x.shape, 0)
    pos = rows if seg is None else _imod(rows, seg)
    limit = n if seg is None else seg
    sh = 1
    while sh < limit:
        x = x + jnp.where(pos >= sh, pltpu.roll(x, sh, axis=0), 0.0)
        sh *= 2
    return x


def _lane_col_block(row):
    return jnp.broadcast_to(row, (LANES, LANES)).T


def _ssd_prompt_kernel(z_ref, xs_ref, bc_ref, dt_ref, cw_ref, cb_ref, dtb_ref, alog_ref, dsk_ref,
                       ng_ref, r64_ref, r128_ref, y_ref, st_ref, cv_ref, h_scr, tail_scr):
    c = pl.program_id(1)
    last = pl.num_programs(1) - 1
    cs = SSD_CHUNK

    @pl.when(c == 0)
    def _():
        h_scr[...] = jnp.zeros_like(h_scr)
        tail_scr[...] = jnp.zeros_like(tail_scr)

    xraw = xs_ref[...]
    bcraw = bc_ref[...]
    tail = tail_scr[...]
    cw = cw_ref[...]
    cb = cb_ref[...]
    xs = _conv_silu(xraw, tail[:, :D_SSM], cw[:, :D_SSM], cb[:, :D_SSM])
    bcm = _conv_silu(bcraw, tail[:, D_SSM:], cw[:, D_SSM:], cb[:, D_SSM:])
    tail_scr[:, :D_SSM] = xraw[cs - SUBLANES:, :]
    tail_scr[:, D_SSM:] = bcraw[cs - SUBLANES:, :]

    @pl.when(c == last)
    def _():
        cv_ref[:, :D_SSM] = xraw[cs - (CONV_WIDTH - 1):, :]
        cv_ref[:, D_SSM:] = bcraw[cs - (CONV_WIDTH - 1):, :]

    dt = _softplus(dt_ref[...] + dtb_ref[...])
    a = -jnp.exp(alog_ref[...])
    acum = _cumsum_rows(dt * a)
    acum_t = acum.T
    r64 = r64_ref[...]
    dt_e = _expand(dt, r64)
    ac_e = _expand(acum, r64)
    col_b = _expand(acum, r128_ref[...])
    last_e = ac_e[cs - 1:cs, :]
    ea_e = jnp.exp(ac_e)
    xdt = xs * dt_e
    xte = (xdt * jnp.exp(last_e - ac_e)).astype(BF16)

    ti = lax.broadcasted_iota(jnp.int32, (cs, cs), 0)
    si = lax.broadcasted_iota(jnp.int32, (cs, cs), 1)
    causal = ti >= si
    lane = lax.broadcasted_iota(jnp.int32, (cs, LANES), 1)
    lo_half = lane < SSM_HEAD_DIM

    gw = HEADS_PER_GROUP * SSM_HEAD_DIM
    y_diag, y_off, states = [], [], []
    for g in range(N_SSM_GROUPS):
        bg = bcm[:, g * D_STATE:(g + 1) * D_STATE].astype(BF16)
        cg = bcm[:, D_BC + g * D_STATE:D_BC + (g + 1) * D_STATE].astype(BF16)
        cbt = _dot_nt(cg, bg)
        h_in = h_scr[g * gw:(g + 1) * gw, :]
        y_off.append(_dot_nt(cg, h_in.astype(BF16)))
        states.append(lax.dot_general(xte[:, g * gw:(g + 1) * gw], bg, TN_DIMS,
                                      preferred_element_type=F32))
        for k in range(HEADS_PER_GROUP // 2):
            pair = g * (HEADS_PER_GROUP // 2) + k
            xp = xdt[:, pair * LANES:(pair + 1) * LANES]
            yd = jnp.zeros((cs, LANES), F32)
            for e in range(2):
                h = 2 * pair + e
                seg = col_b[:, h * LANES:(h + 1) * LANES] - jnp.broadcast_to(acum_t[h:h + 1, :], (cs, cs))
                dec = jnp.exp(jnp.where(causal, seg, NEG_INF))
                cbh = (cbt * dec).astype(BF16)
                xh = jnp.where(lo_half if e == 0 else jnp.logical_not(lo_half), xp, 0.0).astype(BF16)
                yd = yd + _dot(cbh, xh)
            y_diag.append(yd)

    y = (jnp.concatenate(y_diag, axis=1) + jnp.concatenate(y_off, axis=1) * ea_e
         + dsk_ref[...] * xs)
    dec_rows = jnp.exp(jnp.concatenate(
        [_lane_col_block(last_e[:, k * LANES:(k + 1) * LANES]) for k in range(D_SSM // LANES)],
        axis=0))
    h_new = h_scr[...] * dec_rows + jnp.concatenate(states, axis=0)
    h_scr[...] = h_new

    @pl.when(c == last)
    def _():
        st_ref[...] = h_new

    y_ref[...] = _rms(y * _silu(z_ref[...]), ng_ref[...]).astype(y_ref.dtype)


def _ssd_prompt(proj, dt_all, layer, conv_w, conv_b3, dtb, alog, dsk_e, ng3, r64, r128, batch, seq):
    cs = SSD_CHUNK
    nc = seq // cs
    zc, xc, bcc = COL_Z // D_SSM, COL_X // D_SSM, COL_BC // D_SSM
    const2 = lambda b, c: (0, 0)
    return pl.pallas_call(
        _ssd_prompt_kernel,
        out_shape=(jax.ShapeDtypeStruct((batch * seq, D_SSM), BF16),
                   jax.ShapeDtypeStruct((batch, D_SSM, D_STATE), F32),
                   jax.ShapeDtypeStruct((batch, CONV_WIDTH - 1, CONV_DIM), F32)),
        grid=(batch, nc),
        in_specs=[
            pl.BlockSpec((cs, D_SSM), lambda b, c: (b * nc + c, zc)),
            pl.BlockSpec((cs, D_SSM), lambda b, c: (b * nc + c, xc)),
            pl.BlockSpec((cs, D_SSM), lambda b, c: (b * nc + c, bcc)),
            pl.BlockSpec((cs, LANES), lambda b, c: (b * nc + c, 0)),
            pl.BlockSpec((None, CONV_WIDTH, CONV_DIM), lambda b, c: (layer, 0, 0)),
            pl.BlockSpec((None, 1, CONV_DIM), lambda b, c: (layer, 0, 0)),
            pl.BlockSpec((None, 1, LANES), lambda b, c: (layer, 0, 0)),
            pl.BlockSpec((None, 1, LANES), lambda b, c: (layer, 0, 0)),
            pl.BlockSpec((None, 1, D_SSM), lambda b, c: (layer, 0, 0)),
            pl.BlockSpec((None, 1, D_SSM), lambda b, c: (layer, 0, 0)),
            pl.BlockSpec(r64.shape, const2),
            pl.BlockSpec(r128.shape, const2),
        ],
        out_specs=(pl.BlockSpec((cs, D_SSM), lambda b, c: (b * nc + c, 0)),
                   pl.BlockSpec((None, D_SSM, D_STATE), lambda b, c: (b, 0, 0)),
                   pl.BlockSpec((None, CONV_WIDTH - 1, CONV_DIM), lambda b, c: (b, 0, 0))),
        scratch_shapes=[pltpu.VMEM((D_SSM, D_STATE), F32), pltpu.VMEM((SUBLANES, CONV_DIM), F32)],
        compiler_params=_params(("parallel", "arbitrary")),
        name="ssd_prompt",
    )(proj, proj, proj, dt_all, conv_w, conv_b3, dtb, alog, dsk_e, ng3, r64, r128)


def _ssd_sample_kernel(*refs, bt, t_new, aliased):
    if aliased:
        (z_ref, xs_ref, bc_ref, dt_ref, cst_ref, h0_ref, cw_ref, cb_ref, dtb_ref, alog_ref, dsk_ref,
         ng_ref, r64_ref, gs_ref, _, _, y_ref, st_ref, cv_ref, xbc_scr) = refs
    else:
        (z_ref, xs_ref, bc_ref, dt_ref, cst_ref, h0_ref, cw_ref, cb_ref, dtb_ref, alog_ref, dsk_ref,
         ng_ref, r64_ref, gs_ref, y_ref, st_ref, cv_ref, xbc_scr) = refs
    rows = bt * t_new
    kw = CONV_WIDTH - 1
    cw = cw_ref[...]
    cb = cb_ref[...]

    for b in range(bt):
        xb = jnp.concatenate([xs_ref[b * t_new:(b + 1) * t_new, :],
                              bc_ref[b * t_new:(b + 1) * t_new, :]], axis=1)
        xp = jnp.concatenate([cst_ref[b], xb], axis=0)
        out = cb
        for w in range(CONV_WIDTH):
            out = out + cw[w:w + 1, :] * xp[w:w + t_new, :]
        xbc_scr[b * t_new:(b + 1) * t_new, :] = _silu(out)
        cv_ref[b] = xp[t_new:t_new + kw, :]

    xbc = xbc_scr[...]
    xs = xbc[:, :D_SSM]
    bm = xbc[:, D_SSM:D_SSM + D_BC]
    cm = xbc[:, D_SSM + D_BC:]

    tpos = _imod(lax.broadcasted_iota(jnp.int32, (rows, 1), 0), t_new)
    dt = _softplus(dt_ref[...] + dtb_ref[...])
    a = -jnp.exp(alog_ref[...])
    acum = _cumsum_rows(dt * a, seg=t_new)
    r64 = r64_ref[...]
    dt_e = _expand(dt, r64)
    ac_e = _expand(acum, r64)
    v = jnp.where(tpos == t_new - 1, ac_e, 0.0)
    last_e = v
    for d in range(1, t_new):
        last_e = last_e + pltpu.roll(v, rows - d, axis=0)
    ea_e = jnp.exp(ac_e)
    xdt = xs * dt_e
    xte = xdt * jnp.exp(last_e - ac_e)

    cmb = cm.astype(BF16).astype(F32)
    bmb = bm.astype(BF16).astype(F32)
    gs = gs_ref[...]
    y = dsk_ref[...] * xs
    for d in range(t_new):
        if d == 0:
            b_s, x_s, a_s = bmb, xdt, ac_e
        else:
            b_s = pltpu.roll(bmb, d, axis=0)
            x_s = pltpu.roll(xdt, d, axis=0)
            a_s = pltpu.roll(ac_e, d, axis=0)
        cb_e = _expand(cmb * b_s, gs)
        dec = jnp.exp(jnp.where(tpos >= d, ac_e - a_s, NEG_INF))
        y = y + cb_e * dec * x_s

    gw = HEADS_PER_GROUP * SSM_HEAD_DIM
    pad = jnp.zeros((LANES - rows, LANES), F32) if rows < LANES else None
    rowb = _idiv(lax.broadcasted_iota(jnp.int32, (rows, 1), 0), t_new)
    colb = _idiv(lax.broadcasted_iota(jnp.int32, (1, LANES), 1), t_new)
    xte_t = []
    for k in range(D_SSM // LANES):
        blk = xte[:, k * LANES:(k + 1) * LANES]
        if pad is not None:
            blk = jnp.concatenate([blk, pad], axis=0)
        xte_t.append(blk.T)
    e_last = jnp.exp(last_e)
    y_off = [jnp.zeros((rows, gw), F32) for _ in range(N_SSM_GROUPS)]
    for b in range(bt):
        h0 = h0_ref[b]
        new_rows = []
        for g in range(N_SSM_GROUPS):
            cg = cm[:, g * D_STATE:(g + 1) * D_STATE]
            bg = bm[:, g * D_STATE:(g + 1) * D_STATE]
            if pad is not None:
                bg = jnp.concatenate([bg, pad], axis=0)
            bg = bg.astype(BF16)
            h0g = h0[g * gw:(g + 1) * gw, :]
            cgb = jnp.where(rowb == b, cg, 0.0).astype(BF16)
            y_off[g] = y_off[g] + _dot_nt(cgb, h0g.astype(BF16))
            for k in range(gw // LANES):
                blk = g * (gw // LANES) + k
                lhs = jnp.where(colb == b, xte_t[blk], 0.0).astype(BF16)
                st = _dot(lhs, bg)
                r = b * t_new + t_new - 1
                dec = _lane_col_block(e_last[r:r + 1, blk * LANES:(blk + 1) * LANES])
                new_rows.append(h0[blk * LANES:(blk + 1) * LANES, :] * dec + st)
        st_ref[b] = jnp.concatenate(new_rows, axis=0)

    y = y + jnp.concatenate(y_off, axis=1) * ea_e
    y_ref[...] = _rms(y * _silu(z_ref[...]), ng_ref[...]).astype(y_ref.dtype)


def _ssd_sample(proj_s, dt_s, state_conv, state_ssm, layer, conv_w, conv_b3, dtb, alog, dsk_e, ng3,
                r64, gsum, prev_st, prev_cv, *, bt, t_new):
    depth, bsz = state_ssm.shape[:2]
    rows = bt * t_new
    zc, xc, bcc = COL_Z // D_SSM, COL_X // D_SSM, COL_BC // D_SSM
    aliased = prev_st is not None
    kern = functools.partial(_ssd_sample_kernel, bt=bt, t_new=t_new, aliased=aliased)
    const2 = lambda i: (0, 0)
    in_specs = [
        pl.BlockSpec((rows, D_SSM), lambda i: (i, zc)),
        pl.BlockSpec((rows, D_SSM), lambda i: (i, xc)),
        pl.BlockSpec((rows, D_SSM), lambda i: (i, bcc)),
        pl.BlockSpec((rows, LANES), lambda i: (i, 0)),
        pl.BlockSpec((None, bt, CONV_WIDTH - 1, CONV_DIM), lambda i: (layer, i, 0, 0)),
        pl.BlockSpec((None, bt, D_SSM, D_STATE), lambda i: (layer, i, 0, 0)),
        pl.BlockSpec((None, CONV_WIDTH, CONV_DIM), lambda i: (layer, 0, 0)),
        pl.BlockSpec((None, 1, CONV_DIM), lambda i: (layer, 0, 0)),
        pl.BlockSpec((None, 1, LANES), lambda i: (layer, 0, 0)),
        pl.BlockSpec((None, 1, LANES), lambda i: (layer, 0, 0)),
        pl.BlockSpec((None, 1, D_SSM), lambda i: (layer, 0, 0)),
        pl.BlockSpec((None, 1, D_SSM), lambda i: (layer, 0, 0)),
        pl.BlockSpec(r64.shape, const2),
        pl.BlockSpec(gsum.shape, const2),
    ]
    args = [proj_s, proj_s, proj_s, dt_s, state_conv, state_ssm, conv_w, conv_b3, dtb, alog, dsk_e,
            ng3, r64, gsum]
    aliases = {}
    if aliased:
        in_specs += [pl.BlockSpec(memory_space=pl.ANY), pl.BlockSpec(memory_space=pl.ANY)]
        args += [prev_st, prev_cv]
        aliases = {14: 1, 15: 2}
    return pl.pallas_call(
        kern,
        out_shape=(jax.ShapeDtypeStruct((bsz * t_new, D_SSM), BF16),
                   jax.ShapeDtypeStruct(state_ssm.shape, state_ssm.dtype),
                   jax.ShapeDtypeStruct(state_conv.shape, state_conv.dtype)),
        grid=(bsz // bt,),
        in_specs=in_specs,
        out_specs=(pl.BlockSpec((rows, D_SSM), lambda i: (i, 0)),
                   pl.BlockSpec((None, bt, D_SSM, D_STATE), lambda i: (layer, i, 0, 0)),
                   pl.BlockSpec((None, bt, CONV_WIDTH - 1, CONV_DIM), lambda i: (layer, i, 0, 0))),
        scratch_shapes=[pltpu.VMEM((rows, CONV_DIM), F32)],
        input_output_aliases=aliases,
        compiler_params=_params(("parallel",)),
        name="ssd_sample",
    )(*args)


def _resident(shape, index_map):
    return pl.BlockSpec(shape, index_map, pipeline_mode=pl.Buffered(1))


def _out_proj_kernel(att_ref, ssm_ref, h_ref, w_ref, o_ref, wb_ref):
    @pl.when(pl.program_id(0) == 0)
    def _():
        wb_ref[...] = w_ref[...].astype(BF16)

    acc = _dot(att_ref[...], wb_ref[:D_ATT, :]) + _dot(ssm_ref[...], wb_ref[D_ATT:, :])
    o_ref[...] = h_ref[...] + acc


def _out_proj(att, ssm, h, w_out, layer):
    t, d = h.shape
    tm = _row_tile(t, 512)
    return pl.pallas_call(
        _out_proj_kernel,
        out_shape=jax.ShapeDtypeStruct((t, d), F32),
        grid=(t // tm,),
        in_specs=[
            pl.BlockSpec((tm, D_ATT), lambda m: (m, 0)),
            pl.BlockSpec((tm, D_SSM), lambda m: (m, 0)),
            pl.BlockSpec((tm, d), lambda m: (m, 0)),
            _resident((None,) + w_out.shape[1:], lambda m: (layer, 0, 0)),
        ],
        out_specs=pl.BlockSpec((tm, d), lambda m: (m, 0)),
        scratch_shapes=[pltpu.VMEM(w_out.shape[1:], BF16)],
        compiler_params=_params(("arbitrary",)),
        name="out_proj",
    )(att, ssm, h, w_out)


def _ffn_kernel(h_ref, g_ref, wg_ref, wu_ref, wd_ref, o_ref, hf_ref):
    f = pl.program_id(1)

    @pl.when(f == 0)
    def _():
        h = h_ref[...]
        hf_ref[...] = _rms(h, g_ref[...]).astype(BF16)
        o_ref[...] = h

    hf = hf_ref[...]
    act = _silu(_dot(hf, wg_ref[...].astype(BF16))) * _dot(hf, wu_ref[...].astype(BF16))
    o_ref[...] += _dot(act.astype(BF16), wd_ref[...].astype(BF16))


def _ffn(h, g, wg, wu, wd, layer):
    t, d = h.shape
    dff = wg.shape[2]
    tm = _row_tile(t, 1024)
    tf = 256
    assert dff % tf == 0
    return pl.pallas_call(
        _ffn_kernel,
        out_shape=jax.ShapeDtypeStruct((t, d), F32),
        grid=(t // tm, dff // tf),
        in_specs=[
            pl.BlockSpec((tm, d), lambda m, f: (m, 0), pipeline_mode=pl.Buffered(1)),
            pl.BlockSpec((1, d), lambda m, f: (0, 0)),
            pl.BlockSpec((None, d, tf), lambda m, f: (layer, 0, f)),
            pl.BlockSpec((None, d, tf), lambda m, f: (layer, 0, f)),
            pl.BlockSpec((None, tf, d), lambda m, f: (layer, f, 0)),
        ],
        out_specs=pl.BlockSpec((tm, d), lambda m, f: (m, 0)),
        scratch_shapes=[pltpu.VMEM((tm, d), BF16)],
        compiler_params=_params(("parallel", "arbitrary")),
        name="ffn",
    )(h, g, wg, wu, wd)


def _ple_kernel(h_ref, p_ref, g_ref, wg_ref, wp_ref, gf_ref, o_ref, wgb_ref, wpb_ref, *, final, tn):
    @pl.when(pl.program_id(0) == 0)
    def _():
        wgb_ref[...] = wg_ref[...].astype(BF16)
        wpb_ref[...] = wp_ref[...].astype(BF16)

    hn = _rms(h_ref[...], g_ref[...]).astype(BF16)
    pb = p_ref[...].astype(BF16)
    for c in range(h_ref.shape[1] // tn):
        cols = slice(c * tn, (c + 1) * tn)
        gate = jax.nn.sigmoid(_dot(hn, wgb_ref[:, cols]))
        o_ref[:, cols] = h_ref[:, cols] + gate * _dot(pb, wpb_ref[:, cols])
    if final:
        o_ref[...] = _rms(o_ref[...], gf_ref[...])


def _ple(h, p, g, wg, wp, gf, layer, *, final):
    t, d = h.shape
    tm = _row_tile(t, 512)
    kern = functools.partial(_ple_kernel, final=final, tn=512)
    return pl.pallas_call(
        kern,
        out_shape=jax.ShapeDtypeStruct((t, d), F32),
        grid=(t // tm,),
        in_specs=[
            pl.BlockSpec((tm, d), lambda m: (m, 0)),
            pl.BlockSpec((None, tm, p.shape[2]), lambda m: (layer, m, 0)),
            pl.BlockSpec((1, d), lambda m: (0, 0)),
            _resident((None,) + wg.shape[1:], lambda m: (layer, 0, 0)),
            _resident((None,) + wp.shape[1:], lambda m: (layer, 0, 0)),
            pl.BlockSpec((1, d), lambda m: (0, 0)),
        ],
        out_specs=pl.BlockSpec((tm, d), lambda m: (m, 0)),
        scratch_shapes=[pltpu.VMEM(wg.shape[1:], BF16), pltpu.VMEM(wp.shape[1:], BF16)],
        compiler_params=_params(("arbitrary",)),
        name="ple",
    )(h, p, g, wg, wp, gf)


def _rope_tables(pos):
    half = HEAD_DIM // 2
    inv_freq = ROPE_THETA ** (-jnp.arange(half, dtype=F32) / half)
    ang = pos.astype(F32)[:, None] * inv_freq[None, :]
    cos, sin = jnp.cos(ang), jnp.sin(ang)
    return jnp.concatenate([cos, cos], axis=1), jnp.concatenate([-sin, sin], axis=1)


def _pad_lanes(x):
    return jnp.pad(x, [(0, 0)] * (x.ndim - 1) + [(0, LANES - x.shape[-1])])


def kernel(x_prompt, x_sample, cache_k, cache_v, state_ssm, state_conv, p_prompt, p_sample,
           norm_mix_g, w_in, conv_w, conv_b, dt_bias, a_log, d_skip, ssm_norm_g, w_out,
           norm_ffn_g, w_ffn_gate, w_ffn_up, w_ffn_down, norm_ple_g, w_ple_gate, w_ple_proj,
           final_norm_g):
    batch, seq, d = x_prompt.shape
    dec_batch, dec_seq, _ = x_sample.shape
    depth = w_in.shape[0]
    n_past = cache_k.shape[2]
    tp = batch * seq
    ts = dec_batch * dec_seq
    assert w_in.shape[2] == PROJ_COLS + N_SSM_HEADS and n_past == PAST_LEN

    cos_p, sin_p = _rope_tables(jnp.arange(seq, dtype=jnp.int32))
    cos_s, sin_s = _rope_tables(jnp.tile(PAST_LEN + jnp.arange(dec_seq, dtype=jnp.int32), dec_batch))
    r64, r128 = _expand_mats()
    w_in_t = jnp.swapaxes(w_in, 1, 2)
    gsum = _group_sum_mat()

    ck = cache_k.reshape(depth, dec_batch, n_past * N_KV_HEADS, HEAD_DIM)
    cv = cache_v.reshape(depth, dec_batch, n_past * N_KV_HEADS, HEAD_DIM)
    st_in = state_ssm.reshape(depth, dec_batch, D_SSM, D_STATE)
    pp = p_prompt.reshape(depth, tp, -1)
    ps = p_sample.reshape(depth, ts, -1)
    conv_b3 = conv_b.reshape(depth, 1, CONV_DIM)
    dtb = _pad_lanes(dt_bias).reshape(depth, 1, LANES)
    alog = _pad_lanes(a_log).reshape(depth, 1, LANES)
    dsk_e = jnp.repeat(d_skip, SSM_HEAD_DIM, axis=1).reshape(depth, 1, D_SSM)
    ng3 = ssm_norm_g.reshape(depth, 1, D_SSM)
    gf = final_norm_g.reshape(1, d)

    hp = x_prompt.reshape(tp, d)
    hs = x_sample.reshape(ts, d)
    nk_s = nv_s = st_s = cv_s = None
    k_p, v_p, st_p, cv_p = [], [], [], []
    for i in range(depth):
        g_mix = norm_mix_g[i].reshape(1, d)
        g_ffn = norm_ffn_g[i].reshape(1, d)
        g_ple = norm_ple_g[i].reshape(1, d)
        last = i == depth - 1

        proj, dt_p = _in_proj(hp, g_mix, w_in_t, i, cos_p, sin_p)
        proj_s, dt_s = _in_proj(hs, g_mix, w_in_t, i, cos_s, sin_s)
        qkv_s = proj_s[:, :COL_Z].reshape(dec_batch, dec_seq, COL_Z)
        att_s, nk_s, nv_s, att_p = _attn(qkv_s, ck, cv, proj, i, nk_s, nv_s, batch=batch, seq=seq)

        y_p, st_i, cv_i = _ssd_prompt(proj, dt_p, i, conv_w, conv_b3, dtb, alog, dsk_e, ng3,
                                      r64, r128, batch, seq)
        hp = _out_proj(att_p, y_p, hp, w_out, i)
        hp = _ffn(hp, g_ffn, w_ffn_gate, w_ffn_up, w_ffn_down, i)
        hp = _ple(hp, pp, g_ple, w_ple_gate, w_ple_proj, gf, i, final=last)

        keep = min(ATT_WINDOW, seq)
        kv_p = proj.reshape(batch, seq, PROJ_COLS)[:, seq - keep:, COL_K:COL_Z]
        k_p.append(kv_p[..., :D_KV].reshape(batch, keep, N_KV_HEADS, HEAD_DIM))
        v_p.append(kv_p[..., D_KV:].reshape(batch, keep, N_KV_HEADS, HEAD_DIM))
        st_p.append(st_i.reshape(batch, N_SSM_HEADS, SSM_HEAD_DIM, D_STATE))
        cv_p.append(cv_i)

        y_s, st_s, cv_s = _ssd_sample(proj_s, dt_s, state_conv, st_in, i, conv_w, conv_b3,
                                      dtb, alog, dsk_e, ng3, r64, gsum, st_s, cv_s,
                                      bt=16, t_new=dec_seq)
        hs = _out_proj(att_s.reshape(ts, D_ATT).astype(BF16), y_s, hs, w_out, i)
        hs = _ffn(hs, g_ffn, w_ffn_gate, w_ffn_up, w_ffn_down, i)
        hs = _ple(hs, ps, g_ple, w_ple_gate, w_ple_proj, gf, i, final=last)

    return (hp.reshape(batch, seq, d), hs.reshape(dec_batch, dec_seq, d),
            jnp.stack(k_p), jnp.stack(v_p), jnp.stack(st_p), jnp.stack(cv_p),
            nk_s.reshape(cache_k.shape), nv_s.reshape(cache_v.shape),
            st_s.reshape(state_ssm.shape), cv_s)
```

```python
import functools

import numpy as np
import jax
import jax.numpy as jnp
from jax import lax
from jax.experimental import pallas as pl
from jax.experimental.pallas import tpu as pltpu

F32 = jnp.float32
BF16 = jnp.bfloat16

N_Q_HEADS = 8
N_KV_HEADS = 4
Q_PER_KV = N_Q_HEADS // N_KV_HEADS
HEAD_DIM = 128
D_ATT = N_Q_HEADS * HEAD_DIM
D_KV = N_KV_HEADS * HEAD_DIM
DILATED_BRANCHES = ((128, 1), (512, 4), (2048, 16))
ATT_WINDOW = 2048
ROPE_THETA = 10000.0
ATT_SCALE = HEAD_DIM ** -0.5
N_SSM_HEADS = 16
SSM_HEAD_DIM = 64
D_SSM = N_SSM_HEADS * SSM_HEAD_DIM
N_SSM_GROUPS = 4
HEADS_PER_GROUP = N_SSM_HEADS // N_SSM_GROUPS
D_STATE = 128
D_BC = N_SSM_GROUPS * D_STATE
CONV_WIDTH = 4
CONV_DIM = D_SSM + 2 * D_BC
SSD_CHUNK = 128
PAST_LEN = 2048
EPS = 1e-6
NEG_INF = -1e30

LANES = 128
SUBLANES = 8
VMEM_LIMIT_BYTES = 56 * 1024 * 1024
ATTN_VMEM_LIMIT_BYTES = 60 * 1024 * 1024

COL_Q = 0
COL_K = D_ATT
COL_V = D_ATT + D_KV
COL_Z = D_ATT + 2 * D_KV
COL_X = COL_Z + D_SSM
COL_BC = COL_X + D_SSM
PROJ_COLS = COL_BC + 2 * D_BC

NT_DIMS = (((1,), (1,)), ((), ()))
TN_DIMS = (((0,), (0,)), ((), ()))


def _dot(a, b):
    return jnp.dot(a, b, preferred_element_type=F32)


def _dot_nt(a, b):
    return lax.dot_general(a, b, NT_DIMS, preferred_element_type=F32)


def _rms(x, g):
    return x * lax.rsqrt(jnp.mean(x * x, axis=-1, keepdims=True) + EPS) * g


def _silu(x):
    return x * jax.nn.sigmoid(x)


def _softplus(x):
    return jnp.maximum(x, 0.0) + jnp.log1p(jnp.exp(-jnp.abs(x)))


def _expand(x, r):
    hi = x.astype(BF16)
    r1 = x - hi.astype(F32)
    mid = r1.astype(BF16)
    lo = (r1 - mid.astype(F32)).astype(BF16)
    return _dot(hi, r) + _dot(mid, r) + _dot(lo, r)


def _imod(x, n):
    assert n & (n - 1) == 0, "power-of-two divisor expected"
    return x & (n - 1)


def _idiv(x, n):
    assert n & (n - 1) == 0, "power-of-two divisor expected"
    return x >> (n.bit_length() - 1)


def _params(sem):
    return pltpu.CompilerParams(dimension_semantics=sem, vmem_limit_bytes=VMEM_LIMIT_BYTES)


def _row_tile(rows, cap):
    tile = min(rows, cap)
    while rows % tile:
        tile -= SUBLANES
    return tile


def _in_proj_kernel(x_ref, g_ref, w_ref, wdt_ref, cos_ref, sin_ref, o_ref, dt_ref, *rest,
                    rot_tiles, kv_tiles, period_tiles, first_keep):
    if kv_tiles is None:
        (hn_ref,) = rest
    else:
        ok_ref, ov_ref, hn_ref = rest
    n = pl.program_id(1)

    @pl.when(n == 0)
    def _():
        hn = _rms(x_ref[...], g_ref[...]).astype(BF16)
        hn_ref[...] = hn
        row = lax.broadcasted_iota(jnp.int32, wdt_ref.shape, 0)
        wdt = jnp.where(row < N_SSM_HEADS, wdt_ref[...], 0.0).astype(BF16)
        dt_ref[...] = _dot_nt(hn, wdt)

    acc = _dot_nt(hn_ref[...], w_ref[...].astype(BF16))

    @pl.when(n < rot_tiles)
    def _():
        cos = cos_ref[...]
        sin = sin_ref[...]
        for hh in range(acc.shape[1] // HEAD_DIM):
            xh = acc[:, hh * HEAD_DIM:(hh + 1) * HEAD_DIM]
            o_ref[:, hh * HEAD_DIM:(hh + 1) * HEAD_DIM] = (
                xh * cos + pltpu.roll(xh, HEAD_DIM // 2, axis=1) * sin)

    @pl.when(n >= rot_tiles)
    def _():
        o_ref[...] = acc

    if kv_tiles is not None:
        in_keep = lax.rem(pl.program_id(0), period_tiles) >= first_keep
        rows = o_ref.shape[0]
        for tile, dst in zip(kv_tiles, (ok_ref, ov_ref)):
            @pl.when((n == tile) & in_keep)
            def _(dst=dst):
                for hh in range(N_KV_HEADS):
                    dst[pl.ds(hh, rows, stride=N_KV_HEADS), :] = (
                        o_ref[:, hh * HEAD_DIM:(hh + 1) * HEAD_DIM])


def _in_proj(h, g, w_in_t, layer, cos2, sin2, keep=None):
    t, d = h.shape
    tn = 512
    period = cos2.shape[0]
    tm = _row_tile(period, 1024)
    assert t % tm == 0 and PROJ_COLS % LANES == 0 and tn == D_KV
    period_tiles = period // tm
    dt_block = PROJ_COLS // LANES
    out_shape = [jax.ShapeDtypeStruct((t, PROJ_COLS), F32), jax.ShapeDtypeStruct((t, LANES), F32)]
    out_specs = [pl.BlockSpec((tm, tn), lambda m, n: (m, n)),
                 pl.BlockSpec((tm, LANES), lambda m, n: (m, 0))]
    kv_tiles, first_keep = None, 0
    if keep is not None:
        assert keep % tm == 0
        kv_tiles = (COL_K // tn, COL_V // tn)
        keep_tiles = keep // tm
        first_keep = period_tiles - keep_tiles

        def kept_block(m, n):
            return ((m // period_tiles) * keep_tiles
                    + jnp.maximum(m % period_tiles - first_keep, 0), 0)

        win_rows = (t // period) * keep * N_KV_HEADS
        out_shape += [jax.ShapeDtypeStruct((win_rows, HEAD_DIM), F32)] * 2
        out_specs += [pl.BlockSpec((tm * N_KV_HEADS, HEAD_DIM), kept_block)] * 2
    kern = functools.partial(_in_proj_kernel, rot_tiles=(D_ATT + D_KV) // tn, kv_tiles=kv_tiles,
                             period_tiles=period_tiles, first_keep=first_keep)
    return pl.pallas_call(
        kern,
        out_shape=tuple(out_shape),
        grid=(t // tm, PROJ_COLS // tn),
        in_specs=[
            pl.BlockSpec((tm, d), lambda m, n: (m, 0)),
            pl.BlockSpec((1, d), lambda m, n: (0, 0)),
            pl.BlockSpec((None, tn, d), lambda m, n: (layer, n, 0)),
            pl.BlockSpec((None, LANES, d), lambda m, n: (layer, dt_block, 0)),
            pl.BlockSpec((tm, HEAD_DIM), lambda m, n: (m % period_tiles, 0)),
            pl.BlockSpec((tm, HEAD_DIM), lambda m, n: (m % period_tiles, 0)),
        ],
        out_specs=tuple(out_specs),
        scratch_shapes=[pltpu.VMEM((tm, d), BF16)],
        compiler_params=_params(("arbitrary", "arbitrary")),
        name="in_proj",
    )(h, g, w_in_t, w_in_t, cos2, sin2)


ATT_BLOCK = 128
ATT_SPAN = max(w for w, _ in DILATED_BRANCHES)


def _prompt_attn_step(q_ref, k_ref, v_ref, o_ref, scr, w, *, seq, steps_per_unit):
    blk = ATT_BLOCK
    span = ATT_SPAN
    nbr = len(DILATED_BRANCHES)
    ob = [scr[2 * bi] for bi in range(nbr)]
    lb = [scr[2 * bi + 1] for bi in range(nbr)]
    n_spans = seq // span
    steps_per_span = steps_per_unit // n_spans
    assert seq % span == 0 and steps_per_unit % n_spans == 0
    s = _idiv(w, steps_per_span)
    ph = _imod(w, steps_per_span)
    s0 = pl.multiple_of(s * span, span)
    ii = lax.broadcasted_iota(jnp.int32, (blk, 2 * blk), 0)
    jj = lax.broadcasted_iota(jnp.int32, (blk, 2 * blk), 1)
    dist = ii + blk - jj

    def rows(start, dil):
        return pl.ds(start, blk, stride=dil) if dil > 1 else pl.ds(start, blk)

    def piece(ref, start, dil):
        return ref[rows(start, dil), :].astype(BF16)

    def attend(bi, dil, blocks):
        scs = []
        for base, _, kp, kc, _, _, first in blocks:
            kb = jnp.concatenate([kp, kc], axis=0)
            hi = blk if first is None else jnp.where(first, ii, blk)
            mask = (dist >= 0) & (dist <= hi)
            q = piece(q_ref, base, dil)
            scs.append(jnp.where(mask, _dot_nt(q, kb) * ATT_SCALE, NEG_INF))
        sc = jnp.concatenate(scs, axis=0)
        m = jnp.max(sc, axis=1, keepdims=True)
        e = jnp.exp(sc - m)
        ssum = jnp.sum(e, axis=1, keepdims=True)
        p = (e / ssum).astype(BF16)
        lse = jnp.broadcast_to(m + jnp.log(ssum), (len(blocks) * blk, HEAD_DIM))
        for n, (_, local, _, _, vp, vc, _) in enumerate(blocks):
            vb = jnp.concatenate([vp, vc], axis=0)
            ob[bi][rows(local, dil), :] = _dot(p[n * blk:(n + 1) * blk, :], vb)
            lb[bi][rows(local, dil), :] = lse[n * blk:(n + 1) * blk, :]

    for bi, (win, dil) in enumerate(DILATED_BRANCHES):
        assert win // dil == blk and span % (dil * blk) == 0
        stride_rows = blk * dil
        per_class = span // stride_rows
        if per_class >= 2:
            half = per_class // 2
            n_iter = dil * half
        else:
            n_iter = dil // 2
        assert n_iter % steps_per_span == 0
        per_step = n_iter // steps_per_span
        for j in range(per_step):
            idx = ph * per_step + j
            if per_class >= 2:
                r = _idiv(idx, half) if dil > 1 else 0
                i = idx - r * half
                local0 = r + (2 * i) * stride_rows
                base0 = s0 + local0
                first = (s == 0) & (i == 0)
                prev = jnp.maximum(base0 - stride_rows, r)
                km, vm = piece(k_ref, prev, dil), piece(v_ref, prev, dil)
                k0, v0 = piece(k_ref, base0, dil), piece(v_ref, base0, dil)
                k1 = piece(k_ref, base0 + stride_rows, dil)
                v1 = piece(v_ref, base0 + stride_rows, dil)
                attend(bi, dil, [
                    (base0, local0, km, k0, vm, v0, first),
                    (base0 + stride_rows, local0 + stride_rows, k0, k1, v0, v1, None)])
            else:
                blocks = []
                for e in range(2):
                    r = 2 * idx + e
                    base = s0 + r
                    prev = jnp.maximum(base - stride_rows, r)
                    blocks.append((base, r, piece(k_ref, prev, dil), piece(k_ref, base, dil),
                                   piece(v_ref, prev, dil), piece(v_ref, base, dil), s == 0))
                attend(bi, dil, blocks)

    @pl.when(ph == steps_per_span - 1)
    def _():
        rows_per = 256

        def merge(c, carry):
            loc = pl.multiple_of(c * rows_per, rows_per)
            sl = pl.ds(loc, rows_per)
            ls = [lb[bi][sl, :] for bi in range(nbr)]
            mx = functools.reduce(jnp.maximum, ls)
            ws = [jnp.exp(l - mx) for l in ls]
            num = functools.reduce(lambda a, b: a + b,
                                   [wgt * ob[bi][sl, :] for bi, wgt in enumerate(ws)])
            den = functools.reduce(lambda a, b: a + b, ws)
            o_ref[pl.ds(s0 + loc, rows_per), :] = (num / den).astype(o_ref.dtype)
            return carry

        lax.fori_loop(0, span // rows_per, merge, 0)


def _sample_attn_step(qkv_ref, ck_ref, cv_ref, att_ref, nk_ref, nv_ref, *, n_past, t_new):
    qkv = qkv_ref[...]
    k_new = qkv[:, COL_K:COL_K + D_KV]
    v_new = qkv[:, COL_V:COL_V + D_KV]

    nrow = n_past * N_KV_HEADS
    shift = t_new * N_KV_HEADS
    assert shift % SUBLANES == 0
    step = 1024
    for src, new, dst in ((ck_ref, k_new, nk_ref), (cv_ref, v_new, nv_ref)):
        for r0 in range(0, nrow - shift, step):
            n = min(step, nrow - shift - r0)
            dst[r0:r0 + n, :] = src[r0 + shift:r0 + shift + n, :]
        for j in range(t_new):
            for g in range(N_KV_HEADS):
                r = nrow - shift + j * N_KV_HEADS + g
                dst[r:r + 1, :] = new[j:j + 1, g * HEAD_DIM:(g + 1) * HEAD_DIM]

    nq = Q_PER_KV * t_new
    nall = N_KV_HEADS * nq
    vcs, vns, s_c, s_n = [], [], [], [[] for _ in range(t_new)]
    for g in range(N_KV_HEADS):
        qg = jnp.concatenate(
            [qkv[:, (g * Q_PER_KV + r) * HEAD_DIM:(g * Q_PER_KV + r + 1) * HEAD_DIM]
             for r in range(Q_PER_KV)], axis=0).astype(BF16)
        kc = ck_ref[pl.ds(g, n_past, stride=N_KV_HEADS), :].astype(BF16)
        vcs.append(cv_ref[pl.ds(g, n_past, stride=N_KV_HEADS), :].astype(BF16))
        kn = k_new[:, g * HEAD_DIM:(g + 1) * HEAD_DIM].astype(BF16).astype(F32)
        vns.append(v_new[:, g * HEAD_DIM:(g + 1) * HEAD_DIM].astype(BF16).astype(F32))
        s_c.append(_dot_nt(qg, kc))
        qf = qg.astype(F32)
        for j in range(t_new):
            s_n[j].append(jnp.sum(qf * kn[j:j + 1, :], axis=1, keepdims=True))
    s_c = jnp.concatenate(s_c, axis=0) * ATT_SCALE
    s_n = [jnp.concatenate(x, axis=0) * ATT_SCALE for x in s_n]

    cidx = lax.broadcasted_iota(jnp.int32, (nall, n_past), 1)
    tok = _imod(lax.broadcasted_iota(jnp.int32, (nall, n_past), 0), t_new)
    dist_c = n_past + tok - cidx
    tok1 = _imod(lax.broadcasted_iota(jnp.int32, (nall, 1), 0), t_new)

    probs, new_probs, lses = [], [], []
    for win, dil in DILATED_BRANCHES:
        assert win <= n_past
        mask_c = (_imod(dist_c, dil) == 0) & (dist_c <= win)
        sc = jnp.where(mask_c, s_c, NEG_INF)
        m = jnp.max(sc, axis=1, keepdims=True)
        sn = []
        for j in range(t_new):
            dn = tok1 - j
            mask_n = (dn >= 0) & (_imod(dn, dil) == 0)
            snj = jnp.where(mask_n, s_n[j], NEG_INF)
            sn.append(snj)
            m = jnp.maximum(m, snj)
        ec = jnp.exp(sc - m)
        en = [jnp.exp(x - m) for x in sn]
        ssum = jnp.sum(ec, axis=1, keepdims=True)
        for x in en:
            ssum = ssum + x
        probs.append((ec / ssum).astype(BF16))
        new_probs.append([(x / ssum).astype(BF16).astype(F32) for x in en])
        lses.append(m + jnp.log(ssum))

    mx = functools.reduce(jnp.maximum, lses)
    ws = [jnp.exp(l - mx) for l in lses]
    den = functools.reduce(lambda a, b: a + b, ws)
    for g in range(N_KV_HEADS):
        rs = slice(g * nq, (g + 1) * nq)
        o_win = _dot(jnp.concatenate([p[rs, :] for p in probs], axis=0), vcs[g])
        o = jnp.zeros((nq, HEAD_DIM), F32)
        for i, wgt in enumerate(ws):
            o_br = o_win[i * nq:(i + 1) * nq, :]
            for j in range(t_new):
                o_br = o_br + new_probs[i][j][rs, :] * vns[g][j:j + 1, :]
            o = o + wgt[rs, :] * o_br
        o = o / den[rs, :]
        for r in range(Q_PER_KV):
            att_ref[:, (g * Q_PER_KV + r) * HEAD_DIM:(g * Q_PER_KV + r + 1) * HEAD_DIM] = (
                o[r * t_new:(r + 1) * t_new, :])


def _attn_kernel(*refs, n_past, t_new, aliased, seq, steps_per_unit):
    if aliased:
        (qkv_ref, ck_ref, cv_ref, q_ref, k_ref, v_ref, _, _,
         att_s_ref, nk_ref, nv_ref, att_p_ref, *scr) = refs
    else:
        (qkv_ref, ck_ref, cv_ref, q_ref, k_ref, v_ref,
         att_s_ref, nk_ref, nv_ref, att_p_ref, *scr) = refs
    _sample_attn_step(qkv_ref, ck_ref, cv_ref, att_s_ref, nk_ref, nv_ref, n_past=n_past, t_new=t_new)
    w = _imod(pl.program_id(0), steps_per_unit)
    _prompt_attn_step(q_ref, k_ref, v_ref, att_p_ref, scr, w, seq=seq, steps_per_unit=steps_per_unit)


def _attn(qkv_s, cache_k, cache_v, proj, layer, prev_k, prev_v, *, batch, seq):
    depth, bsz, nrow, _ = cache_k.shape
    n_past = nrow // N_KV_HEADS
    t_new = qkv_s.shape[1]
    units = batch * N_Q_HEADS
    steps_per_unit = bsz // units
    assert bsz == units * steps_per_unit
    aliased = prev_k is not None
    kern = functools.partial(_attn_kernel, n_past=n_past, t_new=t_new, aliased=aliased, seq=seq,
                             steps_per_unit=steps_per_unit)
    kq = COL_K // HEAD_DIM
    vq = COL_V // HEAD_DIM

    def unit(i):
        u = i // steps_per_unit
        return u // N_Q_HEADS, u % N_Q_HEADS

    win_spec = pl.BlockSpec((None, None, nrow, HEAD_DIM), lambda i: (layer, i, 0, 0))
    in_specs = [
        pl.BlockSpec((None, t_new, qkv_s.shape[2]), lambda i: (i, 0, 0)), win_spec, win_spec,
        pl.BlockSpec((seq, HEAD_DIM), lambda i: unit(i)),
        pl.BlockSpec((seq, HEAD_DIM), lambda i: (unit(i)[0], kq + unit(i)[1] // Q_PER_KV)),
        pl.BlockSpec((seq, HEAD_DIM), lambda i: (unit(i)[0], vq + unit(i)[1] // Q_PER_KV)),
    ]
    args = [qkv_s, cache_k, cache_v, proj, proj, proj]
    aliases = {}
    if aliased:
        in_specs += [pl.BlockSpec(memory_space=pl.ANY), pl.BlockSpec(memory_space=pl.ANY)]
        args += [prev_k, prev_v]
        aliases = {6: 1, 7: 2}
    n_scr = len(DILATED_BRANCHES) * 2
    return pl.pallas_call(
        kern,
        out_shape=(jax.ShapeDtypeStruct((bsz, t_new, D_ATT), F32),
                   jax.ShapeDtypeStruct(cache_k.shape, cache_k.dtype),
                   jax.ShapeDtypeStruct(cache_v.shape, cache_v.dtype),
                   jax.ShapeDtypeStruct((batch * seq, D_ATT), BF16)),
        grid=(bsz,),
        in_specs=in_specs,
        out_specs=(pl.BlockSpec((None, t_new, D_ATT), lambda i: (i, 0, 0)), win_spec, win_spec,
                   pl.BlockSpec((seq, HEAD_DIM), lambda i: unit(i))),
        scratch_shapes=[pltpu.VMEM((ATT_SPAN, HEAD_DIM), F32) for _ in range(n_scr)],
        input_output_aliases=aliases,
        compiler_params=pltpu.CompilerParams(dimension_semantics=("arbitrary",),
                                             vmem_limit_bytes=ATTN_VMEM_LIMIT_BYTES),
        name="attn",
    )(*args)


def _expand_mats():
    r64 = np.zeros((LANES, D_SSM), np.float32)
    r128 = np.zeros((LANES, N_SSM_HEADS * LANES), np.float32)
    for h in range(N_SSM_HEADS):
        r64[h, h * SSM_HEAD_DIM:(h + 1) * SSM_HEAD_DIM] = 1.0
        r128[h, h * LANES:(h + 1) * LANES] = 1.0
    return jnp.asarray(r64, BF16), jnp.asarray(r128, BF16)


def _group_sum_mat():
    g = np.zeros((D_BC, D_SSM), np.float32)
    for grp in range(N_SSM_GROUPS):
        g[grp * D_STATE:(grp + 1) * D_STATE,
          grp * HEADS_PER_GROUP * SSM_HEAD_DIM:(grp + 1) * HEADS_PER_GROUP * SSM_HEAD_DIM] = 1.0
    return jnp.asarray(g, BF16)


def _conv_silu(x, tail, cw, cb):
    n = x.shape[0]
    xp = jnp.concatenate([tail, x], axis=0)
    out = cb + cw[3:4, :] * x
    for w in range(CONV_WIDTH - 1):
        off = SUBLANES - (CONV_WIDTH - 1) + w
        out = out + cw[w:w + 1, :] * xp[off:off + n, :]
    return _silu(out)


def _cumsum_rows(x, seg=None):
    n = x.shape[0]
    rows = lax.broadcasted_iota(jnp.int32, x.shape, 0)
    pos = rows if seg is None else _imod(rows, seg)
    limit = n if seg is None else seg
    sh = 1
    while sh < limit:
        x = x + jnp.where(pos >= sh, pltpu.roll(x, sh, axis=0), 0.0)
        sh *= 2
    return x


def _lane_col_block(row):
    return jnp.broadcast_to(row, (LANES, LANES)).T


def _ssd_chunk(z, xraw, bcraw, dt_raw, tail, h_prev, cw, cb, dtb, alog, dsk, ng, r64, r128):
    cs = SSD_CHUNK
    xs = _conv_silu(xraw, tail[:, :D_SSM], cw[:, :D_SSM], cb[:, :D_SSM])
    bcm = _conv_silu(bcraw, tail[:, D_SSM:], cw[:, D_SSM:], cb[:, D_SSM:])

    dt = _softplus(dt_raw + dtb)
    a = -jnp.exp(alog)
    acum = _cumsum_rows(dt * a)
    acum_t = acum.T
    dt_e = _expand(dt, r64)
    ac_e = _expand(acum, r64)
    col_b = _expand(acum, r128)
    last_e = ac_e[cs - 1:cs, :]
    ea_e = jnp.exp(ac_e)
    xdt = xs * dt_e
    xte = (xdt * jnp.exp(last_e - ac_e)).astype(BF16)

    ti = lax.broadcasted_iota(jnp.int32, (cs, cs), 0)
    si = lax.broadcasted_iota(jnp.int32, (cs, cs), 1)
    causal = ti >= si
    lane = lax.broadcasted_iota(jnp.int32, (cs, LANES), 1)
    lo_half = lane < SSM_HEAD_DIM

    gw = HEADS_PER_GROUP * SSM_HEAD_DIM
    y_diag, y_off, states = [], [], []
    for g in range(N_SSM_GROUPS):
        bg = bcm[:, g * D_STATE:(g + 1) * D_STATE].astype(BF16)
        cg = bcm[:, D_BC + g * D_STATE:D_BC + (g + 1) * D_STATE].astype(BF16)
        cbt = _dot_nt(cg, bg)
        h_in = h_prev[g * gw:(g + 1) * gw, :]
        y_off.append(_dot_nt(cg, h_in.astype(BF16)))
        states.append(lax.dot_general(xte[:, g * gw:(g + 1) * gw], bg, TN_DIMS,
                                      preferred_element_type=F32))
        for k in range(HEADS_PER_GROUP // 2):
            pair = g * (HEADS_PER_GROUP // 2) + k
            xp = xdt[:, pair * LANES:(pair + 1) * LANES]
            yd = jnp.zeros((cs, LANES), F32)
            for e in range(2):
                h = 2 * pair + e
                seg = col_b[:, h * LANES:(h + 1) * LANES] - jnp.broadcast_to(acum_t[h:h + 1, :], (cs, cs))
                dec = jnp.exp(jnp.where(causal, seg, NEG_INF))
                cbh = (cbt * dec).astype(BF16)
                xh = jnp.where(lo_half if e == 0 else jnp.logical_not(lo_half), xp, 0.0).astype(BF16)
                yd = yd + _dot(cbh, xh)
            y_diag.append(yd)

    y = (jnp.concatenate(y_diag, axis=1) + jnp.concatenate(y_off, axis=1) * ea_e + dsk * xs)
    dec_rows = jnp.exp(jnp.concatenate(
        [_lane_col_block(last_e[:, k * LANES:(k + 1) * LANES]) for k in range(D_SSM // LANES)],
        axis=0))
    h_new = h_prev * dec_rows + jnp.concatenate(states, axis=0)
    return _rms(y * _silu(z), ng).astype(BF16), h_new


SSD_CHUNKS_PER_STEP = 4


def _ssd_prompt_kernel(z_ref, xs_ref, bc_ref, dt_ref, cw_ref, cb_ref, dtb_ref, alog_ref, dsk_ref,
                       ng_ref, r64_ref, r128_ref, y_ref, st_ref, cv_ref, h_scr, tail_scr):
    c = pl.program_id(1)
    last = pl.num_programs(1) - 1
    cs = SSD_CHUNK

    @pl.when(c == 0)
    def _():
        h_scr[...] = jnp.zeros_like(h_scr)
        tail_scr[...] = jnp.zeros_like(tail_scr)

    consts = (cw_ref[...], cb_ref[...], dtb_ref[...], alog_ref[...], dsk_ref[...], ng_ref[...],
              r64_ref[...], r128_ref[...])
    h = h_scr[...]
    tail = tail_scr[...]
    for k in range(SSD_CHUNKS_PER_STEP):
        rs = slice(k * cs, (k + 1) * cs)
        xraw = xs_ref[rs, :]
        bcraw = bc_ref[rs, :]
        y, h = _ssd_chunk(z_ref[rs, :], xraw, bcraw, dt_ref[rs, :], tail, h, *consts)
        y_ref[rs, :] = y
        tail = jnp.concatenate([xraw[cs - SUBLANES:, :], bcraw[cs - SUBLANES:, :]], axis=1)
    h_scr[...] = h
    tail_scr[...] = tail

    @pl.when(c == last)
    def _():
        st_ref[...] = h
        cv_ref[...] = tail[SUBLANES - (CONV_WIDTH - 1):, :]


def _ssd_prompt(proj, dt_all, layer, conv_w, conv_b3, dtb, alog, dsk_e, ng3, r64, r128, batch, seq):
    cs = SSD_CHUNK * SSD_CHUNKS_PER_STEP
    assert seq % cs == 0
    nc = seq // cs
    zc, xc, bcc = COL_Z // D_SSM, COL_X // D_SSM, COL_BC // D_SSM
    const2 = lambda b, c: (0, 0)
    return pl.pallas_call(
        _ssd_prompt_kernel,
        out_shape=(jax.ShapeDtypeStruct((batch * seq, D_SSM), BF16),
                   jax.ShapeDtypeStruct((batch, D_SSM, D_STATE), F32),
                   jax.ShapeDtypeStruct((batch, CONV_WIDTH - 1, CONV_DIM), F32)),
        grid=(batch, nc),
        in_specs=[
            pl.BlockSpec((cs, D_SSM), lambda b, c: (b * nc + c, zc)),
            pl.BlockSpec((cs, D_SSM), lambda b, c: (b * nc + c, xc)),
            pl.BlockSpec((cs, D_SSM), lambda b, c: (b * nc + c, bcc)),
            pl.BlockSpec((cs, LANES), lambda b, c: (b * nc + c, 0)),
            pl.BlockSpec((None, CONV_WIDTH, CONV_DIM), lambda b, c: (layer, 0, 0)),
            pl.BlockSpec((None, 1, CONV_DIM), lambda b, c: (layer, 0, 0)),
            pl.BlockSpec((None, 1, LANES), lambda b, c: (layer, 0, 0)),
            pl.BlockSpec((None, 1, LANES), lambda b, c: (layer, 0, 0)),
            pl.BlockSpec((None, 1, D_SSM), lambda b, c: (layer, 0, 0)),
            pl.BlockSpec((None, 1, D_SSM), lambda b, c: (layer, 0, 0)),
            pl.BlockSpec(r64.shape, const2),
            pl.BlockSpec(r128.shape, const2),
        ],
        out_specs=(pl.BlockSpec((cs, D_SSM), lambda b, c: (b * nc + c, 0)),
                   pl.BlockSpec((None, D_SSM, D_STATE), lambda b, c: (b, 0, 0)),
                   pl.BlockSpec((None, CONV_WIDTH - 1, CONV_DIM), lambda b, c: (b, 0, 0))),
        scratch_shapes=[pltpu.VMEM((D_SSM, D_STATE), F32), pltpu.VMEM((SUBLANES, CONV_DIM), F32)],
        compiler_params=_params(("parallel", "arbitrary")),
        name="ssd_prompt",
    )(proj, proj, proj, dt_all, conv_w, conv_b3, dtb, alog, dsk_e, ng3, r64, r128)


def _ssd_sample_kernel(*refs, bt, t_new, aliased):
    if aliased:
        (z_ref, xs_ref, bc_ref, dt_ref, cst_ref, h0_ref, cw_ref, cb_ref, dtb_ref, alog_ref, dsk_ref,
         ng_ref, r64_ref, gs_ref, _, _, y_ref, st_ref, cv_ref, xbc_scr) = refs
    else:
        (z_ref, xs_ref, bc_ref, dt_ref, cst_ref, h0_ref, cw_ref, cb_ref, dtb_ref, alog_ref, dsk_ref,
         ng_ref, r64_ref, gs_ref, y_ref, st_ref, cv_ref, xbc_scr) = refs
    rows = bt * t_new
    kw = CONV_WIDTH - 1
    cw = cw_ref[...]
    cb = cb_ref[...]

    for b in range(bt):
        xb = jnp.concatenate([xs_ref[b * t_new:(b + 1) * t_new, :],
                              bc_ref[b * t_new:(b + 1) * t_new, :]], axis=1)
        xp = jnp.concatenate([cst_ref[b], xb], axis=0)
        out = cb
        for w in range(CONV_WIDTH):
            out = out + cw[w:w + 1, :] * xp[w:w + t_new, :]
        xbc_scr[b * t_new:(b + 1) * t_new, :] = _silu(out)
        cv_ref[b] = xp[t_new:t_new + kw, :]

    xbc = xbc_scr[...]
    xs = xbc[:, :D_SSM]
    bm = xbc[:, D_SSM:D_SSM + D_BC]
    cm = xbc[:, D_SSM + D_BC:]

    tpos = _imod(lax.broadcasted_iota(jnp.int32, (rows, 1), 0), t_new)
    dt = _softplus(dt_ref[...] + dtb_ref[...])
    a = -jnp.exp(alog_ref[...])
    acum = _cumsum_rows(dt * a, seg=t_new)
    r64 = r64_ref[...]
    dt_e = _expand(dt, r64)
    ac_e = _expand(acum, r64)
    v = jnp.where(tpos == t_new - 1, ac_e, 0.0)
    last_e = v
    for d in range(1, t_new):
        last_e = last_e + pltpu.roll(v, rows - d, axis=0)
    ea_e = jnp.exp(ac_e)
    xdt = xs * dt_e
    xte = xdt * jnp.exp(last_e - ac_e)

    cmb = cm.astype(BF16).astype(F32)
    bmb = bm.astype(BF16).astype(F32)
    gs = gs_ref[...]
    y = dsk_ref[...] * xs
    for d in range(t_new):
        if d == 0:
            b_s, x_s, a_s = bmb, xdt, ac_e
        else:
            b_s = pltpu.roll(bmb, d, axis=0)
            x_s = pltpu.roll(xdt, d, axis=0)
            a_s = pltpu.roll(ac_e, d, axis=0)
        cb_e = _expand(cmb * b_s, gs)
        dec = jnp.exp(jnp.where(tpos >= d, ac_e - a_s, NEG_INF))
        y = y + cb_e * dec * x_s

    gw = HEADS_PER_GROUP * SSM_HEAD_DIM
    pad = jnp.zeros((LANES - rows, LANES), F32) if rows < LANES else None
    rowb = _idiv(lax.broadcasted_iota(jnp.int32, (rows, 1), 0), t_new)
    colb = _idiv(lax.broadcasted_iota(jnp.int32, (1, LANES), 1), t_new)
    xte_t = []
    for k in range(D_SSM // LANES):
        blk = xte[:, k * LANES:(k + 1) * LANES]
        if pad is not None:
            blk = jnp.concatenate([blk, pad], axis=0)
        xte_t.append(blk.T)
    e_last = jnp.exp(last_e)
    y_off = [jnp.zeros((rows, gw), F32) for _ in range(N_SSM_GROUPS)]
    for b in range(bt):
        h0 = h0_ref[b]
        new_rows = []
        for g in range(N_SSM_GROUPS):
            cg = cm[:, g * D_STATE:(g + 1) * D_STATE]
            bg = bm[:, g * D_STATE:(g + 1) * D_STATE]
            if pad is not None:
                bg = jnp.concatenate([bg, pad], axis=0)
            bg = bg.astype(BF16)
            h0g = h0[g * gw:(g + 1) * gw, :]
            cgb = jnp.where(rowb == b, cg, 0.0).astype(BF16)
            y_off[g] = y_off[g] + _dot_nt(cgb, h0g.astype(BF16))
            for k in range(gw // LANES):
                blk = g * (gw // LANES) + k
                lhs = jnp.where(colb == b, xte_t[blk], 0.0).astype(BF16)
                st = _dot(lhs, bg)
                r = b * t_new + t_new - 1
                dec = _lane_col_block(e_last[r:r + 1, blk * LANES:(blk + 1) * LANES])
                new_rows.append(h0[blk * LANES:(blk + 1) * LANES, :] * dec + st)
        st_ref[b] = jnp.concatenate(new_rows, axis=0)

    y = y + jnp.concatenate(y_off, axis=1) * ea_e
    y_ref[...] = _rms(y * _silu(z_ref[...]), ng_ref[...]).astype(y_ref.dtype)


def _ssd_sample(proj_s, dt_s, state_conv, state_ssm, layer, conv_w, conv_b3, dtb, alog, dsk_e, ng3,
                r64, gsum, prev_st, prev_cv, *, bt, t_new):
    depth, bsz = state_ssm.shape[:2]
    rows = bt * t_new
    zc, xc, bcc = COL_Z // D_SSM, COL_X // D_SSM, COL_BC // D_SSM
    aliased = prev_st is not None
    kern = functools.partial(_ssd_sample_kernel, bt=bt, t_new=t_new, aliased=aliased)
    const2 = lambda i: (0, 0)
    in_specs = [
        pl.BlockSpec((rows, D_SSM), lambda i: (i, zc)),
        pl.BlockSpec((rows, D_SSM), lambda i: (i, xc)),
        pl.BlockSpec((rows, D_SSM), lambda i: (i, bcc)),
        pl.BlockSpec((rows, LANES), lambda i: (i, 0)),
        pl.BlockSpec((None, bt, CONV_WIDTH - 1, CONV_DIM), lambda i: (layer, i, 0, 0)),
        pl.BlockSpec((None, bt, D_SSM, D_STATE), lambda i: (layer, i, 0, 0)),
        pl.BlockSpec((None, CONV_WIDTH, CONV_DIM), lambda i: (layer, 0, 0)),
        pl.BlockSpec((None, 1, CONV_DIM), lambda i: (layer, 0, 0)),
        pl.BlockSpec((None, 1, LANES), lambda i: (layer, 0, 0)),
        pl.BlockSpec((None, 1, LANES), lambda i: (layer, 0, 0)),
        pl.BlockSpec((None, 1, D_SSM), lambda i: (layer, 0, 0)),
        pl.BlockSpec((None, 1, D_SSM), lambda i: (layer, 0, 0)),
        pl.BlockSpec(r64.shape, const2),
        pl.BlockSpec(gsum.shape, const2),
    ]
    args = [proj_s, proj_s, proj_s, dt_s, state_conv, state_ssm, conv_w, conv_b3, dtb, alog, dsk_e,
            ng3, r64, gsum]
    aliases = {}
    if aliased:
        in_specs += [pl.BlockSpec(memory_space=pl.ANY), pl.BlockSpec(memory_space=pl.ANY)]
        args += [prev_st, prev_cv]
        aliases = {14: 1, 15: 2}
    return pl.pallas_call(
        kern,
        out_shape=(jax.ShapeDtypeStruct((bsz * t_new, D_SSM), BF16),
                   jax.ShapeDtypeStruct(state_ssm.shape, state_ssm.dtype),
                   jax.ShapeDtypeStruct(state_conv.shape, state_conv.dtype)),
        grid=(bsz // bt,),
        in_specs=in_specs,
        out_specs=(pl.BlockSpec((rows, D_SSM), lambda i: (i, 0)),
                   pl.BlockSpec((None, bt, D_SSM, D_STATE), lambda i: (layer, i, 0, 0)),
                   pl.BlockSpec((None, bt, CONV_WIDTH - 1, CONV_DIM), lambda i: (layer, i, 0, 0))),
        scratch_shapes=[pltpu.VMEM((rows, CONV_DIM), F32)],
        input_output_aliases=aliases,
        compiler_params=_params(("parallel",)),
        name="ssd_sample",
    )(*args)


def _resident(shape, index_map):
    return pl.BlockSpec(shape, index_map, pipeline_mode=pl.Buffered(1))


def _out_proj_kernel(att_ref, ssm_ref, h_ref, w_ref, o_ref, wb_ref):
    @pl.when(pl.program_id(0) == 0)
    def _():
        wb_ref[...] = w_ref[...].astype(BF16)

    acc = _dot(att_ref[...], wb_ref[:D_ATT, :]) + _dot(ssm_ref[...], wb_ref[D_ATT:, :])
    o_ref[...] = h_ref[...] + acc


def _out_proj(att, ssm, h, w_out, layer):
    t, d = h.shape
    tm = _row_tile(t, 512)
    return pl.pallas_call(
        _out_proj_kernel,
        out_shape=jax.ShapeDtypeStruct((t, d), F32),
        grid=(t // tm,),
        in_specs=[
            pl.BlockSpec((tm, D_ATT), lambda m: (m, 0)),
            pl.BlockSpec((tm, D_SSM), lambda m: (m, 0)),
            pl.BlockSpec((tm, d), lambda m: (m, 0)),
            _resident((None,) + w_out.shape[1:], lambda m: (layer, 0, 0)),
        ],
        out_specs=pl.BlockSpec((tm, d), lambda m: (m, 0)),
        scratch_shapes=[pltpu.VMEM(w_out.shape[1:], BF16)],
        compiler_params=_params(("arbitrary",)),
        name="out_proj",
    )(att, ssm, h, w_out)


def _ffn_kernel(h_ref, g_ref, wg_ref, wu_ref, wd_ref, o_ref, hf_ref):
    f = pl.program_id(1)

    @pl.when(f == 0)
    def _():
        h = h_ref[...]
        hf_ref[...] = _rms(h, g_ref[...]).astype(BF16)
        o_ref[...] = h

    hf = hf_ref[...]
    act = _silu(_dot(hf, wg_ref[...].astype(BF16))) * _dot(hf, wu_ref[...].astype(BF16))
    o_ref[...] += _dot(act.astype(BF16), wd_ref[...].astype(BF16))


def _ffn(h, g, wg, wu, wd, layer):
    t, d = h.shape
    dff = wg.shape[2]
    tm = _row_tile(t, 1024)
    tf = 256
    assert dff % tf == 0
    return pl.pallas_call(
        _ffn_kernel,
        out_shape=jax.ShapeDtypeStruct((t, d), F32),
        grid=(t // tm, dff // tf),
        in_specs=[
            pl.BlockSpec((tm, d), lambda m, f: (m, 0), pipeline_mode=pl.Buffered(1)),
            pl.BlockSpec((1, d), lambda m, f: (0, 0)),
            pl.BlockSpec((None, d, tf), lambda m, f: (layer, 0, f)),
            pl.BlockSpec((None, d, tf), lambda m, f: (layer, 0, f)),
            pl.BlockSpec((None, tf, d), lambda m, f: (layer, f, 0)),
        ],
        out_specs=pl.BlockSpec((tm, d), lambda m, f: (m, 0)),
        scratch_shapes=[pltpu.VMEM((tm, d), BF16)],
        compiler_params=_params(("parallel", "arbitrary")),
        name="ffn",
    )(h, g, wg, wu, wd)


def _ple_kernel(h_ref, p_ref, g_ref, wg_ref, wp_ref, gf_ref, o_ref, wgb_ref, wpb_ref, *, final, tn):
    @pl.when(pl.program_id(0) == 0)
    def _():
        wgb_ref[...] = wg_ref[...].astype(BF16)
        wpb_ref[...] = wp_ref[...].astype(BF16)

    hn = _rms(h_ref[...], g_ref[...]).astype(BF16)
    pb = p_ref[...].astype(BF16)
    for c in range(h_ref.shape[1] // tn):
        cols = slice(c * tn, (c + 1) * tn)
        gate = jax.nn.sigmoid(_dot(hn, wgb_ref[:, cols]))
        o_ref[:, cols] = h_ref[:, cols] + gate * _dot(pb, wpb_ref[:, cols])
    if final:
        o_ref[...] = _rms(o_ref[...], gf_ref[...])


def _ple(h, p, g, wg, wp, gf, layer, *, final):
    t, d = h.shape
    tm = _row_tile(t, 512)
    kern = functools.partial(_ple_kernel, final=final, tn=512)
    return pl.pallas_call(
        kern,
        out_shape=jax.ShapeDtypeStruct((t, d), F32),
        grid=(t // tm,),
        in_specs=[
            pl.BlockSpec((tm, d), lambda m: (m, 0)),
            pl.BlockSpec((None, tm, p.shape[2]), lambda m: (layer, m, 0)),
            pl.BlockSpec((1, d), lambda m: (0, 0)),
            _resident((None,) + wg.shape[1:], lambda m: (layer, 0, 0)),
            _resident((None,) + wp.shape[1:], lambda m: (layer, 0, 0)),
            pl.BlockSpec((1, d), lambda m: (0, 0)),
        ],
        out_specs=pl.BlockSpec((tm, d), lambda m: (m, 0)),
        scratch_shapes=[pltpu.VMEM(wg.shape[1:], BF16), pltpu.VMEM(wp.shape[1:], BF16)],
        compiler_params=_params(("arbitrary",)),
        name="ple",
    )(h, p, g, wg, wp, gf)


def _rope_tables(pos):
    half = HEAD_DIM // 2
    inv_freq = ROPE_THETA ** (-jnp.arange(half, dtype=F32) / half)
    ang = pos.astype(F32)[:, None] * inv_freq[None, :]
    cos, sin = jnp.cos(ang), jnp.sin(ang)
    return jnp.concatenate([cos, cos], axis=1), jnp.concatenate([-sin, sin], axis=1)


def _pad_lanes(x):
    return jnp.pad(x, [(0, 0)] * (x.ndim - 1) + [(0, LANES - x.shape[-1])])


def kernel(x_prompt, x_sample, cache_k, cache_v, state_ssm, state_conv, p_prompt, p_sample,
           norm_mix_g, w_in, conv_w, conv_b, dt_bias, a_log, d_skip, ssm_norm_g, w_out,
           norm_ffn_g, w_ffn_gate, w_ffn_up, w_ffn_down, norm_ple_g, w_ple_gate, w_ple_proj,
           final_norm_g):
    batch, seq, d = x_prompt.shape
    dec_batch, dec_seq, _ = x_sample.shape
    depth = w_in.shape[0]
    n_past = cache_k.shape[2]
    tp = batch * seq
    ts = dec_batch * dec_seq
    assert w_in.shape[2] == PROJ_COLS + N_SSM_HEADS and n_past == PAST_LEN

    cos_p, sin_p = _rope_tables(jnp.arange(seq, dtype=jnp.int32))
    cos_s, sin_s = _rope_tables(jnp.tile(PAST_LEN + jnp.arange(dec_seq, dtype=jnp.int32), dec_batch))
    r64, r128 = _expand_mats()
    w_in_t = jnp.swapaxes(w_in, 1, 2)
    gsum = _group_sum_mat()

    ck = cache_k.reshape(depth, dec_batch, n_past * N_KV_HEADS, HEAD_DIM)
    cv = cache_v.reshape(depth, dec_batch, n_past * N_KV_HEADS, HEAD_DIM)
    st_in = state_ssm.reshape(depth, dec_batch, D_SSM, D_STATE)
    pp = p_prompt.reshape(depth, tp, -1)
    ps = p_sample.reshape(depth, ts, -1)
    conv_b3 = conv_b.reshape(depth, 1, CONV_DIM)
    dtb = _pad_lanes(dt_bias).reshape(depth, 1, LANES)
    alog = _pad_lanes(a_log).reshape(depth, 1, LANES)
    dsk_e = jnp.repeat(d_skip, SSM_HEAD_DIM, axis=1).reshape(depth, 1, D_SSM)
    ng3 = ssm_norm_g.reshape(depth, 1, D_SSM)
    gf = final_norm_g.reshape(1, d)

    hp = x_prompt.reshape(tp, d)
    hs = x_sample.reshape(ts, d)
    nk_s = nv_s = st_s = cv_s = None
    k_p, v_p, st_p, cv_p = [], [], [], []
    for i in range(depth):
        g_mix = norm_mix_g[i].reshape(1, d)
        g_ffn = norm_ffn_g[i].reshape(1, d)
        g_ple = norm_ple_g[i].reshape(1, d)
        last = i == depth - 1

        keep = min(ATT_WINDOW, seq)
        proj, dt_p, k_i, v_i = _in_proj(hp, g_mix, w_in_t, i, cos_p, sin_p, keep=keep)
        proj_s, dt_s = _in_proj(hs, g_mix, w_in_t, i, cos_s, sin_s)
        qkv_s = proj_s[:, :COL_Z].reshape(dec_batch, dec_seq, COL_Z)
        att_s, nk_s, nv_s, att_p = _attn(qkv_s, ck, cv, proj, i, nk_s, nv_s, batch=batch, seq=seq)

        y_p, st_i, cv_i = _ssd_prompt(proj, dt_p, i, conv_w, conv_b3, dtb, alog, dsk_e, ng3,
                                      r64, r128, batch, seq)
        hp = _out_proj(att_p, y_p, hp, w_out, i)
        hp = _ffn(hp, g_ffn, w_ffn_gate, w_ffn_up, w_ffn_down, i)
        hp = _ple(hp, pp, g_ple, w_ple_gate, w_ple_proj, gf, i, final=last)

        k_p.append(k_i.reshape(batch, keep, N_KV_HEADS, HEAD_DIM))
        v_p.append(v_i.reshape(batch, keep, N_KV_HEADS, HEAD_DIM))
        st_p.append(st_i.reshape(batch, N_SSM_HEADS, SSM_HEAD_DIM, D_STATE))
        cv_p.append(cv_i)

        y_s, st_s, cv_s = _ssd_sample(proj_s, dt_s, state_conv, st_in, i, conv_w, conv_b3,
                                      dtb, alog, dsk_e, ng3, r64, gsum, st_s, cv_s,
                                      bt=16, t_new=dec_seq)
        hs = _out_proj(att_s.reshape(ts, D_ATT).astype(BF16), y_s, hs, w_out, i)
        hs = _ffn(hs, g_ffn, w_ffn_gate, w_ffn_up, w_ffn_down, i)
        hs = _ple(hs, ps, g_ple, w_ple_gate, w_ple_proj, gf, i, final=last)

    return (hp.reshape(batch, seq, d), hs.reshape(dec_batch, dec_seq, d),
            jnp.stack(k_p), jnp.stack(v_p), jnp.stack(st_p), jnp.stack(cv_p),
            nk_s.reshape(cache_k.shape), nv_s.reshape(cache_v.shape),
            st_s.reshape(state_ssm.shape), cv_s)
```

```python
import functools

import numpy as np
import jax
import jax.numpy as jnp
from jax import lax
from jax.experimental import pallas as pl
from jax.experimental.pallas import tpu as pltpu

F32 = jnp.float32
BF16 = jnp.bfloat16

N_Q_HEADS = 8
N_KV_HEADS = 4
Q_PER_KV = N_Q_HEADS // N_KV_HEADS
HEAD_DIM = 128
D_ATT = N_Q_HEADS * HEAD_DIM
D_KV = N_KV_HEADS * HEAD_DIM
DILATED_BRANCHES = ((128, 1), (512, 4), (2048, 16))
ATT_WINDOW = 2048
ROPE_THETA = 10000.0
ATT_SCALE = HEAD_DIM ** -0.5
N_SSM_HEADS = 16
SSM_HEAD_DIM = 64
D_SSM = N_SSM_HEADS * SSM_HEAD_DIM
N_SSM_GROUPS = 4
HEADS_PER_GROUP = N_SSM_HEADS // N_SSM_GROUPS
D_STATE = 128
D_BC = N_SSM_GROUPS * D_STATE
CONV_WIDTH = 4
CONV_DIM = D_SSM + 2 * D_BC
SSD_CHUNK = 128
PAST_LEN = 2048
EPS = 1e-6
NEG_INF = -1e30

LANES = 128
SUBLANES = 8
MIB = 1024 * 1024
VMEM_LIMIT_BYTES = 56 * MIB
ATTN_VMEM_LIMIT_BYTES = 60 * MIB

ROWS_WEIGHT_STREAM = 1024
ROWS_WEIGHT_RESIDENT = 512
IN_PROJ_COLS = 512
FFN_COLS = 256
FFN_COLS_FEW_ROWS = 512
PLE_COLS = 512
SSD_SAMPLE_SEQS = 16
COPY_ROWS = 1024
MERGE_ROWS = 256

COL_Q = 0
COL_K = D_ATT
COL_V = D_ATT + D_KV
COL_Z = D_ATT + 2 * D_KV
COL_X = COL_Z + D_SSM
COL_BC = COL_X + D_SSM
PROJ_COLS = COL_BC + 2 * D_BC

NT_DIMS = (((1,), (1,)), ((), ()))
TN_DIMS = (((0,), (0,)), ((), ()))


def _dot(a, b):
    return jnp.dot(a, b, preferred_element_type=F32)


def _dot_nt(a, b):
    return lax.dot_general(a, b, NT_DIMS, preferred_element_type=F32)


def _rms(x, g):
    return x * lax.rsqrt(jnp.mean(x * x, axis=-1, keepdims=True) + EPS) * g


def _silu(x):
    return x * jax.nn.sigmoid(x)


def _softplus(x):
    return jnp.maximum(x, 0.0) + jnp.log1p(jnp.exp(-jnp.abs(x)))


def _expand(x, r):
    hi = x.astype(BF16)
    r1 = x - hi.astype(F32)
    mid = r1.astype(BF16)
    lo = (r1 - mid.astype(F32)).astype(BF16)
    return _dot(hi, r) + _dot(mid, r) + _dot(lo, r)


def _imod(x, n):
    assert n & (n - 1) == 0, "power-of-two divisor expected"
    return x & (n - 1)


def _idiv(x, n):
    assert n & (n - 1) == 0, "power-of-two divisor expected"
    return x >> (n.bit_length() - 1)


def _params(sem):
    return pltpu.CompilerParams(dimension_semantics=sem, vmem_limit_bytes=VMEM_LIMIT_BYTES)


def _row_tile(rows, cap):
    tile = min(rows, cap)
    while rows % tile:
        tile -= SUBLANES
    return tile


def _in_proj_kernel(x_ref, g_ref, w_ref, wdt_ref, cos_ref, sin_ref, o_ref, dt_ref, *rest,
                    rot_tiles, kv_tiles, period_tiles, first_keep):
    if kv_tiles is None:
        (hn_ref,) = rest
    else:
        ok_ref, ov_ref, hn_ref = rest
    n = pl.program_id(1)

    @pl.when(n == 0)
    def _():
        hn = _rms(x_ref[...], g_ref[...]).astype(BF16)
        hn_ref[...] = hn
        row = lax.broadcasted_iota(jnp.int32, wdt_ref.shape, 0)
        wdt = jnp.where(row < N_SSM_HEADS, wdt_ref[...], 0.0).astype(BF16)
        dt_ref[...] = _dot_nt(hn, wdt)

    def project():
        return _dot_nt(hn_ref[...], w_ref[...].astype(BF16))

    @pl.when(n < rot_tiles)
    def _():
        acc = project()
        cos = cos_ref[...]
        sin = sin_ref[...]
        for hh in range(acc.shape[1] // HEAD_DIM):
            xh = acc[:, hh * HEAD_DIM:(hh + 1) * HEAD_DIM]
            o_ref[:, hh * HEAD_DIM:(hh + 1) * HEAD_DIM] = (
                xh * cos + pltpu.roll(xh, HEAD_DIM // 2, axis=1) * sin)

    @pl.when(n >= rot_tiles)
    def _():
        o_ref[...] = project()

    if kv_tiles is not None:
        in_keep = lax.rem(pl.program_id(0), period_tiles) >= first_keep
        rows = o_ref.shape[0]
        for tile, dst in zip(kv_tiles, (ok_ref, ov_ref)):
            @pl.when((n == tile) & in_keep)
            def _(dst=dst):
                for hh in range(N_KV_HEADS):
                    dst[pl.ds(hh, rows, stride=N_KV_HEADS), :] = (
                        o_ref[:, hh * HEAD_DIM:(hh + 1) * HEAD_DIM])


def _in_proj(h, g, w_in_t, layer, cos2, sin2, keep=None):
    t, d = h.shape
    tn = IN_PROJ_COLS
    period = cos2.shape[0]
    tm = _row_tile(period, ROWS_WEIGHT_STREAM)
    assert t % tm == 0 and PROJ_COLS % LANES == 0 and tn == D_KV
    period_tiles = period // tm
    dt_block = PROJ_COLS // LANES
    out_shape = [jax.ShapeDtypeStruct((t, PROJ_COLS), F32), jax.ShapeDtypeStruct((t, LANES), F32)]
    out_specs = [pl.BlockSpec((tm, tn), lambda m, n: (m, n)),
                 pl.BlockSpec((tm, LANES), lambda m, n: (m, 0))]
    kv_tiles, first_keep = None, 0
    if keep is not None:
        assert keep % tm == 0
        kv_tiles = (COL_K // tn, COL_V // tn)
        keep_tiles = keep // tm
        first_keep = period_tiles - keep_tiles

        def kept_block(m, n):
            return ((m // period_tiles) * keep_tiles
                    + jnp.maximum(m % period_tiles - first_keep, 0), 0)

        win_rows = (t // period) * keep * N_KV_HEADS
        out_shape += [jax.ShapeDtypeStruct((win_rows, HEAD_DIM), F32)] * 2
        out_specs += [pl.BlockSpec((tm * N_KV_HEADS, HEAD_DIM), kept_block)] * 2
    kern = functools.partial(_in_proj_kernel, rot_tiles=(D_ATT + D_KV) // tn, kv_tiles=kv_tiles,
                             period_tiles=period_tiles, first_keep=first_keep)
    return pl.pallas_call(
        kern,
        out_shape=tuple(out_shape),
        grid=(t // tm, PROJ_COLS // tn),
        in_specs=[
            pl.BlockSpec((tm, d), lambda m, n: (m, 0)),
            pl.BlockSpec((1, d), lambda m, n: (0, 0)),
            pl.BlockSpec((None, tn, d), lambda m, n: (layer, n, 0)),
            pl.BlockSpec((None, LANES, d), lambda m, n: (layer, dt_block, 0)),
            pl.BlockSpec((tm, HEAD_DIM), lambda m, n: (m % period_tiles, 0)),
            pl.BlockSpec((tm, HEAD_DIM), lambda m, n: (m % period_tiles, 0)),
        ],
        out_specs=tuple(out_specs),
        scratch_shapes=[pltpu.VMEM((tm, d), BF16)],
        compiler_params=_params(("arbitrary", "arbitrary")),
        name="in_proj",
    )(h, g, w_in_t, w_in_t, cos2, sin2)


ATT_BLOCK = 128
ATT_SPAN = max(w for w, _ in DILATED_BRANCHES)


def _prompt_attn_step(q_ref, k_ref, v_ref, o_ref, scr, w, *, seq, steps_per_unit):
    blk = ATT_BLOCK
    span = ATT_SPAN
    nbr = len(DILATED_BRANCHES)
    ob = [scr[2 * bi] for bi in range(nbr)]
    lb = [scr[2 * bi + 1] for bi in range(nbr)]
    n_spans = seq // span
    steps_per_span = steps_per_unit // n_spans
    assert seq % span == 0 and steps_per_unit % n_spans == 0
    s = _idiv(w, steps_per_span)
    ph = _imod(w, steps_per_span)
    s0 = pl.multiple_of(s * span, span)
    ii = lax.broadcasted_iota(jnp.int32, (blk, 2 * blk), 0)
    jj = lax.broadcasted_iota(jnp.int32, (blk, 2 * blk), 1)
    dist = ii + blk - jj

    def rows(start, dil):
        return pl.ds(start, blk, stride=dil) if dil > 1 else pl.ds(start, blk)

    def piece(ref, start, dil):
        return ref[rows(start, dil), :].astype(BF16)

    def attend(bi, dil, blocks):
        scs = []
        for base, _, kp, kc, _, _, first in blocks:
            kb = jnp.concatenate([kp, kc], axis=0)
            hi = blk if first is None else jnp.where(first, ii, blk)
            mask = (dist >= 0) & (dist <= hi)
            q = piece(q_ref, base, dil)
            scs.append(jnp.where(mask, _dot_nt(q, kb) * ATT_SCALE, NEG_INF))
        sc = jnp.concatenate(scs, axis=0)
        m = jnp.max(sc, axis=1, keepdims=True)
        e = jnp.exp(sc - m)
        ssum = jnp.sum(e, axis=1, keepdims=True)
        p = (e / ssum).astype(BF16)
        lse = jnp.broadcast_to(m + jnp.log(ssum), (len(blocks) * blk, HEAD_DIM))
        for n, (_, local, _, _, vp, vc, _) in enumerate(blocks):
            vb = jnp.concatenate([vp, vc], axis=0)
            ob[bi][rows(local, dil), :] = _dot(p[n * blk:(n + 1) * blk, :], vb)
            lb[bi][rows(local, dil), :] = lse[n * blk:(n + 1) * blk, :]

    for bi, (win, dil) in enumerate(DILATED_BRANCHES):
        assert win // dil == blk and span % (dil * blk) == 0
        stride_rows = blk * dil
        per_class = span // stride_rows
        if per_class >= 2:
            half = per_class // 2
            n_iter = dil * half
        else:
            n_iter = dil // 2
        assert n_iter % steps_per_span == 0
        per_step = n_iter // steps_per_span
        for j in range(per_step):
            idx = ph * per_step + j
            if per_class >= 2:
                r = _idiv(idx, half) if dil > 1 else 0
                i = idx - r * half
                local0 = r + (2 * i) * stride_rows
                base0 = s0 + local0
                first = (s == 0) & (i == 0)
                prev = jnp.maximum(base0 - stride_rows, r)
                km, vm = piece(k_ref, prev, dil), piece(v_ref, prev, dil)
                k0, v0 = piece(k_ref, base0, dil), piece(v_ref, base0, dil)
                k1 = piece(k_ref, base0 + stride_rows, dil)
                v1 = piece(v_ref, base0 + stride_rows, dil)
                attend(bi, dil, [
                    (base0, local0, km, k0, vm, v0, first),
                    (base0 + stride_rows, local0 + stride_rows, k0, k1, v0, v1, None)])
            else:
                blocks = []
                for e in range(2):
                    r = 2 * idx + e
                    base = s0 + r
                    prev = jnp.maximum(base - stride_rows, r)
                    blocks.append((base, r, piece(k_ref, prev, dil), piece(k_ref, base, dil),
                                   piece(v_ref, prev, dil), piece(v_ref, base, dil), s == 0))
                attend(bi, dil, blocks)

    @pl.when(ph == steps_per_span - 1)
    def _():
        rows_per = MERGE_ROWS

        def merge(c, carry):
            loc = pl.multiple_of(c * rows_per, rows_per)
            sl = pl.ds(loc, rows_per)
            ls = [lb[bi][sl, :] for bi in range(nbr)]
            mx = functools.reduce(jnp.maximum, ls)
            ws = [jnp.exp(l - mx) for l in ls]
            num = functools.reduce(lambda a, b: a + b,
                                   [wgt * ob[bi][sl, :] for bi, wgt in enumerate(ws)])
            den = functools.reduce(lambda a, b: a + b, ws)
            o_ref[pl.ds(s0 + loc, rows_per), :] = (num / den).astype(o_ref.dtype)
            return carry

        lax.fori_loop(0, span // rows_per, merge, 0)


def _sample_attn_step(qkv_ref, ck_ref, cv_ref, att_ref, nk_ref, nv_ref, *, n_past, t_new):
    qkv = qkv_ref[...]
    k_new = qkv[:, COL_K:COL_K + D_KV]
    v_new = qkv[:, COL_V:COL_V + D_KV]

    nrow = n_past * N_KV_HEADS
    shift = t_new * N_KV_HEADS
    assert shift % SUBLANES == 0
    step = COPY_ROWS
    for src, new, dst in ((ck_ref, k_new, nk_ref), (cv_ref, v_new, nv_ref)):
        for r0 in range(0, nrow - shift, step):
            n = min(step, nrow - shift - r0)
            dst[r0:r0 + n, :] = src[r0 + shift:r0 + shift + n, :]
        for j in range(t_new):
            for g in range(N_KV_HEADS):
                r = nrow - shift + j * N_KV_HEADS + g
                dst[r:r + 1, :] = new[j:j + 1, g * HEAD_DIM:(g + 1) * HEAD_DIM]

    nq = Q_PER_KV * t_new
    nall = N_KV_HEADS * nq
    vcs, vns, s_c, s_n = [], [], [], [[] for _ in range(t_new)]
    for g in range(N_KV_HEADS):
        qg = jnp.concatenate(
            [qkv[:, (g * Q_PER_KV + r) * HEAD_DIM:(g * Q_PER_KV + r + 1) * HEAD_DIM]
             for r in range(Q_PER_KV)], axis=0).astype(BF16)
        kc = ck_ref[pl.ds(g, n_past, stride=N_KV_HEADS), :].astype(BF16)
        vcs.append(cv_ref[pl.ds(g, n_past, stride=N_KV_HEADS), :].astype(BF16))
        kn = k_new[:, g * HEAD_DIM:(g + 1) * HEAD_DIM].astype(BF16).astype(F32)
        vns.append(v_new[:, g * HEAD_DIM:(g + 1) * HEAD_DIM].astype(BF16).astype(F32))
        s_c.append(_dot_nt(qg, kc))
        qf = qg.astype(F32)
        for j in range(t_new):
            s_n[j].append(jnp.sum(qf * kn[j:j + 1, :], axis=1, keepdims=True))
    s_c = jnp.concatenate(s_c, axis=0) * ATT_SCALE
    s_n = [jnp.concatenate(x, axis=0) * ATT_SCALE for x in s_n]

    cidx = lax.broadcasted_iota(jnp.int32, (nall, n_past), 1)
    tok = _imod(lax.broadcasted_iota(jnp.int32, (nall, n_past), 0), t_new)
    dist_c = n_past + tok - cidx
    tok1 = _imod(lax.broadcasted_iota(jnp.int32, (nall, 1), 0), t_new)

    probs, new_probs, lses = [], [], []
    for win, dil in DILATED_BRANCHES:
        c0 = n_past - win
        assert c0 >= 0 and c0 % LANES == 0
        dist_b = dist_c[:, c0:]
        mask_c = (_imod(dist_b, dil) == 0) & (dist_b <= win)
        sc = jnp.where(mask_c, s_c[:, c0:], NEG_INF)
        m = jnp.max(sc, axis=1, keepdims=True)
        sn = []
        for j in range(t_new):
            dn = tok1 - j
            mask_n = (dn >= 0) & (_imod(dn, dil) == 0)
            snj = jnp.where(mask_n, s_n[j], NEG_INF)
            sn.append(snj)
            m = jnp.maximum(m, snj)
        ec = jnp.exp(sc - m)
        en = [jnp.exp(x - m) for x in sn]
        ssum = jnp.sum(ec, axis=1, keepdims=True)
        for x in en:
            ssum = ssum + x
        probs.append((ec / ssum).astype(BF16))
        new_probs.append([(x / ssum).astype(BF16).astype(F32) for x in en])
        lses.append(m + jnp.log(ssum))

    mx = functools.reduce(jnp.maximum, lses)
    ws = [jnp.exp(l - mx) for l in lses]
    den = functools.reduce(lambda a, b: a + b, ws)
    for g in range(N_KV_HEADS):
        rs = slice(g * nq, (g + 1) * nq)
        o = jnp.zeros((nq, HEAD_DIM), F32)
        for i, wgt in enumerate(ws):
            cols = probs[i].shape[1]
            o_br = _dot(probs[i][rs, :], vcs[g][n_past - cols:, :])
            for j in range(t_new):
                o_br = o_br + new_probs[i][j][rs, :] * vns[g][j:j + 1, :]
            o = o + wgt[rs, :] * o_br
        o = o / den[rs, :]
        for r in range(Q_PER_KV):
            att_ref[:, (g * Q_PER_KV + r) * HEAD_DIM:(g * Q_PER_KV + r + 1) * HEAD_DIM] = (
                o[r * t_new:(r + 1) * t_new, :])


def _attn_kernel(*refs, n_past, t_new, aliased, seq, steps_per_unit):
    if aliased:
        (qkv_ref, ck_ref, cv_ref, q_ref, k_ref, v_ref, _, _,
         att_s_ref, nk_ref, nv_ref, att_p_ref, *scr) = refs
    else:
        (qkv_ref, ck_ref, cv_ref, q_ref, k_ref, v_ref,
         att_s_ref, nk_ref, nv_ref, att_p_ref, *scr) = refs
    _sample_attn_step(qkv_ref, ck_ref, cv_ref, att_s_ref, nk_ref, nv_ref, n_past=n_past, t_new=t_new)
    w = _imod(pl.program_id(0), steps_per_unit)
    _prompt_attn_step(q_ref, k_ref, v_ref, att_p_ref, scr, w, seq=seq, steps_per_unit=steps_per_unit)


def _attn(qkv_s, cache_k, cache_v, proj, layer, prev_k, prev_v, *, batch, seq):
    depth, bsz, nrow, _ = cache_k.shape
    n_past = nrow // N_KV_HEADS
    t_new = qkv_s.shape[1]
    units = batch * N_Q_HEADS
    steps_per_unit = bsz // units
    assert bsz == units * steps_per_unit
    aliased = prev_k is not None
    kern = functools.partial(_attn_kernel, n_past=n_past, t_new=t_new, aliased=aliased, seq=seq,
                             steps_per_unit=steps_per_unit)
    kq = COL_K // HEAD_DIM
    vq = COL_V // HEAD_DIM

    def unit(i):
        u = i // steps_per_unit
        return u // N_Q_HEADS, u % N_Q_HEADS

    win_spec = pl.BlockSpec((None, None, nrow, HEAD_DIM), lambda i: (layer, i, 0, 0))
    in_specs = [
        pl.BlockSpec((None, t_new, qkv_s.shape[2]), lambda i: (i, 0, 0)), win_spec, win_spec,
        pl.BlockSpec((seq, HEAD_DIM), lambda i: unit(i)),
        pl.BlockSpec((seq, HEAD_DIM), lambda i: (unit(i)[0], kq + unit(i)[1] // Q_PER_KV)),
        pl.BlockSpec((seq, HEAD_DIM), lambda i: (unit(i)[0], vq + unit(i)[1] // Q_PER_KV)),
    ]
    args = [qkv_s, cache_k, cache_v, proj, proj, proj]
    aliases = {}
    if aliased:
        in_specs += [pl.BlockSpec(memory_space=pl.ANY), pl.BlockSpec(memory_space=pl.ANY)]
        args += [prev_k, prev_v]
        aliases = {6: 1, 7: 2}
    n_scr = len(DILATED_BRANCHES) * 2
    return pl.pallas_call(
        kern,
        out_shape=(jax.ShapeDtypeStruct((bsz, t_new, D_ATT), F32),
                   jax.ShapeDtypeStruct(cache_k.shape, cache_k.dtype),
                   jax.ShapeDtypeStruct(cache_v.shape, cache_v.dtype),
                   jax.ShapeDtypeStruct((batch * seq, D_ATT), BF16)),
        grid=(bsz,),
        in_specs=in_specs,
        out_specs=(pl.BlockSpec((None, t_new, D_ATT), lambda i: (i, 0, 0)), win_spec, win_spec,
                   pl.BlockSpec((seq, HEAD_DIM), lambda i: unit(i))),
        scratch_shapes=[pltpu.VMEM((ATT_SPAN, HEAD_DIM), F32) for _ in range(n_scr)],
        input_output_aliases=aliases,
        compiler_params=pltpu.CompilerParams(dimension_semantics=("arbitrary",),
                                             vmem_limit_bytes=ATTN_VMEM_LIMIT_BYTES),
        name="attn",
    )(*args)


def _expand_mats():
    r64 = np.zeros((LANES, D_SSM), np.float32)
    r128 = np.zeros((LANES, N_SSM_HEADS * LANES), np.float32)
    for h in range(N_SSM_HEADS):
        r64[h, h * SSM_HEAD_DIM:(h + 1) * SSM_HEAD_DIM] = 1.0
        r128[h, h * LANES:(h + 1) * LANES] = 1.0
    return jnp.asarray(r64, BF16), jnp.asarray(r128, BF16)


def _group_sum_mat():
    g = np.zeros((D_BC, D_SSM), np.float32)
    for grp in range(N_SSM_GROUPS):
        g[grp * D_STATE:(grp + 1) * D_STATE,
          grp * HEADS_PER_GROUP * SSM_HEAD_DIM:(grp + 1) * HEADS_PER_GROUP * SSM_HEAD_DIM] = 1.0
    return jnp.asarray(g, BF16)


def _conv_silu(x, tail, cw, cb):
    n = x.shape[0]
    xp = jnp.concatenate([tail, x], axis=0)
    out = cb + cw[3:4, :] * x
    for w in range(CONV_WIDTH - 1):
        off = SUBLANES - (CONV_WIDTH - 1) + w
        out = out + cw[w:w + 1, :] * xp[off:off + n, :]
    return _silu(out)


def _cumsum_rows(x, seg=None):
    n = x.shape[0]
    rows = lax.broadcasted_iota(jnp.int32, x.shape, 0)
    pos = rows if seg is None else _imod(rows, seg)
    limit = n if seg is None else seg
    sh = 1
    while sh < limit:
        x = x + jnp.where(pos >= sh, pltpu.roll(x, sh, axis=0), 0.0)
        sh *= 2
    return x


def _lane_col_block(row):
    return jnp.broadcast_to(row, (LANES, LANES)).T


def _ssd_chunk(z, xraw, bcraw, dt_raw, tail, h_prev, cw, cb, dtb, alog, dsk, ng, r64, r128):
    cs = SSD_CHUNK
    xs = _conv_silu(xraw, tail[:, :D_SSM], cw[:, :D_SSM], cb[:, :D_SSM])
    bcm = _conv_silu(bcraw, tail[:, D_SSM:], cw[:, D_SSM:], cb[:, D_SSM:])

    dt = _softplus(dt_raw + dtb)
    a = -jnp.exp(alog)
    acum = _cumsum_rows(dt * a)
    acum_t = acum.T
    dt_e = _expand(dt, r64)
    ac_e = _expand(acum, r64)
    col_b = _expand(acum, r128)
    last_e = ac_e[cs - 1:cs, :]
    ea_e = jnp.exp(ac_e)
    xdt = xs * dt_e
    xte = (xdt * jnp.exp(last_e - ac_e)).astype(BF16)

    ti = lax.broadcasted_iota(jnp.int32, (cs, cs), 0)
    si = lax.broadcasted_iota(jnp.int32, (cs, cs), 1)
    causal = ti >= si
    lane = lax.broadcasted_iota(jnp.int32, (cs, LANES), 1)
    lo_half = lane < SSM_HEAD_DIM

    gw = HEADS_PER_GROUP * SSM_HEAD_DIM
    y_diag, y_off, states = [], [], []
    for g in range(N_SSM_GROUPS):
        bg = bcm[:, g * D_STATE:(g + 1) * D_STATE].astype(BF16)
        cg = bcm[:, D_BC + g * D_STATE:D_BC + (g + 1) * D_STATE].astype(BF16)
        cbt = _dot_nt(cg, bg)
        h_in = h_prev[g * gw:(g + 1) * gw, :]
        y_off.append(_dot_nt(cg, h_in.astype(BF16)))
        states.append(lax.dot_general(xte[:, g * gw:(g + 1) * gw], bg, TN_DIMS,
                                      preferred_element_type=F32))
        for k in range(HEADS_PER_GROUP // 2):
            pair = g * (HEADS_PER_GROUP // 2) + k
            xp = xdt[:, pair * LANES:(pair + 1) * LANES]
            yd = jnp.zeros((cs, LANES), F32)
            for e in range(2):
                h = 2 * pair + e
                seg = col_b[:, h * LANES:(h + 1) * LANES] - jnp.broadcast_to(acum_t[h:h + 1, :], (cs, cs))
                dec = jnp.exp(jnp.where(causal, seg, NEG_INF))
                cbh = (cbt * dec).astype(BF16)
                xh = jnp.where(lo_half if e == 0 else jnp.logical_not(lo_half), xp, 0.0).astype(BF16)
                yd = yd + _dot(cbh, xh)
            y_diag.append(yd)

    y = (jnp.concatenate(y_diag, axis=1) + jnp.concatenate(y_off, axis=1) * ea_e + dsk * xs)
    dec_rows = jnp.exp(jnp.concatenate(
        [_lane_col_block(last_e[:, k * LANES:(k + 1) * LANES]) for k in range(D_SSM // LANES)],
        axis=0))
    h_new = h_prev * dec_rows + jnp.concatenate(states, axis=0)
    return _rms(y * _silu(z), ng).astype(BF16), h_new


SSD_CHUNKS_PER_STEP = 4


def _ssd_prompt_kernel(z_ref, xs_ref, bc_ref, dt_ref, cw_ref, cb_ref, dtb_ref, alog_ref, dsk_ref,
                       ng_ref, r64_ref, r128_ref, y_ref, st_ref, cv_ref, h_scr, tail_scr):
    c = pl.program_id(1)
    last = pl.num_programs(1) - 1
    cs = SSD_CHUNK

    @pl.when(c == 0)
    def _():
        h_scr[...] = jnp.zeros_like(h_scr)
        tail_scr[...] = jnp.zeros_like(tail_scr)

    consts = (cw_ref[...], cb_ref[...], dtb_ref[...], alog_ref[...], dsk_ref[...], ng_ref[...],
              r64_ref[...], r128_ref[...])
    h = h_scr[...]
    tail = tail_scr[...]
    for k in range(SSD_CHUNKS_PER_STEP):
        rs = slice(k * cs, (k + 1) * cs)
        xraw = xs_ref[rs, :]
        bcraw = bc_ref[rs, :]
        y, h = _ssd_chunk(z_ref[rs, :], xraw, bcraw, dt_ref[rs, :], tail, h, *consts)
        y_ref[rs, :] = y
        tail = jnp.concatenate([xraw[cs - SUBLANES:, :], bcraw[cs - SUBLANES:, :]], axis=1)
    h_scr[...] = h
    tail_scr[...] = tail

    @pl.when(c == last)
    def _():
        st_ref[...] = h
        cv_ref[...] = tail[SUBLANES - (CONV_WIDTH - 1):, :]


def _ssd_prompt(proj, dt_all, layer, conv_w, conv_b3, dtb, alog, dsk_e, ng3, r64, r128, batch, seq):
    cs = SSD_CHUNK * SSD_CHUNKS_PER_STEP
    assert seq % cs == 0
    nc = seq // cs
    zc, xc, bcc = COL_Z // D_SSM, COL_X // D_SSM, COL_BC // D_SSM
    const2 = lambda b, c: (0, 0)
    return pl.pallas_call(
        _ssd_prompt_kernel,
        out_shape=(jax.ShapeDtypeStruct((batch * seq, D_SSM), BF16),
                   jax.ShapeDtypeStruct((batch, D_SSM, D_STATE), F32),
                   jax.ShapeDtypeStruct((batch, CONV_WIDTH - 1, CONV_DIM), F32)),
        grid=(batch, nc),
        in_specs=[
            pl.BlockSpec((cs, D_SSM), lambda b, c: (b * nc + c, zc)),
            pl.BlockSpec((cs, D_SSM), lambda b, c: (b * nc + c, xc)),
            pl.BlockSpec((cs, D_SSM), lambda b, c: (b * nc + c, bcc)),
            pl.BlockSpec((cs, LANES), lambda b, c: (b * nc + c, 0)),
            pl.BlockSpec((None, CONV_WIDTH, CONV_DIM), lambda b, c: (layer, 0, 0)),
            pl.BlockSpec((None, 1, CONV_DIM), lambda b, c: (layer, 0, 0)),
            pl.BlockSpec((None, 1, LANES), lambda b, c: (layer, 0, 0)),
            pl.BlockSpec((None, 1, LANES), lambda b, c: (layer, 0, 0)),
            pl.BlockSpec((None, 1, D_SSM), lambda b, c: (layer, 0, 0)),
            pl.BlockSpec((None, 1, D_SSM), lambda b, c: (layer, 0, 0)),
            pl.BlockSpec(r64.shape, const2),
            pl.BlockSpec(r128.shape, const2),
        ],
        out_specs=(pl.BlockSpec((cs, D_SSM), lambda b, c: (b * nc + c, 0)),
                   pl.BlockSpec((None, D_SSM, D_STATE), lambda b, c: (b, 0, 0)),
                   pl.BlockSpec((None, CONV_WIDTH - 1, CONV_DIM), lambda b, c: (b, 0, 0))),
        scratch_shapes=[pltpu.VMEM((D_SSM, D_STATE), F32), pltpu.VMEM((SUBLANES, CONV_DIM), F32)],
        compiler_params=_params(("parallel", "arbitrary")),
        name="ssd_prompt",
    )(proj, proj, proj, dt_all, conv_w, conv_b3, dtb, alog, dsk_e, ng3, r64, r128)


def _ssd_sample_kernel(*refs, bt, t_new, aliased):
    if aliased:
        (z_ref, xs_ref, bc_ref, dt_ref, cst_ref, h0_ref, cw_ref, cb_ref, dtb_ref, alog_ref, dsk_ref,
         ng_ref, r64_ref, gs_ref, _, _, y_ref, st_ref, cv_ref, xbc_scr) = refs
    else:
        (z_ref, xs_ref, bc_ref, dt_ref, cst_ref, h0_ref, cw_ref, cb_ref, dtb_ref, alog_ref, dsk_ref,
         ng_ref, r64_ref, gs_ref, y_ref, st_ref, cv_ref, xbc_scr) = refs
    rows = bt * t_new
    kw = CONV_WIDTH - 1
    cw = cw_ref[...]
    cb = cb_ref[...]

    for b in range(bt):
        xb = jnp.concatenate([xs_ref[b * t_new:(b + 1) * t_new, :],
                              bc_ref[b * t_new:(b + 1) * t_new, :]], axis=1)
        xp = jnp.concatenate([cst_ref[b], xb], axis=0)
        out = cb
        for w in range(CONV_WIDTH):
            out = out + cw[w:w + 1, :] * xp[w:w + t_new, :]
        xbc_scr[b * t_new:(b + 1) * t_new, :] = _silu(out)
        cv_ref[b] = xp[t_new:t_new + kw, :]

    xbc = xbc_scr[...]
    xs = xbc[:, :D_SSM]
    bm = xbc[:, D_SSM:D_SSM + D_BC]
    cm = xbc[:, D_SSM + D_BC:]

    tpos = _imod(lax.broadcasted_iota(jnp.int32, (rows, 1), 0), t_new)
    dt = _softplus(dt_ref[...] + dtb_ref[...])
    a = -jnp.exp(alog_ref[...])
    acum = _cumsum_rows(dt * a, seg=t_new)
    r64 = r64_ref[...]
    dt_e = _expand(dt, r64)
    ac_e = _expand(acum, r64)
    v = jnp.where(tpos == t_new - 1, ac_e, 0.0)
    last_e = v
    for d in range(1, t_new):
        last_e = last_e + pltpu.roll(v, rows - d, axis=0)
    ea_e = jnp.exp(ac_e)
    xdt = xs * dt_e
    xte = xdt * jnp.exp(last_e - ac_e)

    cmb = cm.astype(BF16).astype(F32)
    bmb = bm.astype(BF16).astype(F32)
    gs = gs_ref[...]
    y = dsk_ref[...] * xs
    for d in range(t_new):
        if d == 0:
            b_s, x_s, a_s = bmb, xdt, ac_e
        else:
            b_s = pltpu.roll(bmb, d, axis=0)
            x_s = pltpu.roll(xdt, d, axis=0)
            a_s = pltpu.roll(ac_e, d, axis=0)
        cb_e = _expand(cmb * b_s, gs)
        dec = jnp.exp(jnp.where(tpos >= d, ac_e - a_s, NEG_INF))
        y = y + cb_e * dec * x_s

    gw = HEADS_PER_GROUP * SSM_HEAD_DIM
    pad = jnp.zeros((LANES - rows, LANES), F32) if rows < LANES else None
    rowb = _idiv(lax.broadcasted_iota(jnp.int32, (rows, 1), 0), t_new)
    colb = _idiv(lax.broadcasted_iota(jnp.int32, (1, LANES), 1), t_new)
    xte_t = []
    for k in range(D_SSM // LANES):
        blk = xte[:, k * LANES:(k + 1) * LANES]
        if pad is not None:
            blk = jnp.concatenate([blk, pad], axis=0)
        xte_t.append(blk.T)
    e_last = jnp.exp(last_e)
    y_off = [jnp.zeros((rows, gw), F32) for _ in range(N_SSM_GROUPS)]
    for b in range(bt):
        h0 = h0_ref[b]
        new_rows = []
        for g in range(N_SSM_GROUPS):
            cg = cm[:, g * D_STATE:(g + 1) * D_STATE]
            bg = bm[:, g * D_STATE:(g + 1) * D_STATE]
            if pad is not None:
                bg = jnp.concatenate([bg, pad], axis=0)
            bg = bg.astype(BF16)
            h0g = h0[g * gw:(g + 1) * gw, :]
            cgb = jnp.where(rowb == b, cg, 0.0).astype(BF16)
            y_off[g] = y_off[g] + _dot_nt(cgb, h0g.astype(BF16))
            for k in range(gw // LANES):
                blk = g * (gw // LANES) + k
                lhs = jnp.where(colb == b, xte_t[blk], 0.0).astype(BF16)
                st = _dot(lhs, bg)
                r = b * t_new + t_new - 1
                dec = _lane_col_block(e_last[r:r + 1, blk * LANES:(blk + 1) * LANES])
                new_rows.append(h0[blk * LANES:(blk + 1) * LANES, :] * dec + st)
        st_ref[b] = jnp.concatenate(new_rows, axis=0)

    y = y + jnp.concatenate(y_off, axis=1) * ea_e
    y_ref[...] = _rms(y * _silu(z_ref[...]), ng_ref[...]).astype(y_ref.dtype)


def _ssd_sample(proj_s, dt_s, state_conv, state_ssm, layer, conv_w, conv_b3, dtb, alog, dsk_e, ng3,
                r64, gsum, prev_st, prev_cv, *, bt, t_new):
    depth, bsz = state_ssm.shape[:2]
    rows = bt * t_new
    zc, xc, bcc = COL_Z // D_SSM, COL_X // D_SSM, COL_BC // D_SSM
    aliased = prev_st is not None
    kern = functools.partial(_ssd_sample_kernel, bt=bt, t_new=t_new, aliased=aliased)
    const2 = lambda i: (0, 0)
    in_specs = [
        pl.BlockSpec((rows, D_SSM), lambda i: (i, zc)),
        pl.BlockSpec((rows, D_SSM), lambda i: (i, xc)),
        pl.BlockSpec((rows, D_SSM), lambda i: (i, bcc)),
        pl.BlockSpec((rows, LANES), lambda i: (i, 0)),
        pl.BlockSpec((None, bt, CONV_WIDTH - 1, CONV_DIM), lambda i: (layer, i, 0, 0)),
        pl.BlockSpec((None, bt, D_SSM, D_STATE), lambda i: (layer, i, 0, 0)),
        pl.BlockSpec((None, CONV_WIDTH, CONV_DIM), lambda i: (layer, 0, 0)),
        pl.BlockSpec((None, 1, CONV_DIM), lambda i: (layer, 0, 0)),
        pl.BlockSpec((None, 1, LANES), lambda i: (layer, 0, 0)),
        pl.BlockSpec((None, 1, LANES), lambda i: (layer, 0, 0)),
        pl.BlockSpec((None, 1, D_SSM), lambda i: (layer, 0, 0)),
        pl.BlockSpec((None, 1, D_SSM), lambda i: (layer, 0, 0)),
        pl.BlockSpec(r64.shape, const2),
        pl.BlockSpec(gsum.shape, const2),
    ]
    args = [proj_s, proj_s, proj_s, dt_s, state_conv, state_ssm, conv_w, conv_b3, dtb, alog, dsk_e,
            ng3, r64, gsum]
    aliases = {}
    if aliased:
        in_specs += [pl.BlockSpec(memory_space=pl.ANY), pl.BlockSpec(memory_space=pl.ANY)]
        args += [prev_st, prev_cv]
        aliases = {14: 1, 15: 2}
    return pl.pallas_call(
        kern,
        out_shape=(jax.ShapeDtypeStruct((bsz * t_new, D_SSM), BF16),
                   jax.ShapeDtypeStruct(state_ssm.shape, state_ssm.dtype),
                   jax.ShapeDtypeStruct(state_conv.shape, state_conv.dtype)),
        grid=(bsz // bt,),
        in_specs=in_specs,
        out_specs=(pl.BlockSpec((rows, D_SSM), lambda i: (i, 0)),
                   pl.BlockSpec((None, bt, D_SSM, D_STATE), lambda i: (layer, i, 0, 0)),
                   pl.BlockSpec((None, bt, CONV_WIDTH - 1, CONV_DIM), lambda i: (layer, i, 0, 0))),
        scratch_shapes=[pltpu.VMEM((rows, CONV_DIM), F32)],
        input_output_aliases=aliases,
        compiler_params=_params(("parallel",)),
        name="ssd_sample",
    )(*args)


def _resident(shape, index_map):
    return pl.BlockSpec(shape, index_map, pipeline_mode=pl.Buffered(1))


def _out_proj_kernel(att_ref, ssm_ref, h_ref, w_ref, o_ref, wb_ref):
    @pl.when(pl.program_id(0) == 0)
    def _():
        wb_ref[...] = w_ref[...].astype(BF16)

    acc = _dot(att_ref[...], wb_ref[:D_ATT, :]) + _dot(ssm_ref[...], wb_ref[D_ATT:, :])
    o_ref[...] = h_ref[...] + acc


def _out_proj(att, ssm, h, w_out, layer):
    t, d = h.shape
    tm = _row_tile(t, ROWS_WEIGHT_RESIDENT)
    return pl.pallas_call(
        _out_proj_kernel,
        out_shape=jax.ShapeDtypeStruct((t, d), F32),
        grid=(t // tm,),
        in_specs=[
            pl.BlockSpec((tm, D_ATT), lambda m: (m, 0)),
            pl.BlockSpec((tm, D_SSM), lambda m: (m, 0)),
            pl.BlockSpec((tm, d), lambda m: (m, 0)),
            _resident((None,) + w_out.shape[1:], lambda m: (layer, 0, 0)),
        ],
        out_specs=pl.BlockSpec((tm, d), lambda m: (m, 0)),
        scratch_shapes=[pltpu.VMEM(w_out.shape[1:], BF16)],
        compiler_params=_params(("arbitrary",)),
        name="out_proj",
    )(att, ssm, h, w_out)


def _ffn_kernel(h_ref, g_ref, wg_ref, wu_ref, wd_ref, o_ref, hf_ref):
    f = pl.program_id(1)

    @pl.when(f == 0)
    def _():
        h = h_ref[...]
        hf_ref[...] = _rms(h, g_ref[...]).astype(BF16)
        o_ref[...] = h

    hf = hf_ref[...]
    act = _silu(_dot(hf, wg_ref[...].astype(BF16))) * _dot(hf, wu_ref[...].astype(BF16))
    o_ref[...] += _dot(act.astype(BF16), wd_ref[...].astype(BF16))


def _ffn(h, g, wg, wu, wd, layer):
    t, d = h.shape
    dff = wg.shape[2]
    tm = _row_tile(t, ROWS_WEIGHT_STREAM)
    tf = FFN_COLS if tm > ROWS_WEIGHT_RESIDENT else FFN_COLS_FEW_ROWS
    assert dff % tf == 0
    return pl.pallas_call(
        _ffn_kernel,
        out_shape=jax.ShapeDtypeStruct((t, d), F32),
        grid=(t // tm, dff // tf),
        in_specs=[
            pl.BlockSpec((tm, d), lambda m, f: (m, 0), pipeline_mode=pl.Buffered(1)),
            pl.BlockSpec((1, d), lambda m, f: (0, 0)),
            pl.BlockSpec((None, d, tf), lambda m, f: (layer, 0, f)),
            pl.BlockSpec((None, d, tf), lambda m, f: (layer, 0, f)),
            pl.BlockSpec((None, tf, d), lambda m, f: (layer, f, 0)),
        ],
        out_specs=pl.BlockSpec((tm, d), lambda m, f: (m, 0)),
        scratch_shapes=[pltpu.VMEM((tm, d), BF16)],
        compiler_params=_params(("parallel", "arbitrary")),
        name="ffn",
    )(h, g, wg, wu, wd)


def _ple_kernel(h_ref, p_ref, g_ref, wg_ref, wp_ref, gf_ref, o_ref, wgb_ref, wpb_ref, *, final, tn):
    @pl.when(pl.program_id(0) == 0)
    def _():
        wgb_ref[...] = wg_ref[...].astype(BF16)
        wpb_ref[...] = wp_ref[...].astype(BF16)

    hn = _rms(h_ref[...], g_ref[...]).astype(BF16)
    pb = p_ref[...].astype(BF16)
    for c in range(h_ref.shape[1] // tn):
        cols = slice(c * tn, (c + 1) * tn)
        gate = jax.nn.sigmoid(_dot(hn, wgb_ref[:, cols]))
        o_ref[:, cols] = h_ref[:, cols] + gate * _dot(pb, wpb_ref[:, cols])
    if final:
        o_ref[...] = _rms(o_ref[...], gf_ref[...])


def _ple(h, p, g, wg, wp, gf, layer, *, final):
    t, d = h.shape
    tm = _row_tile(t, ROWS_WEIGHT_RESIDENT)
    kern = functools.partial(_ple_kernel, final=final, tn=PLE_COLS)
    return pl.pallas_call(
        kern,
        out_shape=jax.ShapeDtypeStruct((t, d), F32),
        grid=(t // tm,),
        in_specs=[
            pl.BlockSpec((tm, d), lambda m: (m, 0)),
            pl.BlockSpec((None, tm, p.shape[2]), lambda m: (layer, m, 0)),
            pl.BlockSpec((1, d), lambda m: (0, 0)),
            _resident((None,) + wg.shape[1:], lambda m: (layer, 0, 0)),
            _resident((None,) + wp.shape[1:], lambda m: (layer, 0, 0)),
            pl.BlockSpec((1, d), lambda m: (0, 0)),
        ],
        out_specs=pl.BlockSpec((tm, d), lambda m: (m, 0)),
        scratch_shapes=[pltpu.VMEM(wg.shape[1:], BF16), pltpu.VMEM(wp.shape[1:], BF16)],
        compiler_params=_params(("arbitrary",)),
        name="ple",
    )(h, p, g, wg, wp, gf)


def _rope_tables(pos):
    half = HEAD_DIM // 2
    inv_freq = ROPE_THETA ** (-jnp.arange(half, dtype=F32) / half)
    ang = pos.astype(F32)[:, None] * inv_freq[None, :]
    cos, sin = jnp.cos(ang), jnp.sin(ang)
    return jnp.concatenate([cos, cos], axis=1), jnp.concatenate([-sin, sin], axis=1)


def _pad_lanes(x):
    return jnp.pad(x, [(0, 0)] * (x.ndim - 1) + [(0, LANES - x.shape[-1])])


def kernel(x_prompt, x_sample, cache_k, cache_v, state_ssm, state_conv, p_prompt, p_sample,
           norm_mix_g, w_in, conv_w, conv_b, dt_bias, a_log, d_skip, ssm_norm_g, w_out,
           norm_ffn_g, w_ffn_gate, w_ffn_up, w_ffn_down, norm_ple_g, w_ple_gate, w_ple_proj,
           final_norm_g):
    batch, seq, d = x_prompt.shape
    dec_batch, dec_seq, _ = x_sample.shape
    depth = w_in.shape[0]
    n_past = cache_k.shape[2]
    tp = batch * seq
    ts = dec_batch * dec_seq
    assert w_in.shape[2] == PROJ_COLS + N_SSM_HEADS and n_past == PAST_LEN

    cos_p, sin_p = _rope_tables(jnp.arange(seq, dtype=jnp.int32))
    cos_s, sin_s = _rope_tables(jnp.tile(PAST_LEN + jnp.arange(dec_seq, dtype=jnp.int32), dec_batch))
    r64, r128 = _expand_mats()
    w_in_t = jnp.swapaxes(w_in, 1, 2)
    gsum = _group_sum_mat()

    ck = cache_k.reshape(depth, dec_batch, n_past * N_KV_HEADS, HEAD_DIM)
    cv = cache_v.reshape(depth, dec_batch, n_past * N_KV_HEADS, HEAD_DIM)
    st_in = state_ssm.reshape(depth, dec_batch, D_SSM, D_STATE)
    pp = p_prompt.reshape(depth, tp, -1)
    ps = p_sample.reshape(depth, ts, -1)
    conv_b3 = conv_b.reshape(depth, 1, CONV_DIM)
    dtb = _pad_lanes(dt_bias).reshape(depth, 1, LANES)
    alog = _pad_lanes(a_log).reshape(depth, 1, LANES)
    dsk_e = jnp.repeat(d_skip, SSM_HEAD_DIM, axis=1).reshape(depth, 1, D_SSM)
    ng3 = ssm_norm_g.reshape(depth, 1, D_SSM)
    gf = final_norm_g.reshape(1, d)

    hp = x_prompt.reshape(tp, d)
    hs = x_sample.reshape(ts, d)
    nk_s = nv_s = st_s = cv_s = None
    k_p, v_p, st_p, cv_p = [], [], [], []
    for i in range(depth):
        g_mix = norm_mix_g[i].reshape(1, d)
        g_ffn = norm_ffn_g[i].reshape(1, d)
        g_ple = norm_ple_g[i].reshape(1, d)
        last = i == depth - 1

        keep = min(ATT_WINDOW, seq)
        proj, dt_p, k_i, v_i = _in_proj(hp, g_mix, w_in_t, i, cos_p, sin_p, keep=keep)
        proj_s, dt_s = _in_proj(hs, g_mix, w_in_t, i, cos_s, sin_s)
        qkv_s = proj_s[:, :COL_Z].reshape(dec_batch, dec_seq, COL_Z)
        att_s, nk_s, nv_s, att_p = _attn(qkv_s, ck, cv, proj, i, nk_s, nv_s, batch=batch, seq=seq)

        y_p, st_i, cv_i = _ssd_prompt(proj, dt_p, i, conv_w, conv_b3, dtb, alog, dsk_e, ng3,
                                      r64, r128, batch, seq)
        hp = _out_proj(att_p, y_p, hp, w_out, i)
        hp = _ffn(hp, g_ffn, w_ffn_gate, w_ffn_up, w_ffn_down, i)
        hp = _ple(hp, pp, g_ple, w_ple_gate, w_ple_proj, gf, i, final=last)

        k_p.append(k_i.reshape(batch, keep, N_KV_HEADS, HEAD_DIM))
        v_p.append(v_i.reshape(batch, keep, N_KV_HEADS, HEAD_DIM))
        st_p.append(st_i.reshape(batch, N_SSM_HEADS, SSM_HEAD_DIM, D_STATE))
        cv_p.append(cv_i)

        y_s, st_s, cv_s = _ssd_sample(proj_s, dt_s, state_conv, st_in, i, conv_w, conv_b3,
                                      dtb, alog, dsk_e, ng3, r64, gsum, st_s, cv_s,
                                      bt=SSD_SAMPLE_SEQS, t_new=dec_seq)
        hs = _out_proj(att_s.reshape(ts, D_ATT).astype(BF16), y_s, hs, w_out, i)
        hs = _ffn(hs, g_ffn, w_ffn_gate, w_ffn_up, w_ffn_down, i)
        hs = _ple(hs, ps, g_ple, w_ple_gate, w_ple_proj, gf, i, final=last)

    return (hp.reshape(batch, seq, d), hs.reshape(dec_batch, dec_seq, d),
            jnp.stack(k_p), jnp.stack(v_p), jnp.stack(st_p), jnp.stack(cv_p),
            nk_s.reshape(cache_k.shape), nv_s.reshape(cache_v.shape),
            st_s.reshape(state_ssm.shape), cv_s)
```

```python
import functools

import numpy as np
import jax
import jax.numpy as jnp
from jax import lax
from jax.experimental import pallas as pl
from jax.experimental.pallas import tpu as pltpu

F32 = jnp.float32
BF16 = jnp.bfloat16

N_Q_HEADS = 8
N_KV_HEADS = 4
Q_PER_KV = N_Q_HEADS // N_KV_HEADS
HEAD_DIM = 128
D_ATT = N_Q_HEADS * HEAD_DIM
D_KV = N_KV_HEADS * HEAD_DIM
DILATED_BRANCHES = ((128, 1), (512, 4), (2048, 16))
ATT_WINDOW = 2048
ROPE_THETA = 10000.0
ATT_SCALE = HEAD_DIM ** -0.5
N_SSM_HEADS = 16
SSM_HEAD_DIM = 64
D_SSM = N_SSM_HEADS * SSM_HEAD_DIM
N_SSM_GROUPS = 4
HEADS_PER_GROUP = N_SSM_HEADS // N_SSM_GROUPS
D_STATE = 128
D_BC = N_SSM_GROUPS * D_STATE
CONV_WIDTH = 4
CONV_DIM = D_SSM + 2 * D_BC
SSD_CHUNK = 128
PAST_LEN = 2048
EPS = 1e-6
NEG_INF = -1e30

LANES = 128
SUBLANES = 8
MIB = 1024 * 1024
VMEM_LIMIT_BYTES = 56 * MIB
ATTN_VMEM_LIMIT_BYTES = 60 * MIB

ROWS_WEIGHT_STREAM = 1024
ROWS_WEIGHT_RESIDENT = 512
IN_PROJ_COLS = 512
FFN_COLS = 256
FFN_COLS_FEW_ROWS = 512
PLE_COLS = 512
SSD_SAMPLE_SEQS = 16
COPY_ROWS = 1024
MERGE_ROWS = 256

COL_Q = 0
COL_K = D_ATT
COL_V = D_ATT + D_KV
COL_Z = D_ATT + 2 * D_KV
COL_X = COL_Z + D_SSM
COL_BC = COL_X + D_SSM
PROJ_COLS = COL_BC + 2 * D_BC

NT_DIMS = (((1,), (1,)), ((), ()))
TN_DIMS = (((0,), (0,)), ((), ()))


def _dot(a, b):
    return jnp.dot(a, b, preferred_element_type=F32)


def _dot_nt(a, b):
    return lax.dot_general(a, b, NT_DIMS, preferred_element_type=F32)


def _rms(x, g):
    return x * lax.rsqrt(jnp.mean(x * x, axis=-1, keepdims=True) + EPS) * g


def _silu(x):
    return x * jax.nn.sigmoid(x)


def _softplus(x):
    return jnp.maximum(x, 0.0) + jnp.log1p(jnp.exp(-jnp.abs(x)))


def _expand(x, r):
    hi = x.astype(BF16)
    r1 = x - hi.astype(F32)
    mid = r1.astype(BF16)
    lo = (r1 - mid.astype(F32)).astype(BF16)
    return _dot(hi, r) + _dot(mid, r) + _dot(lo, r)


def _imod(x, n):
    assert n & (n - 1) == 0, "power-of-two divisor expected"
    return x & (n - 1)


def _idiv(x, n):
    assert n & (n - 1) == 0, "power-of-two divisor expected"
    return x >> (n.bit_length() - 1)


def _params(sem):
    return pltpu.CompilerParams(dimension_semantics=sem, vmem_limit_bytes=VMEM_LIMIT_BYTES)


def _row_tile(rows, cap):
    tile = min(rows, cap)
    while rows % tile:
        tile -= SUBLANES
    return tile


def _in_proj_kernel(*refs, rot_tiles, kv_tiles, period_tiles, first_keep, aliased):
    x_ref, g_ref, w_ref, wdt_ref, cos_ref, sin_ref = refs[:6]
    rest = refs[6 + (2 if aliased else 0):]
    o_ref, dt_ref = rest[:2]
    if kv_tiles is None:
        (hn_ref,) = rest[2:]
    else:
        ok_ref, ov_ref, hn_ref = rest[2:]
    n = pl.program_id(1)

    @pl.when(n == 0)
    def _():
        hn = _rms(x_ref[...], g_ref[...]).astype(BF16)
        hn_ref[...] = hn
        row = lax.broadcasted_iota(jnp.int32, wdt_ref.shape, 0)
        wdt = jnp.where(row < N_SSM_HEADS, wdt_ref[...], 0.0).astype(BF16)
        dt_ref[...] = _dot_nt(hn, wdt)

    def project():
        return _dot_nt(hn_ref[...], w_ref[...].astype(BF16))

    @pl.when(n < rot_tiles)
    def _():
        acc = project()
        cos = cos_ref[...]
        sin = sin_ref[...]
        for hh in range(acc.shape[1] // HEAD_DIM):
            xh = acc[:, hh * HEAD_DIM:(hh + 1) * HEAD_DIM]
            o_ref[:, hh * HEAD_DIM:(hh + 1) * HEAD_DIM] = (
                xh * cos + pltpu.roll(xh, HEAD_DIM // 2, axis=1) * sin)

    @pl.when(n >= rot_tiles)
    def _():
        o_ref[...] = project()

    if kv_tiles is not None:
        in_keep = lax.rem(pl.program_id(0), period_tiles) >= first_keep
        rows = o_ref.shape[0]
        for tile, dst in zip(kv_tiles, (ok_ref, ov_ref)):
            @pl.when((n == tile) & in_keep)
            def _(dst=dst):
                for hh in range(N_KV_HEADS):
                    dst[pl.ds(hh, rows, stride=N_KV_HEADS), :] = (
                        o_ref[:, hh * HEAD_DIM:(hh + 1) * HEAD_DIM])


def _in_proj(h, g, w_in_t, layer, cos2, sin2, keep=None, prev_kv=None):
    t, d = h.shape
    depth = w_in_t.shape[0]
    tn = IN_PROJ_COLS
    period = cos2.shape[0]
    tm = _row_tile(period, ROWS_WEIGHT_STREAM)
    assert t % tm == 0 and PROJ_COLS % LANES == 0 and tn == D_KV
    period_tiles = period // tm
    dt_block = PROJ_COLS // LANES
    out_shape = [jax.ShapeDtypeStruct((t, PROJ_COLS), F32), jax.ShapeDtypeStruct((t, LANES), F32)]
    out_specs = [pl.BlockSpec((tm, tn), lambda m, n: (m, n)),
                 pl.BlockSpec((tm, LANES), lambda m, n: (m, 0))]
    kv_tiles, first_keep = None, 0
    if keep is not None:
        assert keep % tm == 0
        kv_tiles = (COL_K // tn, COL_V // tn)
        keep_tiles = keep // tm
        first_keep = period_tiles - keep_tiles

        layer_blocks = (t // period) * keep_tiles

        def kept_block(m, n):
            return (layer * layer_blocks + (m // period_tiles) * keep_tiles
                    + jnp.maximum(m % period_tiles - first_keep, 0), 0)

        win_rows = depth * (t // period) * keep * N_KV_HEADS
        out_shape += [jax.ShapeDtypeStruct((win_rows, HEAD_DIM), F32)] * 2
        out_specs += [pl.BlockSpec((tm * N_KV_HEADS, HEAD_DIM), kept_block)] * 2
    in_specs = [
        pl.BlockSpec((tm, d), lambda m, n: (m, 0)),
        pl.BlockSpec((1, d), lambda m, n: (0, 0)),
        pl.BlockSpec((None, tn, d), lambda m, n: (layer, n, 0)),
        pl.BlockSpec((None, LANES, d), lambda m, n: (layer, dt_block, 0)),
        pl.BlockSpec((tm, HEAD_DIM), lambda m, n: (m % period_tiles, 0)),
        pl.BlockSpec((tm, HEAD_DIM), lambda m, n: (m % period_tiles, 0)),
    ]
    args = [h, g, w_in_t, w_in_t, cos2, sin2]
    aliases = {}
    if prev_kv is not None:
        assert keep is not None
        in_specs += [pl.BlockSpec(memory_space=pl.ANY), pl.BlockSpec(memory_space=pl.ANY)]
        args += list(prev_kv)
        aliases = {6: 2, 7: 3}
    kern = functools.partial(_in_proj_kernel, rot_tiles=(D_ATT + D_KV) // tn, kv_tiles=kv_tiles,
                             period_tiles=period_tiles, first_keep=first_keep,
                             aliased=prev_kv is not None)
    return pl.pallas_call(
        kern,
        out_shape=tuple(out_shape),
        grid=(t // tm, PROJ_COLS // tn),
        in_specs=in_specs,
        out_specs=tuple(out_specs),
        scratch_shapes=[pltpu.VMEM((tm, d), BF16)],
        input_output_aliases=aliases,
        compiler_params=_params(("arbitrary", "arbitrary")),
        name="in_proj",
    )(*args)


ATT_BLOCK = 128
ATT_SPAN = max(w for w, _ in DILATED_BRANCHES)


def _prompt_attn_step(q_ref, k_ref, v_ref, o_ref, scr, w, *, seq, steps_per_unit):
    blk = ATT_BLOCK
    span = ATT_SPAN
    nbr = len(DILATED_BRANCHES)
    ob = [scr[2 * bi] for bi in range(nbr)]
    lb = [scr[2 * bi + 1] for bi in range(nbr)]
    n_spans = seq // span
    steps_per_span = steps_per_unit // n_spans
    assert seq % span == 0 and steps_per_unit % n_spans == 0
    s = _idiv(w, steps_per_span)
    ph = _imod(w, steps_per_span)
    s0 = pl.multiple_of(s * span, span)
    ii = lax.broadcasted_iota(jnp.int32, (blk, 2 * blk), 0)
    jj = lax.broadcasted_iota(jnp.int32, (blk, 2 * blk), 1)
    dist = ii + blk - jj

    def rows(start, dil):
        return pl.ds(start, blk, stride=dil) if dil > 1 else pl.ds(start, blk)

    def piece(ref, start, dil):
        return ref[rows(start, dil), :].astype(BF16)

    def attend(bi, dil, blocks):
        scs = []
        for base, _, kp, kc, _, _, first in blocks:
            kb = jnp.concatenate([kp, kc], axis=0)
            hi = blk if first is None else jnp.where(first, ii, blk)
            mask = (dist >= 0) & (dist <= hi)
            q = piece(q_ref, base, dil)
            scs.append(jnp.where(mask, _dot_nt(q, kb) * ATT_SCALE, NEG_INF))
        sc = jnp.concatenate(scs, axis=0)
        m = jnp.max(sc, axis=1, keepdims=True)
        e = jnp.exp(sc - m)
        ssum = jnp.sum(e, axis=1, keepdims=True)
        p = (e / ssum).astype(BF16)
        lse = jnp.broadcast_to(m + jnp.log(ssum), (len(blocks) * blk, HEAD_DIM))
        for n, (_, local, _, _, vp, vc, _) in enumerate(blocks):
            vb = jnp.concatenate([vp, vc], axis=0)
            ob[bi][rows(local, dil), :] = _dot(p[n * blk:(n + 1) * blk, :], vb)
            lb[bi][rows(local, dil), :] = lse[n * blk:(n + 1) * blk, :]

    for bi, (win, dil) in enumerate(DILATED_BRANCHES):
        assert win // dil == blk and span % (dil * blk) == 0
        stride_rows = blk * dil
        per_class = span // stride_rows
        if per_class >= 2:
            half = per_class // 2
            n_iter = dil * half
        else:
            n_iter = dil // 2
        assert n_iter % steps_per_span == 0
        per_step = n_iter // steps_per_span
        for j in range(per_step):
            idx = ph * per_step + j
            if per_class >= 2:
                r = _idiv(idx, half) if dil > 1 else 0
                i = idx - r * half
                local0 = r + (2 * i) * stride_rows
                base0 = s0 + local0
                first = (s == 0) & (i == 0)
                prev = jnp.maximum(base0 - stride_rows, r)
                km, vm = piece(k_ref, prev, dil), piece(v_ref, prev, dil)
                k0, v0 = piece(k_ref, base0, dil), piece(v_ref, base0, dil)
                k1 = piece(k_ref, base0 + stride_rows, dil)
                v1 = piece(v_ref, base0 + stride_rows, dil)
                attend(bi, dil, [
                    (base0, local0, km, k0, vm, v0, first),
                    (base0 + stride_rows, local0 + stride_rows, k0, k1, v0, v1, None)])
            else:
                blocks = []
                for e in range(2):
                    r = 2 * idx + e
                    base = s0 + r
                    prev = jnp.maximum(base - stride_rows, r)
                    blocks.append((base, r, piece(k_ref, prev, dil), piece(k_ref, base, dil),
                                   piece(v_ref, prev, dil), piece(v_ref, base, dil), s == 0))
                attend(bi, dil, blocks)

    @pl.when(ph == steps_per_span - 1)
    def _():
        rows_per = MERGE_ROWS

        def merge(c, carry):
            loc = pl.multiple_of(c * rows_per, rows_per)
            sl = pl.ds(loc, rows_per)
            ls = [lb[bi][sl, :] for bi in range(nbr)]
            mx = functools.reduce(jnp.maximum, ls)
            ws = [jnp.exp(l - mx) for l in ls]
            num = functools.reduce(lambda a, b: a + b,
                                   [wgt * ob[bi][sl, :] for bi, wgt in enumerate(ws)])
            den = functools.reduce(lambda a, b: a + b, ws)
            o_ref[pl.ds(s0 + loc, rows_per), :] = (num / den).astype(o_ref.dtype)
            return carry

        lax.fori_loop(0, span // rows_per, merge, 0)


def _sample_attn_step(qkv_ref, ck_ref, cv_ref, att_ref, nk_ref, nv_ref, *, n_past, t_new):
    qkv = qkv_ref[...]
    k_new = qkv[:, COL_K:COL_K + D_KV]
    v_new = qkv[:, COL_V:COL_V + D_KV]

    nrow = n_past * N_KV_HEADS
    shift = t_new * N_KV_HEADS
    assert shift % SUBLANES == 0
    step = COPY_ROWS
    for src, new, dst in ((ck_ref, k_new, nk_ref), (cv_ref, v_new, nv_ref)):
        for r0 in range(0, nrow - shift, step):
            n = min(step, nrow - shift - r0)
            dst[r0:r0 + n, :] = src[r0 + shift:r0 + shift + n, :]
        for j in range(t_new):
            for g in range(N_KV_HEADS):
                r = nrow - shift + j * N_KV_HEADS + g
                dst[r:r + 1, :] = new[j:j + 1, g * HEAD_DIM:(g + 1) * HEAD_DIM]

    nq = Q_PER_KV * t_new
    nall = N_KV_HEADS * nq
    vcs, vns, s_c, s_n = [], [], [], [[] for _ in range(t_new)]
    for g in range(N_KV_HEADS):
        qg = jnp.concatenate(
            [qkv[:, (g * Q_PER_KV + r) * HEAD_DIM:(g * Q_PER_KV + r + 1) * HEAD_DIM]
             for r in range(Q_PER_KV)], axis=0).astype(BF16)
        kc = ck_ref[pl.ds(g, n_past, stride=N_KV_HEADS), :].astype(BF16)
        vcs.append(cv_ref[pl.ds(g, n_past, stride=N_KV_HEADS), :].astype(BF16))
        kn = k_new[:, g * HEAD_DIM:(g + 1) * HEAD_DIM].astype(BF16).astype(F32)
        vns.append(v_new[:, g * HEAD_DIM:(g + 1) * HEAD_DIM].astype(BF16).astype(F32))
        s_c.append(_dot_nt(qg, kc))
        qf = qg.astype(F32)
        for j in range(t_new):
            s_n[j].append(jnp.sum(qf * kn[j:j + 1, :], axis=1, keepdims=True))
    s_c = jnp.concatenate(s_c, axis=0) * ATT_SCALE
    s_n = [jnp.concatenate(x, axis=0) * ATT_SCALE for x in s_n]

    cidx = lax.broadcasted_iota(jnp.int32, (nall, n_past), 1)
    tok = _imod(lax.broadcasted_iota(jnp.int32, (nall, n_past), 0), t_new)
    dist_c = n_past + tok - cidx
    tok1 = _imod(lax.broadcasted_iota(jnp.int32, (nall, 1), 0), t_new)

    probs, new_probs, lses = [], [], []
    for win, dil in DILATED_BRANCHES:
        c0 = n_past - win
        assert c0 >= 0 and c0 % LANES == 0
        dist_b = dist_c[:, c0:]
        mask_c = (_imod(dist_b, dil) == 0) & (dist_b <= win)
        sc = jnp.where(mask_c, s_c[:, c0:], NEG_INF)
        m = jnp.max(sc, axis=1, keepdims=True)
        sn = []
        for j in range(t_new):
            dn = tok1 - j
            mask_n = (dn >= 0) & (_imod(dn, dil) == 0)
            snj = jnp.where(mask_n, s_n[j], NEG_INF)
            sn.append(snj)
            m = jnp.maximum(m, snj)
        ec = jnp.exp(sc - m)
        en = [jnp.exp(x - m) for x in sn]
        ssum = jnp.sum(ec, axis=1, keepdims=True)
        for x in en:
            ssum = ssum + x
        probs.append((ec / ssum).astype(BF16))
        new_probs.append([(x / ssum).astype(BF16).astype(F32) for x in en])
        lses.append(m + jnp.log(ssum))

    mx = functools.reduce(jnp.maximum, lses)
    ws = [jnp.exp(l - mx) for l in lses]
    den = functools.reduce(lambda a, b: a + b, ws)
    for g in range(N_KV_HEADS):
        rs = slice(g * nq, (g + 1) * nq)
        o = jnp.zeros((nq, HEAD_DIM), F32)
        for i, wgt in enumerate(ws):
            cols = probs[i].shape[1]
            o_br = _dot(probs[i][rs, :], vcs[g][n_past - cols:, :])
            for j in range(t_new):
                o_br = o_br + new_probs[i][j][rs, :] * vns[g][j:j + 1, :]
            o = o + wgt[rs, :] * o_br
        o = o / den[rs, :]
        for r in range(Q_PER_KV):
            att_ref[:, (g * Q_PER_KV + r) * HEAD_DIM:(g * Q_PER_KV + r + 1) * HEAD_DIM] = (
                o[r * t_new:(r + 1) * t_new, :])


def _attn_kernel(*refs, n_past, t_new, aliased, seq, steps_per_unit):
    if aliased:
        (qkv_ref, ck_ref, cv_ref, q_ref, k_ref, v_ref, _, _,
         att_s_ref, nk_ref, nv_ref, att_p_ref, *scr) = refs
    else:
        (qkv_ref, ck_ref, cv_ref, q_ref, k_ref, v_ref,
         att_s_ref, nk_ref, nv_ref, att_p_ref, *scr) = refs
    _sample_attn_step(qkv_ref, ck_ref, cv_ref, att_s_ref, nk_ref, nv_ref, n_past=n_past, t_new=t_new)
    w = _imod(pl.program_id(0), steps_per_unit)
    _prompt_attn_step(q_ref, k_ref, v_ref, att_p_ref, scr, w, seq=seq, steps_per_unit=steps_per_unit)


def _attn(qkv_s, cache_k, cache_v, proj, layer, prev_k, prev_v, *, batch, seq):
    depth, bsz, nrow, _ = cache_k.shape
    n_past = nrow // N_KV_HEADS
    t_new = qkv_s.shape[1]
    units = batch * N_Q_HEADS
    steps_per_unit = bsz // units
    assert bsz == units * steps_per_unit
    aliased = prev_k is not None
    kern = functools.partial(_attn_kernel, n_past=n_past, t_new=t_new, aliased=aliased, seq=seq,
                             steps_per_unit=steps_per_unit)
    kq = COL_K // HEAD_DIM
    vq = COL_V // HEAD_DIM

    def unit(i):
        u = i // steps_per_unit
        return u // N_Q_HEADS, u % N_Q_HEADS

    win_spec = pl.BlockSpec((None, None, nrow, HEAD_DIM), lambda i: (layer, i, 0, 0))
    in_specs = [
        pl.BlockSpec((None, t_new, qkv_s.shape[2]), lambda i: (i, 0, 0)), win_spec, win_spec,
        pl.BlockSpec((seq, HEAD_DIM), lambda i: unit(i)),
        pl.BlockSpec((seq, HEAD_DIM), lambda i: (unit(i)[0], kq + unit(i)[1] // Q_PER_KV)),
        pl.BlockSpec((seq, HEAD_DIM), lambda i: (unit(i)[0], vq + unit(i)[1] // Q_PER_KV)),
    ]
    args = [qkv_s, cache_k, cache_v, proj, proj, proj]
    aliases = {}
    if aliased:
        in_specs += [pl.BlockSpec(memory_space=pl.ANY), pl.BlockSpec(memory_space=pl.ANY)]
        args += [prev_k, prev_v]
        aliases = {6: 1, 7: 2}
    n_scr = len(DILATED_BRANCHES) * 2
    return pl.pallas_call(
        kern,
        out_shape=(jax.ShapeDtypeStruct((bsz, t_new, D_ATT), F32),
                   jax.ShapeDtypeStruct(cache_k.shape, cache_k.dtype),
                   jax.ShapeDtypeStruct(cache_v.shape, cache_v.dtype),
                   jax.ShapeDtypeStruct((batch * seq, D_ATT), BF16)),
        grid=(bsz,),
        in_specs=in_specs,
        out_specs=(pl.BlockSpec((None, t_new, D_ATT), lambda i: (i, 0, 0)), win_spec, win_spec,
                   pl.BlockSpec((seq, HEAD_DIM), lambda i: unit(i))),
        scratch_shapes=[pltpu.VMEM((ATT_SPAN, HEAD_DIM), F32) for _ in range(n_scr)],
        input_output_aliases=aliases,
        compiler_params=pltpu.CompilerParams(dimension_semantics=("arbitrary",),
                                             vmem_limit_bytes=ATTN_VMEM_LIMIT_BYTES),
        name="attn",
    )(*args)


def _expand_mats():
    r64 = np.zeros((LANES, D_SSM), np.float32)
    r128 = np.zeros((LANES, N_SSM_HEADS * LANES), np.float32)
    for h in range(N_SSM_HEADS):
        r64[h, h * SSM_HEAD_DIM:(h + 1) * SSM_HEAD_DIM] = 1.0
        r128[h, h * LANES:(h + 1) * LANES] = 1.0
    return jnp.asarray(r64, BF16), jnp.asarray(r128, BF16)


def _group_sum_mat():
    g = np.zeros((D_BC, D_SSM), np.float32)
    for grp in range(N_SSM_GROUPS):
        g[grp * D_STATE:(grp + 1) * D_STATE,
          grp * HEADS_PER_GROUP * SSM_HEAD_DIM:(grp + 1) * HEADS_PER_GROUP * SSM_HEAD_DIM] = 1.0
    return jnp.asarray(g, BF16)


def _conv_silu(x, tail, cw, cb):
    n = x.shape[0]
    xp = jnp.concatenate([tail, x], axis=0)
    out = cb + cw[3:4, :] * x
    for w in range(CONV_WIDTH - 1):
        off = SUBLANES - (CONV_WIDTH - 1) + w
        out = out + cw[w:w + 1, :] * xp[off:off + n, :]
    return _silu(out)


def _cumsum_rows(x, seg=None):
    n = x.shape[0]
    rows = lax.broadcasted_iota(jnp.int32, x.shape, 0)
    pos = rows if seg is None else _imod(rows, seg)
    limit = n if seg is None else seg
    sh = 1
    while sh < limit:
        x = x + jnp.where(pos >= sh, pltpu.roll(x, sh, axis=0), 0.0)
        sh *= 2
    return x


def _lane_col_block(row):
    return jnp.broadcast_to(row, (LANES, LANES)).T


def _ssd_chunk(z, xraw, bcraw, dt_raw, tail, h_prev, cw, cb, dtb, alog, dsk, ng, r64, r128):
    cs = SSD_CHUNK
    xs = _conv_silu(xraw, tail[:, :D_SSM], cw[:, :D_SSM], cb[:, :D_SSM])
    bcm = _conv_silu(bcraw, tail[:, D_SSM:], cw[:, D_SSM:], cb[:, D_SSM:])

    dt = _softplus(dt_raw + dtb)
    a = -jnp.exp(alog)
    acum = _cumsum_rows(dt * a)
    acum_t = acum.T
    dt_e = _expand(dt, r64)
    ac_e = _expand(acum, r64)
    col_b = _expand(acum, r128)
    last_e = ac_e[cs - 1:cs, :]
    ea_e = jnp.exp(ac_e)
    xdt = xs * dt_e
    xte = (xdt * jnp.exp(last_e - ac_e)).astype(BF16)

    ti = lax.broadcasted_iota(jnp.int32, (cs, cs), 0)
    si = lax.broadcasted_iota(jnp.int32, (cs, cs), 1)
    causal = ti >= si
    lane = lax.broadcasted_iota(jnp.int32, (cs, LANES), 1)
    lo_half = lane < SSM_HEAD_DIM

    gw = HEADS_PER_GROUP * SSM_HEAD_DIM
    y_diag, y_off, states = [], [], []
    for g in range(N_SSM_GROUPS):
        bg = bcm[:, g * D_STATE:(g + 1) * D_STATE].astype(BF16)
        cg = bcm[:, D_BC + g * D_STATE:D_BC + (g + 1) * D_STATE].astype(BF16)
        cbt = _dot_nt(cg, bg)
        h_in = h_prev[g * gw:(g + 1) * gw, :]
        y_off.append(_dot_nt(cg, h_in.astype(BF16)))
        states.append(lax.dot_general(xte[:, g * gw:(g + 1) * gw], bg, TN_DIMS,
                                      preferred_element_type=F32))
        for k in range(HEADS_PER_GROUP // 2):
            pair = g * (HEADS_PER_GROUP // 2) + k
            xp = xdt[:, pair * LANES:(pair + 1) * LANES]
            yd = jnp.zeros((cs, LANES), F32)
            for e in range(2):
                h = 2 * pair + e
                seg = col_b[:, h * LANES:(h + 1) * LANES] - jnp.broadcast_to(acum_t[h:h + 1, :], (cs, cs))
                dec = jnp.exp(jnp.where(causal, seg, NEG_INF))
                cbh = (cbt * dec).astype(BF16)
                xh = jnp.where(lo_half if e == 0 else jnp.logical_not(lo_half), xp, 0.0).astype(BF16)
                yd = yd + _dot(cbh, xh)
            y_diag.append(yd)

    y = (jnp.concatenate(y_diag, axis=1) + jnp.concatenate(y_off, axis=1) * ea_e + dsk * xs)
    dec_rows = jnp.exp(jnp.concatenate(
        [_lane_col_block(last_e[:, k * LANES:(k + 1) * LANES]) for k in range(D_SSM // LANES)],
        axis=0))
    h_new = h_prev * dec_rows + jnp.concatenate(states, axis=0)
    return _rms(y * _silu(z), ng).astype(BF16), h_new


SSD_CHUNKS_PER_STEP = 4


def _ssd_prompt_kernel(z_ref, xs_ref, bc_ref, dt_ref, cw_ref, cb_ref, dtb_ref, alog_ref, dsk_ref,
                       ng_ref, r64_ref, r128_ref, y_ref, st_ref, cv_ref, h_scr, tail_scr):
    c = pl.program_id(1)
    last = pl.num_programs(1) - 1
    cs = SSD_CHUNK

    @pl.when(c == 0)
    def _():
        h_scr[...] = jnp.zeros_like(h_scr)
        tail_scr[...] = jnp.zeros_like(tail_scr)

    consts = (cw_ref[...], cb_ref[...], dtb_ref[...], alog_ref[...], dsk_ref[...], ng_ref[...],
              r64_ref[...], r128_ref[...])
    h = h_scr[...]
    tail = tail_scr[...]
    for k in range(SSD_CHUNKS_PER_STEP):
        rs = slice(k * cs, (k + 1) * cs)
        xraw = xs_ref[rs, :]
        bcraw = bc_ref[rs, :]
        y, h = _ssd_chunk(z_ref[rs, :], xraw, bcraw, dt_ref[rs, :], tail, h, *consts)
        y_ref[rs, :] = y
        tail = jnp.concatenate([xraw[cs - SUBLANES:, :], bcraw[cs - SUBLANES:, :]], axis=1)
    h_scr[...] = h
    tail_scr[...] = tail

    @pl.when(c == last)
    def _():
        st_ref[...] = h
        cv_ref[...] = tail[SUBLANES - (CONV_WIDTH - 1):, :]


def _ssd_prompt(proj, dt_all, layer, conv_w, conv_b3, dtb, alog, dsk_e, ng3, r64, r128, batch, seq):
    cs = SSD_CHUNK * SSD_CHUNKS_PER_STEP
    assert seq % cs == 0
    nc = seq // cs
    zc, xc, bcc = COL_Z // D_SSM, COL_X // D_SSM, COL_BC // D_SSM
    const2 = lambda b, c: (0, 0)
    return pl.pallas_call(
        _ssd_prompt_kernel,
        out_shape=(jax.ShapeDtypeStruct((batch * seq, D_SSM), BF16),
                   jax.ShapeDtypeStruct((batch, D_SSM, D_STATE), F32),
                   jax.ShapeDtypeStruct((batch, CONV_WIDTH - 1, CONV_DIM), F32)),
        grid=(batch, nc),
        in_specs=[
            pl.BlockSpec((cs, D_SSM), lambda b, c: (b * nc + c, zc)),
            pl.BlockSpec((cs, D_SSM), lambda b, c: (b * nc + c, xc)),
            pl.BlockSpec((cs, D_SSM), lambda b, c: (b * nc + c, bcc)),
            pl.BlockSpec((cs, LANES), lambda b, c: (b * nc + c, 0)),
            pl.BlockSpec((None, CONV_WIDTH, CONV_DIM), lambda b, c: (layer, 0, 0)),
            pl.BlockSpec((None, 1, CONV_DIM), lambda b, c: (layer, 0, 0)),
            pl.BlockSpec((None, 1, LANES), lambda b, c: (layer, 0, 0)),
            pl.BlockSpec((None, 1, LANES), lambda b, c: (layer, 0, 0)),
            pl.BlockSpec((None, 1, D_SSM), lambda b, c: (layer, 0, 0)),
            pl.BlockSpec((None, 1, D_SSM), lambda b, c: (layer, 0, 0)),
            pl.BlockSpec(r64.shape, const2),
            pl.BlockSpec(r128.shape, const2),
        ],
        out_specs=(pl.BlockSpec((cs, D_SSM), lambda b, c: (b * nc + c, 0)),
                   pl.BlockSpec((None, D_SSM, D_STATE), lambda b, c: (b, 0, 0)),
                   pl.BlockSpec((None, CONV_WIDTH - 1, CONV_DIM), lambda b, c: (b, 0, 0))),
        scratch_shapes=[pltpu.VMEM((D_SSM, D_STATE), F32), pltpu.VMEM((SUBLANES, CONV_DIM), F32)],
        compiler_params=_params(("parallel", "arbitrary")),
        name="ssd_prompt",
    )(proj, proj, proj, dt_all, conv_w, conv_b3, dtb, alog, dsk_e, ng3, r64, r128)


def _ssd_sample_kernel(*refs, bt, t_new, aliased):
    if aliased:
        (z_ref, xs_ref, bc_ref, dt_ref, cst_ref, h0_ref, cw_ref, cb_ref, dtb_ref, alog_ref, dsk_ref,
         ng_ref, r64_ref, gs_ref, _, _, y_ref, st_ref, cv_ref, xbc_scr) = refs
    else:
        (z_ref, xs_ref, bc_ref, dt_ref, cst_ref, h0_ref, cw_ref, cb_ref, dtb_ref, alog_ref, dsk_ref,
         ng_ref, r64_ref, gs_ref, y_ref, st_ref, cv_ref, xbc_scr) = refs
    rows = bt * t_new
    kw = CONV_WIDTH - 1
    cw = cw_ref[...]
    cb = cb_ref[...]

    for b in range(bt):
        xb = jnp.concatenate([xs_ref[b * t_new:(b + 1) * t_new, :],
                              bc_ref[b * t_new:(b + 1) * t_new, :]], axis=1)
        xp = jnp.concatenate([cst_ref[b], xb], axis=0)
        out = cb
        for w in range(CONV_WIDTH):
            out = out + cw[w:w + 1, :] * xp[w:w + t_new, :]
        xbc_scr[b * t_new:(b + 1) * t_new, :] = _silu(out)
        cv_ref[b] = xp[t_new:t_new + kw, :]

    xbc = xbc_scr[...]
    xs = xbc[:, :D_SSM]
    bm = xbc[:, D_SSM:D_SSM + D_BC]
    cm = xbc[:, D_SSM + D_BC:]

    tpos = _imod(lax.broadcasted_iota(jnp.int32, (rows, 1), 0), t_new)
    dt = _softplus(dt_ref[...] + dtb_ref[...])
    a = -jnp.exp(alog_ref[...])
    acum = _cumsum_rows(dt * a, seg=t_new)
    r64 = r64_ref[...]
    dt_e = _expand(dt, r64)
    ac_e = _expand(acum, r64)
    v = jnp.where(tpos == t_new - 1, ac_e, 0.0)
    last_e = v
    for d in range(1, t_new):
        last_e = last_e + pltpu.roll(v, rows - d, axis=0)
    ea_e = jnp.exp(ac_e)
    xdt = xs * dt_e
    xte = xdt * jnp.exp(last_e - ac_e)

    cmb = cm.astype(BF16).astype(F32)
    bmb = bm.astype(BF16).astype(F32)
    gs = gs_ref[...]
    y = dsk_ref[...] * xs
    for d in range(t_new):
        if d == 0:
            b_s, x_s, a_s = bmb, xdt, ac_e
        else:
            b_s = pltpu.roll(bmb, d, axis=0)
            x_s = pltpu.roll(xdt, d, axis=0)
            a_s = pltpu.roll(ac_e, d, axis=0)
        cb_e = _expand(cmb * b_s, gs)
        dec = jnp.exp(jnp.where(tpos >= d, ac_e - a_s, NEG_INF))
        y = y + cb_e * dec * x_s

    gw = HEADS_PER_GROUP * SSM_HEAD_DIM
    pad = jnp.zeros((LANES - rows, LANES), F32) if rows < LANES else None
    rowb = _idiv(lax.broadcasted_iota(jnp.int32, (rows, 1), 0), t_new)
    colb = _idiv(lax.broadcasted_iota(jnp.int32, (1, LANES), 1), t_new)
    xte_t = []
    for k in range(D_SSM // LANES):
        blk = xte[:, k * LANES:(k + 1) * LANES]
        if pad is not None:
            blk = jnp.concatenate([blk, pad], axis=0)
        xte_t.append(blk.T)
    e_last = jnp.exp(last_e)
    y_off = [jnp.zeros((rows, gw), F32) for _ in range(N_SSM_GROUPS)]
    for b in range(bt):
        h0 = h0_ref[b]
        new_rows = []
        for g in range(N_SSM_GROUPS):
            cg = cm[:, g * D_STATE:(g + 1) * D_STATE]
            bg = bm[:, g * D_STATE:(g + 1) * D_STATE]
            if pad is not None:
                bg = jnp.concatenate([bg, pad], axis=0)
            bg = bg.astype(BF16)
            h0g = h0[g * gw:(g + 1) * gw, :]
            cgb = jnp.where(rowb == b, cg, 0.0).astype(BF16)
            y_off[g] = y_off[g] + _dot_nt(cgb, h0g.astype(BF16))
            for k in range(gw // LANES):
                blk = g * (gw // LANES) + k
                lhs = jnp.where(colb == b, xte_t[blk], 0.0).astype(BF16)
                st = _dot(lhs, bg)
                r = b * t_new + t_new - 1
                dec = _lane_col_block(e_last[r:r + 1, blk * LANES:(blk + 1) * LANES])
                new_rows.append(h0[blk * LANES:(blk + 1) * LANES, :] * dec + st)
        st_ref[b] = jnp.concatenate(new_rows, axis=0)

    y = y + jnp.concatenate(y_off, axis=1) * ea_e
    y_ref[...] = _rms(y * _silu(z_ref[...]), ng_ref[...]).astype(y_ref.dtype)


def _ssd_sample(proj_s, dt_s, state_conv, state_ssm, layer, conv_w, conv_b3, dtb, alog, dsk_e, ng3,
                r64, gsum, prev_st, prev_cv, *, bt, t_new):
    depth, bsz = state_ssm.shape[:2]
    rows = bt * t_new
    zc, xc, bcc = COL_Z // D_SSM, COL_X // D_SSM, COL_BC // D_SSM
    aliased = prev_st is not None
    kern = functools.partial(_ssd_sample_kernel, bt=bt, t_new=t_new, aliased=aliased)
    const2 = lambda i: (0, 0)
    in_specs = [
        pl.BlockSpec((rows, D_SSM), lambda i: (i, zc)),
        pl.BlockSpec((rows, D_SSM), lambda i: (i, xc)),
        pl.BlockSpec((rows, D_SSM), lambda i: (i, bcc)),
        pl.BlockSpec((rows, LANES), lambda i: (i, 0)),
        pl.BlockSpec((None, bt, CONV_WIDTH - 1, CONV_DIM), lambda i: (layer, i, 0, 0)),
        pl.BlockSpec((None, bt, D_SSM, D_STATE), lambda i: (layer, i, 0, 0)),
        pl.BlockSpec((None, CONV_WIDTH, CONV_DIM), lambda i: (layer, 0, 0)),
        pl.BlockSpec((None, 1, CONV_DIM), lambda i: (layer, 0, 0)),
        pl.BlockSpec((None, 1, LANES), lambda i: (layer, 0, 0)),
        pl.BlockSpec((None, 1, LANES), lambda i: (layer, 0, 0)),
        pl.BlockSpec((None, 1, D_SSM), lambda i: (layer, 0, 0)),
        pl.BlockSpec((None, 1, D_SSM), lambda i: (layer, 0, 0)),
        pl.BlockSpec(r64.shape, const2),
        pl.BlockSpec(gsum.shape, const2),
    ]
    args = [proj_s, proj_s, proj_s, dt_s, state_conv, state_ssm, conv_w, conv_b3, dtb, alog, dsk_e,
            ng3, r64, gsum]
    aliases = {}
    if aliased:
        in_specs += [pl.BlockSpec(memory_space=pl.ANY), pl.BlockSpec(memory_space=pl.ANY)]
        args += [prev_st, prev_cv]
        aliases = {14: 1, 15: 2}
    return pl.pallas_call(
        kern,
        out_shape=(jax.ShapeDtypeStruct((bsz * t_new, D_SSM), BF16),
                   jax.ShapeDtypeStruct(state_ssm.shape, state_ssm.dtype),
                   jax.ShapeDtypeStruct(state_conv.shape, state_conv.dtype)),
        grid=(bsz // bt,),
        in_specs=in_specs,
        out_specs=(pl.BlockSpec((rows, D_SSM), lambda i: (i, 0)),
                   pl.BlockSpec((None, bt, D_SSM, D_STATE), lambda i: (layer, i, 0, 0)),
                   pl.BlockSpec((None, bt, CONV_WIDTH - 1, CONV_DIM), lambda i: (layer, i, 0, 0))),
        scratch_shapes=[pltpu.VMEM((rows, CONV_DIM), F32)],
        input_output_aliases=aliases,
        compiler_params=_params(("parallel",)),
        name="ssd_sample",
    )(*args)


def _resident(shape, index_map):
    return pl.BlockSpec(shape, index_map, pipeline_mode=pl.Buffered(1))


def _out_proj_kernel(att_ref, ssm_ref, h_ref, w_ref, o_ref, wb_ref):
    @pl.when(pl.program_id(0) == 0)
    def _():
        wb_ref[...] = w_ref[...].astype(BF16)

    acc = _dot(att_ref[...], wb_ref[:D_ATT, :]) + _dot(ssm_ref[...], wb_ref[D_ATT:, :])
    o_ref[...] = h_ref[...] + acc


def _out_proj(att, ssm, h, w_out, layer):
    t, d = h.shape
    tm = _row_tile(t, ROWS_WEIGHT_RESIDENT)
    return pl.pallas_call(
        _out_proj_kernel,
        out_shape=jax.ShapeDtypeStruct((t, d), F32),
        grid=(t // tm,),
        in_specs=[
            pl.BlockSpec((tm, D_ATT), lambda m: (m, 0)),
            pl.BlockSpec((tm, D_SSM), lambda m: (m, 0)),
            pl.BlockSpec((tm, d), lambda m: (m, 0)),
            _resident((None,) + w_out.shape[1:], lambda m: (layer, 0, 0)),
        ],
        out_specs=pl.BlockSpec((tm, d), lambda m: (m, 0)),
        scratch_shapes=[pltpu.VMEM(w_out.shape[1:], BF16)],
        compiler_params=_params(("arbitrary",)),
        name="out_proj",
    )(att, ssm, h, w_out)


def _ffn_kernel(h_ref, g_ref, wg_ref, wu_ref, wd_ref, o_ref, hf_ref):
    f = pl.program_id(1)

    def contribution(hf):
        act = _silu(_dot(hf, wg_ref[...].astype(BF16))) * _dot(hf, wu_ref[...].astype(BF16))
        return _dot(act.astype(BF16), wd_ref[...].astype(BF16))

    @pl.when(f == 0)
    def _():
        h = h_ref[...]
        hf = _rms(h, g_ref[...]).astype(BF16)
        hf_ref[...] = hf
        o_ref[...] = h + contribution(hf)

    @pl.when(f > 0)
    def _():
        o_ref[...] += contribution(hf_ref[...])


def _ffn(h, g, wg, wu, wd, layer):
    t, d = h.shape
    dff = wg.shape[2]
    tm = _row_tile(t, ROWS_WEIGHT_STREAM)
    tf = FFN_COLS if tm > ROWS_WEIGHT_RESIDENT else FFN_COLS_FEW_ROWS
    assert dff % tf == 0
    return pl.pallas_call(
        _ffn_kernel,
        out_shape=jax.ShapeDtypeStruct((t, d), F32),
        grid=(t // tm, dff // tf),
        in_specs=[
            pl.BlockSpec((tm, d), lambda m, f: (m, 0), pipeline_mode=pl.Buffered(1)),
            pl.BlockSpec((1, d), lambda m, f: (0, 0)),
            pl.BlockSpec((None, d, tf), lambda m, f: (layer, 0, f)),
            pl.BlockSpec((None, d, tf), lambda m, f: (layer, 0, f)),
            pl.BlockSpec((None, tf, d), lambda m, f: (layer, f, 0)),
        ],
        out_specs=pl.BlockSpec((tm, d), lambda m, f: (m, 0)),
        scratch_shapes=[pltpu.VMEM((tm, d), BF16)],
        compiler_params=_params(("parallel", "arbitrary")),
        name="ffn",
    )(h, g, wg, wu, wd)


def _ple_kernel(h_ref, p_ref, g_ref, wg_ref, wp_ref, gf_ref, o_ref, wgb_ref, wpb_ref, *, final, tn):
    @pl.when(pl.program_id(0) == 0)
    def _():
        wgb_ref[...] = wg_ref[...].astype(BF16)
        wpb_ref[...] = wp_ref[...].astype(BF16)

    hn = _rms(h_ref[...], g_ref[...]).astype(BF16)
    pb = p_ref[...].astype(BF16)
    for c in range(h_ref.shape[1] // tn):
        cols = slice(c * tn, (c + 1) * tn)
        gate = jax.nn.sigmoid(_dot(hn, wgb_ref[:, cols]))
        o_ref[:, cols] = h_ref[:, cols] + gate * _dot(pb, wpb_ref[:, cols])
    if final:
        o_ref[...] = _rms(o_ref[...], gf_ref[...])


def _ple(h, p, g, wg, wp, gf, layer, *, final):
    t, d = h.shape
    tm = _row_tile(t, ROWS_WEIGHT_RESIDENT)
    kern = functools.partial(_ple_kernel, final=final, tn=PLE_COLS)
    return pl.pallas_call(
        kern,
        out_shape=jax.ShapeDtypeStruct((t, d), F32),
        grid=(t // tm,),
        in_specs=[
            pl.BlockSpec((tm, d), lambda m: (m, 0)),
            pl.BlockSpec((None, tm, p.shape[2]), lambda m: (layer, m, 0)),
            pl.BlockSpec((1, d), lambda m: (0, 0)),
            _resident((None,) + wg.shape[1:], lambda m: (layer, 0, 0)),
            _resident((None,) + wp.shape[1:], lambda m: (layer, 0, 0)),
            pl.BlockSpec((1, d), lambda m: (0, 0)),
        ],
        out_specs=pl.BlockSpec((tm, d), lambda m: (m, 0)),
        scratch_shapes=[pltpu.VMEM(wg.shape[1:], BF16), pltpu.VMEM(wp.shape[1:], BF16)],
        compiler_params=_params(("arbitrary",)),
        name="ple",
    )(h, p, g, wg, wp, gf)


def _rope_tables(pos):
    half = HEAD_DIM // 2
    inv_freq = ROPE_THETA ** (-jnp.arange(half, dtype=F32) / half)
    ang = pos.astype(F32)[:, None] * inv_freq[None, :]
    cos, sin = jnp.cos(ang), jnp.sin(ang)
    return jnp.concatenate([cos, cos], axis=1), jnp.concatenate([-sin, sin], axis=1)


def _pad_lanes(x):
    return jnp.pad(x, [(0, 0)] * (x.ndim - 1) + [(0, LANES - x.shape[-1])])


def kernel(x_prompt, x_sample, cache_k, cache_v, state_ssm, state_conv, p_prompt, p_sample,
           norm_mix_g, w_in, conv_w, conv_b, dt_bias, a_log, d_skip, ssm_norm_g, w_out,
           norm_ffn_g, w_ffn_gate, w_ffn_up, w_ffn_down, norm_ple_g, w_ple_gate, w_ple_proj,
           final_norm_g):
    batch, seq, d = x_prompt.shape
    dec_batch, dec_seq, _ = x_sample.shape
    depth = w_in.shape[0]
    n_past = cache_k.shape[2]
    tp = batch * seq
    ts = dec_batch * dec_seq
    assert w_in.shape[2] == PROJ_COLS + N_SSM_HEADS and n_past == PAST_LEN

    cos_p, sin_p = _rope_tables(jnp.arange(seq, dtype=jnp.int32))
    cos_s, sin_s = _rope_tables(jnp.tile(PAST_LEN + jnp.arange(dec_seq, dtype=jnp.int32), dec_batch))
    r64, r128 = _expand_mats()
    w_in_t = jnp.swapaxes(w_in, 1, 2)
    gsum = _group_sum_mat()

    ck = cache_k.reshape(depth, dec_batch, n_past * N_KV_HEADS, HEAD_DIM)
    cv = cache_v.reshape(depth, dec_batch, n_past * N_KV_HEADS, HEAD_DIM)
    st_in = state_ssm.reshape(depth, dec_batch, D_SSM, D_STATE)
    pp = p_prompt.reshape(depth, tp, -1)
    ps = p_sample.reshape(depth, ts, -1)
    conv_b3 = conv_b.reshape(depth, 1, CONV_DIM)
    dtb = _pad_lanes(dt_bias).reshape(depth, 1, LANES)
    alog = _pad_lanes(a_log).reshape(depth, 1, LANES)
    dsk_e = jnp.repeat(d_skip, SSM_HEAD_DIM, axis=1).reshape(depth, 1, D_SSM)
    ng3 = ssm_norm_g.reshape(depth, 1, D_SSM)
    gf = final_norm_g.reshape(1, d)

    hp = x_prompt.reshape(tp, d)
    hs = x_sample.reshape(ts, d)
    nk_s = nv_s = st_s = cv_s = None
    kv_p = None
    st_p, cv_p = [], []
    for i in range(depth):
        g_mix = norm_mix_g[i].reshape(1, d)
        g_ffn = norm_ffn_g[i].reshape(1, d)
        g_ple = norm_ple_g[i].reshape(1, d)
        last = i == depth - 1

        keep = min(ATT_WINDOW, seq)
        proj, dt_p, *kv_p = _in_proj(hp, g_mix, w_in_t, i, cos_p, sin_p, keep=keep, prev_kv=kv_p)
        proj_s, dt_s = _in_proj(hs, g_mix, w_in_t, i, cos_s, sin_s)
        qkv_s = proj_s[:, :COL_Z].reshape(dec_batch, dec_seq, COL_Z)
        att_s, nk_s, nv_s, att_p = _attn(qkv_s, ck, cv, proj, i, nk_s, nv_s, batch=batch, seq=seq)

        y_p, st_i, cv_i = _ssd_prompt(proj, dt_p, i, conv_w, conv_b3, dtb, alog, dsk_e, ng3,
                                      r64, r128, batch, seq)
        hp = _out_proj(att_p, y_p, hp, w_out, i)
        hp = _ffn(hp, g_ffn, w_ffn_gate, w_ffn_up, w_ffn_down, i)
        hp = _ple(hp, pp, g_ple, w_ple_gate, w_ple_proj, gf, i, final=last)

        st_p.append(st_i.reshape(batch, N_SSM_HEADS, SSM_HEAD_DIM, D_STATE))
        cv_p.append(cv_i)

        y_s, st_s, cv_s = _ssd_sample(proj_s, dt_s, state_conv, st_in, i, conv_w, conv_b3,
                                      dtb, alog, dsk_e, ng3, r64, gsum, st_s, cv_s,
                                      bt=SSD_SAMPLE_SEQS, t_new=dec_seq)
        hs = _out_proj(att_s.reshape(ts, D_ATT).astype(BF16), y_s, hs, w_out, i)
        hs = _ffn(hs, g_ffn, w_ffn_gate, w_ffn_up, w_ffn_down, i)
        hs = _ple(hs, ps, g_ple, w_ple_gate, w_ple_proj, gf, i, final=last)

    return (hp.reshape(batch, seq, d), hs.reshape(dec_batch, dec_seq, d),
            kv_p[0].reshape(depth, batch, keep, N_KV_HEADS, HEAD_DIM),
            kv_p[1].reshape(depth, batch, keep, N_KV_HEADS, HEAD_DIM),
            jnp.stack(st_p), jnp.stack(cv_p),
            nk_s.reshape(cache_k.shape), nv_s.reshape(cache_v.shape),
            st_s.reshape(state_ssm.shape), cv_s)
```
